```python
import jax, jax.numpy as jnp
from jax import lax
import numpy as np

D_MODEL = 1024
BATCH = 2
SEQ = 8192
DEPTH = 1
DEC_BATCH = 128
DEC_SEQ = 8
PAST_LEN = 16384
PAGE_SIZE = 128

ATT_HEADS = 8
ATT_KV_HEADS = 2
ATT_GROUP = ATT_HEADS // ATT_KV_HEADS
ATT_HEAD_DIM = 64
ATT_WIDTH = ATT_HEADS * ATT_HEAD_DIM
KV_WIDTH = ATT_KV_HEADS * ATT_HEAD_DIM
WINDOW = 128
ATT_BLOCK = 128
ROPE_THETA = 10000.0
GLA_HEADS = 4
GLA_WIDTH = D_MODEL // 2
GLA_DV = GLA_WIDTH // GLA_HEADS
GLA_KEY_WIDTH = GLA_WIDTH // 2
GLA_DK = GLA_KEY_WIDTH // GLA_HEADS
GLA_GATE_RANK = 16
GLA_GATE_TAU = 16.0
GLA_CHUNK = 16
EPS = 1e-6
NEG = -1e30
IN_COLS = 2 * ATT_WIDTH + 2 * KV_WIDTH + 2 * GLA_KEY_WIDTH + 2 * GLA_WIDTH + GLA_GATE_RANK + 2 * D_MODEL

kernel_name = "hybrid_swa_sink_gla_parallel_adaln_step"


def rms_norm(x, g):
    xf = x.astype(jnp.float32)
    y = xf * lax.rsqrt(jnp.mean(xf * xf, axis=-1, keepdims=True) + EPS)
    return (y * g.astype(jnp.float32)).astype(x.dtype)


def rope(x, pos):
    half = ATT_HEAD_DIM // 2
    inv = 1.0 / (ROPE_THETA ** (jnp.arange(half, dtype=jnp.float32) / half))
    ang = pos.astype(jnp.float32)[:, None] * inv[None, :]
    cos = jnp.cos(ang)[:, None, :]
    sin = jnp.sin(ang)[:, None, :]
    xf = x.astype(jnp.float32)
    x1, x2 = xf[..., :half], xf[..., half:]
    return jnp.concatenate([x1 * cos - x2 * sin, x2 * cos + x1 * sin], axis=-1).astype(x.dtype)


def split_proj(p):
    sizes = (ATT_WIDTH, KV_WIDTH, KV_WIDTH, ATT_WIDTH, GLA_KEY_WIDTH, GLA_KEY_WIDTH,
             GLA_WIDTH, GLA_GATE_RANK, GLA_WIDTH, D_MODEL, D_MODEL)
    idx = np.cumsum(sizes)[:-1].tolist()
    return jnp.split(p, idx, axis=-1)


def sink_attention(q, k, v, mask, sinks):
    s = jnp.einsum('...qhgd,...khd->...hgqk', q, k).astype(jnp.float32) * (ATT_HEAD_DIM ** -0.5)
    s = jnp.where(mask, s, NEG)
    sink = jnp.broadcast_to(sinks.astype(jnp.float32).reshape(ATT_KV_HEADS, ATT_GROUP, 1, 1),
                            s.shape[:-1] + (1,))
    p = jax.nn.softmax(jnp.concatenate([s, sink], axis=-1), axis=-1)[..., :-1]
    return jnp.einsum('...hgqk,...khd->...qhgd', p.astype(v.dtype), v)


def swa_prompt(q, k, v, sinks):
    B, S = q.shape[:2]
    nb = S // ATT_BLOCK
    qb = q.reshape(B, nb, ATT_BLOCK, ATT_KV_HEADS, ATT_GROUP, ATT_HEAD_DIM)
    kb = k.reshape(B, nb, ATT_BLOCK, ATT_KV_HEADS, ATT_HEAD_DIM)
    vb = v.reshape(B, nb, ATT_BLOCK, ATT_KV_HEADS, ATT_HEAD_DIM)
    pad = ((0, 0), (1, 0), (0, 0), (0, 0), (0, 0))
    kk = jnp.concatenate([jnp.pad(kb, pad)[:, :-1], kb], axis=2)
    vv = jnp.concatenate([jnp.pad(vb, pad)[:, :-1], vb], axis=2)
    qpos = jnp.arange(ATT_BLOCK)[:, None] + ATT_BLOCK
    kpos = jnp.arange(2 * ATT_BLOCK)[None, :]
    d = qpos - kpos
    band = (d >= 0) & (d <= WINDOW)
    blk = jnp.arange(nb)[:, None, None]
    mask = band[None] & ((kpos[None] >= ATT_BLOCK) | (blk > 0))
    o = sink_attention(qb, kk, vv, mask[None, :, None, None], sinks)
    return o.reshape(B, S, ATT_HEADS, ATT_HEAD_DIM)


def swa_sample(q, k_new, v_new, k_cache, v_cache, sinks):
    B, L = q.shape[:2]
    W = k_cache.shape[1]
    kk = jnp.concatenate([k_cache.astype(k_new.dtype), k_new], axis=1)
    vv = jnp.concatenate([v_cache.astype(v_new.dtype), v_new], axis=1)
    d = (W + jnp.arange(L))[:, None] - jnp.arange(W + L)[None, :]
    mask = (d >= 0) & (d <= WINDOW)
    qg = q.reshape(B, L, ATT_KV_HEADS, ATT_GROUP, ATT_HEAD_DIM)
    o = sink_attention(qg, kk, vv, mask, sinks).reshape(B, L, ATT_HEADS, ATT_HEAD_DIM)
    return o, kk[:, -W:], vv[:, -W:]


def gla_recurrence(q, k, v, log_a, S0):
    B, L = q.shape[:2]
    C = GLA_CHUNK
    pad = (-L) % C
    nc = (L + pad) // C

    def chunks(t):
        t = jnp.pad(t.astype(jnp.float32), ((0, 0), (0, pad), (0, 0), (0, 0)))
        return t.reshape(B, nc, C, t.shape[2], t.shape[3]).transpose(1, 0, 2, 3, 4)

    tri = (jnp.arange(C)[:, None] >= jnp.arange(C)[None, :])[None, :, :, None, None]

    def step(S, inp):
        qc, kc, vc, ac = inp
        b = jnp.cumsum(ac, axis=1)
        o_inter = jnp.einsum('bthk,bhkv->bthv', qc * jnp.exp(b), S)
        diff = jnp.where(tri, b[:, :, None] - b[:, None, :], NEG)
        A = jnp.einsum('bthk,bshk,btshk->bhts', qc, kc, jnp.exp(diff))
        o_intra = jnp.einsum('bhts,bshv->bthv', A, vc)
        b_last = b[:, -1]
        k_dec = kc * jnp.exp(b_last[:, None] - b)
        S_new = jnp.exp(b_last)[..., None] * S + jnp.einsum('bshk,bshv->bhkv', k_dec, vc)
        return S_new, o_inter + o_intra

    S_fin, o = lax.scan(step, S0.astype(jnp.float32), (chunks(q), chunks(k), chunks(v), chunks(log_a)))
    o = o.transpose(1, 0, 2, 3, 4).reshape(B, nc * C, GLA_HEADS, GLA_DV)[:, :L]
    return o, S_fin


def trunk_layer(x, c, pos0, win_k, win_v, gla_s, norm_g, w_ada, b_ada, w_in, q_norm_g, k_norm_g,
                attn_sinks, w_gla_gate, b_gla_gate, gla_norm_g, w_branch_att, w_branch_gla, w_out):
    B, L, _ = x.shape
    mod = jnp.einsum('bd,de->be', jax.nn.silu(c), w_ada) + b_ada
    shift, scale, gate = jnp.split(mod, 3, axis=-1)
    h = rms_norm(x, norm_g) * (1.0 + scale[:, None]) + shift[:, None]
    proj = jnp.einsum('bld,de->ble', h, w_in)
    q_a, k_a, v_a, z_a, q_g, k_g, v_g, lr_g, z_g, m_a, m_g = split_proj(proj)

    pos = pos0 + jnp.arange(L, dtype=jnp.int32)
    q_a = rope(rms_norm(q_a.reshape(B, L, ATT_HEADS, ATT_HEAD_DIM), q_norm_g), pos)
    k_a = rope(rms_norm(k_a.reshape(B, L, ATT_KV_HEADS, ATT_HEAD_DIM), k_norm_g), pos)
    v_a = v_a.reshape(B, L, ATT_KV_HEADS, ATT_HEAD_DIM)
    if win_k is None:
        o_a = swa_prompt(q_a, k_a, v_a, attn_sinks)
        new_k, new_v = k_a[:, -WINDOW:], v_a[:, -WINDOW:]
    else:
        o_a, new_k, new_v = swa_sample(q_a, k_a, v_a, win_k, win_v, attn_sinks)
    o_a = o_a.reshape(B, L, ATT_WIDTH) * jax.nn.silu(z_a)

    log_a = jax.nn.log_sigmoid((jnp.einsum('blr,rk->blk', lr_g, w_gla_gate) + b_gla_gate).astype(jnp.float32)) / GLA_GATE_TAU
    q_g = q_g.reshape(B, L, GLA_HEADS, GLA_DK) * (GLA_DK ** -0.5)
    k_g = k_g.reshape(B, L, GLA_HEADS, GLA_DK)
    v_g = v_g.reshape(B, L, GLA_HEADS, GLA_DV)
    o_g, new_s = gla_recurrence(q_g, k_g, v_g, log_a.reshape(B, L, GLA_HEADS, GLA_DK), gla_s)
    o_g = rms_norm(o_g, gla_norm_g).astype(x.dtype).reshape(B, L, GLA_WIDTH) * jax.nn.silu(z_g)

    merged = (jax.nn.sigmoid(m_a) * jnp.einsum('ble,ed->bld', o_a, w_branch_att)
              + jax.nn.sigmoid(m_g) * jnp.einsum('ble,ed->bld', o_g, w_branch_gla))
    y = x + gate[:, None] * jnp.einsum('bld,de->ble', merged, w_out)
    return y, new_k, new_v, new_s


def setup_inputs(seed: int = 0) -> dict:
    key = jax.random.key(seed)
    ks = jax.random.split(key, 24)
    f32 = jnp.float32
    win = min(WINDOW, PAST_LEN)

    def nrm(k, shape, s=1.0):
        return jax.random.normal(k, shape, f32) * s

    return {
        'x_prompt': nrm(ks[0], (BATCH, SEQ, D_MODEL)),
        'x_sample': nrm(ks[1], (DEC_BATCH, DEC_SEQ, D_MODEL)),
        'cache_win_k': nrm(ks[2], (DEPTH, DEC_BATCH, win, ATT_KV_HEADS, ATT_HEAD_DIM)),
        'cache_win_v': nrm(ks[3], (DEPTH, DEC_BATCH, win, ATT_KV_HEADS, ATT_HEAD_DIM)),
        'state_gla': nrm(ks[4], (DEPTH, DEC_BATCH, GLA_HEADS, GLA_DK, GLA_DV)),
        'c_prompt': nrm(ks[5], (BATCH, D_MODEL)),
        'c_sample': nrm(ks[6], (DEC_BATCH, D_MODEL)),
        'norm_g': 1.0 + nrm(ks[7], (DEPTH, D_MODEL), 0.02),
        'w_ada': nrm(ks[8], (DEPTH, D_MODEL, 3 * D_MODEL), D_MODEL ** -0.5),
        'b_ada': nrm(ks[9], (DEPTH, 3 * D_MODEL), 0.02),
        'w_in': nrm(ks[10], (DEPTH, D_MODEL, IN_COLS), D_MODEL ** -0.5),
        'q_norm_g': 1.0 + nrm(ks[11], (DEPTH, ATT_HEAD_DIM), 0.02),
        'k_norm_g': 1.0 + nrm(ks[12], (DEPTH, ATT_HEAD_DIM), 0.02),
        'attn_sinks': nrm(ks[13], (DEPTH, ATT_HEADS), 0.5),
        'w_gla_gate': nrm(ks[14], (DEPTH, GLA_GATE_RANK, GLA_KEY_WIDTH), GLA_GATE_RANK ** -0.5),
        'b_gla_gate': nrm(ks[15], (DEPTH, GLA_KEY_WIDTH), 0.02),
        'gla_norm_g': 1.0 + nrm(ks[16], (DEPTH, GLA_DV), 0.02),
        'w_branch_att': nrm(ks[17], (DEPTH, ATT_WIDTH, D_MODEL), ATT_WIDTH ** -0.5),
        'w_branch_gla': nrm(ks[18], (DEPTH, GLA_WIDTH, D_MODEL), GLA_WIDTH ** -0.5),
        'w_out': nrm(ks[19], (DEPTH, D_MODEL, D_MODEL), D_MODEL ** -0.5),
    }


def reference(x_prompt, x_sample, cache_win_k, cache_win_v, state_gla, c_prompt, c_sample,
              norm_g, w_ada, b_ada, w_in, q_norm_g, k_norm_g, attn_sinks, w_gla_gate, b_gla_gate,
              gla_norm_g, w_branch_att, w_branch_gla, w_out):
    y_prompt, y_sample = x_prompt, x_sample
    pk, pv, ps, sk, sv, ss = [], [], [], [], [], []
    for l in range(DEPTH):
        weights = (norm_g[l], w_ada[l], b_ada[l], w_in[l], q_norm_g[l], k_norm_g[l], attn_sinks[l],
                   w_gla_gate[l], b_gla_gate[l], gla_norm_g[l], w_branch_att[l], w_branch_gla[l], w_out[l])
        s0 = jnp.zeros((BATCH, GLA_HEADS, GLA_DK, GLA_DV), jnp.float32)
        y_prompt, k1, v1, s1 = trunk_layer(y_prompt, c_prompt, 0, None, None, s0, *weights)
        y_sample, k2, v2, s2 = trunk_layer(y_sample, c_sample, PAST_LEN, cache_win_k[l], cache_win_v[l],
                                           state_gla[l], *weights)
        pk.append(k1); pv.append(v1); ps.append(s1)
        sk.append(k2); sv.append(v2); ss.append(s2)
    prompt_win_k = jnp.stack(pk)
    prompt_win_v = jnp.stack(pv)
    prompt_gla_state = jnp.stack(ps)
    sample_win_k = jnp.stack(sk)
    sample_win_v = jnp.stack(sv)
    sample_gla_state = jnp.stack(ss)
    return (y_prompt, y_sample, prompt_win_k, prompt_win_v, prompt_gla_state, sample_win_k, sample_win_v, sample_gla_state)
```

```python
import functools

import jax
import jax.numpy as jnp
from jax import lax
from jax.experimental import pallas as pl
from jax.experimental.pallas import tpu as pltpu

F32 = jnp.float32
BF16 = jnp.bfloat16

D_MODEL = 1024
BATCH = 2
SEQ = 8192
DEC_BATCH = 128
DEC_SEQ = 8
PAST_LEN = 16384
ATT_HEADS = 8
ATT_KV_HEADS = 2
ATT_GROUP = ATT_HEADS // ATT_KV_HEADS
ATT_HEAD_DIM = 64
ATT_WIDTH = ATT_HEADS * ATT_HEAD_DIM
KV_WIDTH = ATT_KV_HEADS * ATT_HEAD_DIM
WINDOW = 128
ATT_BLOCK = 128
ROPE_THETA = 10000.0
GLA_HEADS = 4
GLA_WIDTH = D_MODEL // 2
GLA_DV = GLA_WIDTH // GLA_HEADS
GLA_KEY_WIDTH = GLA_WIDTH // 2
GLA_DK = GLA_KEY_WIDTH // GLA_HEADS
GLA_GATE_RANK = 16
GLA_GATE_TAU = 16.0
EPS = 1e-6
NEG = -1e30

LANES = 128
HALF = LANES // 2
C_Q, C_K, C_V, C_ZA = 0, 512, 640, 768
C_QG, C_KG, C_VG, C_ZG = 1280, 1536, 1792, 2304
C_MA, C_MG, C_LR, C_END = 2816, 3840, 4864, 4992

T_PROMPT = 256
GLA_BLOCK = 64
SEQ_GROUP = 16
T_SAMPLE = SEQ_GROUP * DEC_SEQ
MOD_PAD = 16
MOD_TILE = 512
VMEM_LIMIT = 56 * 1024 * 1024


def _sigmoid(x):
    return 1.0 / (1.0 + jnp.exp(-x))


def _dot(a, b):
    return jnp.dot(a, b, preferred_element_type=F32)


def _dot_nt(a, b):
    return lax.dot_general(a, b, (((1,), (1,)), ((), ())), preferred_element_type=F32)


def _dot_tn(a, b):
    return lax.dot_general(a, b, (((0,), (0,)), ((), ())), preferred_element_type=F32)


def _split3(x):
    hi = x.astype(BF16)
    r1 = x - hi.astype(F32)
    mid = r1.astype(BF16)
    lo = (r1 - mid.astype(F32)).astype(BF16)
    return hi, mid, lo


def _dot_exact_lhs01(m01, x):
    hi, mid, lo = _split3(x)
    return _dot(m01, hi) + _dot(m01, mid) + _dot(m01, lo)


def _modulated_norm(x, shift, scale, norm_g):
    ms = jnp.mean(x * x, axis=-1, keepdims=True)
    xn = x * lax.rsqrt(ms + EPS) * norm_g
    return xn * (1.0 + scale) + shift


def _norm_rope_slab(xs, g, cos, sin):
    lane = lax.broadcasted_iota(jnp.int32, xs.shape, 1)
    lo = lane < HALF
    first = (lane & (HALF // 2)) == 0
    sq = xs * xs
    s_lo = jnp.sum(jnp.where(lo, sq, 0.0), axis=-1, keepdims=True)
    s_hi = jnp.sum(jnp.where(lo, 0.0, sq), axis=-1, keepdims=True)
    inv = jnp.where(lo, lax.rsqrt(s_lo * (1.0 / ATT_HEAD_DIM) + EPS),
                    lax.rsqrt(s_hi * (1.0 / ATT_HEAD_DIM) + EPS))
    xn = xs * inv * g
    swapped = jnp.where(first, pltpu.roll(xn, LANES - HALF // 2, 1), pltpu.roll(xn, HALF // 2, 1))
    return xn * cos + swapped * sin


def _log_decay(lr, wgate_ref, bgate_ref):
    pre = _dot(lr.astype(BF16), wgate_ref[...]) + bgate_ref[...]
    log_sig = jnp.minimum(pre, 0.0) - jnp.log1p(jnp.exp(-jnp.abs(pre)))
    return log_sig * (1.0 / GLA_GATE_TAU)


def _head_masks(rows):
    lane = lax.broadcasted_iota(jnp.int32, (rows, GLA_KEY_WIDTH), 1)
    return [(lane >= h * GLA_DK) & (lane < (h + 1) * GLA_DK) for h in range(GLA_HEADS)]


def _layer_tail(x, gate, o_a, z_a, o_g, z_g, m_a, m_g, glang_ref, wba_ref, wbg_ref, wout_ref):
    o_a = o_a * (z_a * _sigmoid(z_a))
    slabs = []
    for h in range(GLA_HEADS):
        oh = o_g[:, h * GLA_DV:(h + 1) * GLA_DV]
        ms = jnp.mean(oh * oh, axis=-1, keepdims=True)
        slabs.append(oh * lax.rsqrt(ms + EPS) * glang_ref[...])
    o_g = jnp.concatenate(slabs, axis=1) * (z_g * _sigmoid(z_g))
    merged = (_sigmoid(m_a) * _dot(o_a.astype(BF16), wba_ref[...])
              + _sigmoid(m_g) * _dot(o_g.astype(BF16), wbg_ref[...]))
    return x + gate * _dot(merged.astype(BF16), wout_ref[...])


def _mod_body(c_ref, w_ref, b_ref, o_ref, act_ref):
    @pl.when(pl.program_id(0) == 0)
    def _():
        c = c_ref[...]
        act_ref[...] = (c * _sigmoid(c)).astype(BF16)

    o_ref[...] = _dot(act_ref[...], w_ref[...].astype(BF16)) + b_ref[...]


def _modulation(c_all, w_ada, b_ada):
    rows = c_all.shape[0]
    return pl.pallas_call(
        _mod_body,
        grid=(3 * D_MODEL // MOD_TILE,),
        in_specs=[
            pl.BlockSpec((rows, D_MODEL), lambda i: (0, 0)),
            pl.BlockSpec((D_MODEL, MOD_TILE), lambda i: (0, i)),
            pl.BlockSpec((1, MOD_TILE), lambda i: (0, i)),
        ],
        out_specs=pl.BlockSpec((rows, MOD_TILE), lambda i: (0, i)),
        out_shape=jax.ShapeDtypeStruct((rows, 3 * D_MODEL), F32),
        scratch_shapes=[pltpu.VMEM((rows, D_MODEL), BF16)],
        compiler_params=pltpu.CompilerParams(dimension_semantics=("arbitrary",),
                                             vmem_limit_bytes=VMEM_LIMIT),
        name="adaln_mod",
    )(c_all, w_ada, b_ada)


def _attend_block(q_slabs, kcat, vcat, mask, sink_ref):
    lane = lax.broadcasted_iota(jnp.int32, (ATT_BLOCK, LANES), 1)
    lo = lane < HALF
    pieces = ([jnp.where(lo, s, 0.0).astype(BF16) for s in q_slabs]
              + [jnp.where(lo, 0.0, s).astype(BF16) for s in q_slabs])
    qstack = jnp.concatenate(pieces, axis=0)
    s = _dot_nt(qstack, kcat)
    probs, denoms = [], []
    for h in range(ATT_HEADS):
        sh = jnp.where(mask, s[h * ATT_BLOCK:(h + 1) * ATT_BLOCK], NEG)
        sink = sink_ref[h]
        m = jnp.maximum(jnp.max(sh, axis=-1, keepdims=True), sink)
        e = jnp.exp(sh - m)
        denoms.append(jnp.sum(e, axis=-1, keepdims=True) + jnp.exp(sink - m))
        probs.append(e.astype(BF16))
    o = _dot(jnp.concatenate(probs, axis=0), vcat)
    out = []
    for j in range(ATT_GROUP):
        o_lo = o[j * ATT_BLOCK:(j + 1) * ATT_BLOCK] * (1.0 / denoms[j])
        o_hi = o[(j + ATT_GROUP) * ATT_BLOCK:(j + ATT_GROUP + 1) * ATT_BLOCK] * (1.0 / denoms[j + ATT_GROUP])
        out.append(jnp.where(lo, o_lo, o_hi))
    return out


def _gla_block(qg, kg, vg, la, s_t, tri, hmasks, causal):
    b = _dot_exact_lhs01(tri, la)
    b_last = b[GLA_BLOCK - 1:GLA_BLOCK]
    q_hat = qg * jnp.exp(b)
    k_til = (kg * jnp.exp(-b)).astype(BF16)
    k_dec = kg * jnp.exp(b_last - b)
    qstack = jnp.concatenate([jnp.where(hm, q_hat, 0.0) for hm in hmasks], axis=0).astype(BF16)
    rhs = jnp.concatenate([s_t.astype(BF16), k_til], axis=0)
    r = _dot_nt(qstack, rhs)
    o_inter = r[:, :GLA_DV]
    a = jnp.where(causal, r[:, GLA_DV:], 0.0).astype(BF16)
    outs = []
    upd = jnp.zeros_like(s_t)
    for h in range(GLA_HEADS):
        vh = vg[:, h * GLA_DV:(h + 1) * GLA_DV].astype(BF16)
        rows = slice(h * GLA_BLOCK, (h + 1) * GLA_BLOCK)
        outs.append(o_inter[rows] + _dot(a[rows], vh))
        upd = upd + _dot_tn(vh, jnp.where(hmasks[h], k_dec, 0.0).astype(BF16))
    return jnp.concatenate(outs, axis=1), s_t * jnp.exp(b_last) + upd


def _prompt_body(x_ref, mod_ref, cos_ref, sin_ref, ng_ref, win_ref, qng_ref, kng_ref, sink_ref,
                 wgate_ref, bgate_ref, glang_ref, wba_ref, wbg_ref, wout_ref,
                 y_ref, wk_ref, wv_ref, st_ref,
                 kprev_ref, vprev_ref, st_scr):
    j = pl.program_id(1)

    @pl.when(j == 0)
    def _():
        kprev_ref[...] = jnp.zeros_like(kprev_ref)
        vprev_ref[...] = jnp.zeros_like(vprev_ref)
        st_scr[...] = jnp.zeros_like(st_scr)

    x = x_ref[...]
    shift, scale, gate = mod_ref[0:1, :], mod_ref[1:2, :], mod_ref[2:3, :]
    h = _modulated_norm(x, shift, scale, ng_ref[...]).astype(BF16)

    def proj(a, b):
        return _dot(h, win_ref[:, a:b])

    cos, sin = cos_ref[...], sin_ref[...]
    q = proj(C_Q, C_K)
    q_slabs = [_norm_rope_slab(q[:, i * LANES:(i + 1) * LANES], qng_ref[...], cos, sin)
               * (ATT_HEAD_DIM ** -0.5) for i in range(ATT_GROUP)]
    k = _norm_rope_slab(proj(C_K, C_V), kng_ref[...], cos, sin)
    v = proj(C_V, C_ZA)
    wk_ref[...] = k[T_PROMPT - WINDOW:]
    wv_ref[...] = v[T_PROMPT - WINDOW:]
    k16, v16 = k.astype(BF16), v.astype(BF16)

    r = lax.broadcasted_iota(jnp.int32, (ATT_BLOCK, 2 * ATT_BLOCK), 0)
    c = lax.broadcasted_iota(jnp.int32, (ATT_BLOCK, 2 * ATT_BLOCK), 1)
    band = ((c < ATT_BLOCK) & (c >= r)) | ((c >= ATT_BLOCK) & (c - ATT_BLOCK <= r))
    first_mask = band & ((c >= ATT_BLOCK) | (j > 0))

    o_rows = []
    for i in range(T_PROMPT // ATT_BLOCK):
        rows = slice(i * ATT_BLOCK, (i + 1) * ATT_BLOCK)
        if i == 0:
            kp, vp, mask = kprev_ref[...], vprev_ref[...], first_mask
        else:
            prev = slice((i - 1) * ATT_BLOCK, i * ATT_BLOCK)
            kp, vp, mask = k16[prev], v16[prev], band
        kcat = jnp.concatenate([kp, k16[rows]], axis=0)
        vcat = jnp.concatenate([vp, v16[rows]], axis=0)
        o_rows.append(jnp.concatenate(_attend_block([s[rows] for s in q_slabs], kcat, vcat, mask, sink_ref),
                                      axis=1))
    o_a = jnp.concatenate(o_rows, axis=0)
    kprev_ref[...] = k16[T_PROMPT - ATT_BLOCK:]
    vprev_ref[...] = v16[T_PROMPT - ATT_BLOCK:]

    qg = proj(C_QG, C_KG) * (GLA_DK ** -0.5)
    kg = proj(C_KG, C_VG)
    vg = proj(C_VG, C_ZG)
    la = _log_decay(proj(C_LR, C_END), wgate_ref, bgate_ref)
    tr = lax.broadcasted_iota(jnp.int32, (GLA_BLOCK, GLA_BLOCK), 0)
    tc = lax.broadcasted_iota(jnp.int32, (GLA_BLOCK, GLA_BLOCK), 1)
    tri = (tr >= tc).astype(BF16)
    sr = lax.broadcasted_iota(jnp.int32, (GLA_HEADS * GLA_BLOCK, GLA_BLOCK), 0)
    sc = lax.broadcasted_iota(jnp.int32, (GLA_HEADS * GLA_BLOCK, GLA_BLOCK), 1)
    causal = (sr & (GLA_BLOCK - 1)) >= sc
    hmasks = _head_masks(GLA_BLOCK)
    s_t = st_scr[...]
    og_rows = []
    for i in range(T_PROMPT // GLA_BLOCK):
        rows = slice(i * GLA_BLOCK, (i + 1) * GLA_BLOCK)
        o_blk, s_t = _gla_block(qg[rows], kg[rows], vg[rows], la[rows], s_t, tri, hmasks, causal)
        og_rows.append(o_blk)
    st_scr[...] = s_t
    o_g = jnp.concatenate(og_rows, axis=0)

    @pl.when(j == pl.num_programs(1) - 1)
    def _():
        st_ref[...] = s_t.T

    y_ref[...] = _layer_tail(x, gate, o_a, proj(C_ZA, C_QG), o_g, proj(C_ZG, C_MA),
                             proj(C_MA, C_MG), proj(C_MG, C_LR), glang_ref, wba_ref, wbg_ref, wout_ref)


def _const_spec(shape, nargs):
    zeros = (0,) * len(shape)
    if nargs == 1:
        return pl.BlockSpec(shape, lambda i: zeros)
    return pl.BlockSpec(shape, lambda b, j: zeros)


def _weight_specs(nargs):
    smem = pl.BlockSpec(memory_space=pltpu.SMEM)
    return [
        _const_spec((1, D_MODEL), nargs),
        _const_spec((D_MODEL, C_END), nargs),
        _const_spec((1, LANES), nargs),
        _const_spec((1, LANES), nargs),
        smem,
        _const_spec((LANES, GLA_KEY_WIDTH), nargs),
        _const_spec((1, GLA_KEY_WIDTH), nargs),
        _const_spec((1, GLA_DV), nargs),
        _const_spec((ATT_WIDTH, D_MODEL), nargs),
        _const_spec((GLA_WIDTH, D_MODEL), nargs),
        _const_spec((D_MODEL, D_MODEL), nargs),
    ]


def _prompt_layer(x, mod, cos, sin, weights):
    nblk = SEQ // T_PROMPT
    return pl.pallas_call(
        _prompt_body,
        grid=(BATCH, nblk),
        in_specs=[
            pl.BlockSpec((None, T_PROMPT, D_MODEL), lambda b, j: (b, j, 0)),
            pl.BlockSpec((None, 3, D_MODEL), lambda b, j: (b, 0, 0)),
            pl.BlockSpec((T_PROMPT, LANES), lambda b, j: (j, 0)),
            pl.BlockSpec((T_PROMPT, LANES), lambda b, j: (j, 0)),
        ] + _weight_specs(2),
        out_specs=[
            pl.BlockSpec((None, T_PROMPT, D_MODEL), lambda b, j: (b, j, 0)),
            pl.BlockSpec((None, WINDOW, KV_WIDTH), lambda b, j: (b, 0, 0)),
            pl.BlockSpec((None, WINDOW, KV_WIDTH), lambda b, j: (b, 0, 0)),
            pl.BlockSpec((None, GLA_KEY_WIDTH, GLA_DV), lambda b, j: (b, 0, 0)),
        ],
        out_shape=[
            jax.ShapeDtypeStruct((BATCH, SEQ, D_MODEL), F32),
            jax.ShapeDtypeStruct((BATCH, WINDOW, KV_WIDTH), F32),
            jax.ShapeDtypeStruct((BATCH, WINDOW, KV_WIDTH), F32),
            jax.ShapeDtypeStruct((BATCH, GLA_KEY_WIDTH, GLA_DV), F32),
        ],
        scratch_shapes=[
            pltpu.VMEM((ATT_BLOCK, KV_WIDTH), BF16),
            pltpu.VMEM((ATT_BLOCK, KV_WIDTH), BF16),
            pltpu.VMEM((GLA_DV, GLA_KEY_WIDTH), F32),
        ],
        compiler_params=pltpu.CompilerParams(dimension_semantics=("arbitrary", "arbitrary"),
                                             vmem_limit_bytes=VMEM_LIMIT),
        name="prompt_layer",
    )(x, mod, cos, sin, *weights)


def _sample_body(x_ref, mod_ref, cos_ref, sin_ref, ng_ref, win_ref, qng_ref, kng_ref, sink_ref,
                 wgate_ref, bgate_ref, glang_ref, wba_ref, wbg_ref, wout_ref,
                 ck_ref, cv_ref, s0_ref,
                 y_ref, ok_ref, ov_ref, s1_ref,
                 qrows_scr, knew_scr, vnew_scr, qhat_scr, kdec_scr, vg_scr, la3_scr, oa_scr, oi_scr):
    G, L, T = SEQ_GROUP, DEC_SEQ, T_SAMPLE
    x = x_ref[...]
    shift = mod_ref[:, 0:D_MODEL]
    scale = mod_ref[:, D_MODEL:2 * D_MODEL]
    gate = mod_ref[:, 2 * D_MODEL:3 * D_MODEL]
    h = _modulated_norm(x, shift, scale, ng_ref[...]).astype(BF16)

    def proj(a, b):
        return _dot(h, win_ref[:, a:b])

    cos, sin = cos_ref[...], sin_ref[...]
    lane = lax.broadcasted_iota(jnp.int32, (T, LANES), 1)
    lo = lane < HALF
    q = proj(C_Q, C_K)
    for i in range(ATT_GROUP):
        s = _norm_rope_slab(q[:, i * LANES:(i + 1) * LANES], qng_ref[...], cos, sin) * (ATT_HEAD_DIM ** -0.5)
        qrows_scr[:, i * L:(i + 1) * L, :] = jnp.where(lo, s, 0.0).reshape(G, L, LANES)
        qrows_scr[:, (i + ATT_GROUP) * L:(i + ATT_GROUP + 1) * L, :] = jnp.where(lo, 0.0, s).reshape(G, L, LANES)
    knew_scr[...] = _norm_rope_slab(proj(C_K, C_V), kng_ref[...], cos, sin).reshape(G, L, KV_WIDTH)
    vnew_scr[...] = proj(C_V, C_ZA).reshape(G, L, KV_WIDTH)

    qg = proj(C_QG, C_KG) * (GLA_DK ** -0.5)
    kg = proj(C_KG, C_VG)
    vg = proj(C_VG, C_ZG)
    la = _log_decay(proj(C_LR, C_END), wgate_ref, bgate_ref)
    tr = lax.broadcasted_iota(jnp.int32, (T, T), 0)
    tc = lax.broadcasted_iota(jnp.int32, (T, T), 1)
    same_seq = (tr // L) == (tc // L)
    causal = same_seq & (tc <= tr)
    hi, mid, lo3 = _split3(la)
    tri = causal.astype(BF16)
    blk = same_seq.astype(BF16)
    b = _dot(tri, hi) + _dot(tri, mid) + _dot(tri, lo3)
    b_last = _dot(blk, hi) + _dot(blk, mid) + _dot(blk, lo3)
    q_hat = qg * jnp.exp(b)
    k_til = (kg * jnp.exp(-b)).astype(BF16)
    k_dec = kg * jnp.exp(b_last - b)
    hmasks = _head_masks(T)
    zeros_l = jnp.zeros((G, L, GLA_KEY_WIDTH), F32)
    for hh in range(GLA_HEADS):
        qhat_scr[:, hh * L:(hh + 1) * L, :] = jnp.where(hmasks[hh], q_hat, 0.0).reshape(G, L, GLA_KEY_WIDTH)
    kdec_scr[:, 0:L, :] = k_dec.reshape(G, L, GLA_KEY_WIDTH)
    kdec_scr[:, L:2 * L, :] = zeros_l
    vg_scr[:, 0:L, :] = vg.reshape(G, L, GLA_WIDTH)
    vg_scr[:, L:2 * L, :] = jnp.zeros((G, L, GLA_WIDTH), F32)
    la3_scr[:, 0:L, :] = hi.astype(F32).reshape(G, L, GLA_KEY_WIDTH)
    la3_scr[:, L:2 * L, :] = mid.astype(F32).reshape(G, L, GLA_KEY_WIDTH)
    la3_scr[:, 2 * L:3 * L, :] = lo3.astype(F32).reshape(G, L, GLA_KEY_WIDTH)
    la3_scr[:, 3 * L:4 * L, :] = zeros_l

    nrow = ATT_HEADS * L
    rr = lax.broadcasted_iota(jnp.int32, (nrow, 2 * WINDOW), 0) & (L - 1)
    cc = lax.broadcasted_iota(jnp.int32, (nrow, 2 * WINDOW), 1)
    att_mask = (cc >= rr) & (cc <= WINDOW + rr)
    sink_col = jnp.concatenate([jnp.full((L, 1), sink_ref[hh], F32) for hh in range(ATT_HEADS)], axis=0)
    lo8 = lax.broadcasted_iota(jnp.int32, (L, LANES), 1) < HALF
    kv_pad = jnp.zeros((WINDOW - L, KV_WIDTH), F32)
    ones16 = jnp.ones((4 * L, GLA_DV), BF16)

    def seq_body(n, carry):
        kc, vc = ck_ref[n], cv_ref[n]
        kn, vn = knew_scr[n], vnew_scr[n]
        ok_ref[n, 0:WINDOW - L, :] = kc[L:]
        ok_ref[n, WINDOW - L:WINDOW, :] = kn
        ov_ref[n, 0:WINDOW - L, :] = vc[L:]
        ov_ref[n, WINDOW - L:WINDOW, :] = vn
        kcat = jnp.concatenate([kc, kn, kv_pad], axis=0).astype(BF16)
        vcat = jnp.concatenate([vc, vn, kv_pad], axis=0).astype(BF16)
        s = jnp.where(att_mask, _dot_nt(qrows_scr[n].astype(BF16), kcat), NEG)
        m = jnp.maximum(jnp.max(s, axis=-1, keepdims=True), sink_col)
        e = jnp.exp(s - m)
        denom = jnp.sum(e, axis=-1, keepdims=True) + jnp.exp(sink_col - m)
        o = _dot(e.astype(BF16), vcat) * (1.0 / denom)
        oa_scr[n] = jnp.concatenate(
            [jnp.where(lo8, o[i * L:(i + 1) * L], o[(i + ATT_GROUP) * L:(i + ATT_GROUP + 1) * L])
             for i in range(ATT_GROUP)], axis=1)

        s0 = s0_ref[n]
        oi = _dot(qhat_scr[n].astype(BF16), s0.astype(BF16))
        oi_scr[n] = jnp.concatenate([oi[hh * L:(hh + 1) * L] for hh in range(GLA_HEADS)], axis=1)
        kv = _dot_tn(kdec_scr[n].astype(BF16), vg_scr[n].astype(BF16))
        upd = jnp.concatenate([kv[hh * GLA_DK:(hh + 1) * GLA_DK, hh * GLA_DV:(hh + 1) * GLA_DV]
                               for hh in range(GLA_HEADS)], axis=0)
        b_col = _dot_tn(la3_scr[n].astype(BF16), ones16)
        s1_ref[n] = jnp.exp(b_col) * s0 + upd
        return carry

    lax.fori_loop(0, G, seq_body, 0)

    o_a = oa_scr[...].reshape(T, ATT_WIDTH)
    o_inter = oi_scr[...].reshape(T, GLA_WIDTH)
    intra = []
    for hh in range(GLA_HEADS):
        a = _dot_nt(jnp.where(hmasks[hh], q_hat, 0.0).astype(BF16), k_til)
        a = jnp.where(causal, a, 0.0).astype(BF16)
        intra.append(_dot(a, vg[:, hh * GLA_DV:(hh + 1) * GLA_DV].astype(BF16)))
    o_g = o_inter + jnp.concatenate(intra, axis=1)

    y_ref[...] = _layer_tail(x, gate, o_a, proj(C_ZA, C_QG), o_g, proj(C_ZG, C_MA),
                             proj(C_MA, C_MG), proj(C_MG, C_LR), glang_ref, wba_ref, wbg_ref, wout_ref)


def _sample_layer(x, mod, cos, sin, weights, ck, cv, s0):
    G, L, T = SEQ_GROUP, DEC_SEQ, T_SAMPLE
    row_spec = lambda w: pl.BlockSpec((T, w), lambda i: (i, 0))
    seq_spec = lambda a, b: pl.BlockSpec((G, a, b), lambda i: (i, 0, 0))
    return pl.pallas_call(
        _sample_body,
        grid=(DEC_BATCH // G,),
        in_specs=[row_spec(D_MODEL), row_spec(3 * D_MODEL),
                  _const_spec((T, LANES), 1), _const_spec((T, LANES), 1)]
                 + _weight_specs(1)
                 + [seq_spec(WINDOW, KV_WIDTH), seq_spec(WINDOW, KV_WIDTH), seq_spec(GLA_KEY_WIDTH, GLA_DV)],
        out_specs=[row_spec(D_MODEL), seq_spec(WINDOW, KV_WIDTH), seq_spec(WINDOW, KV_WIDTH),
                   seq_spec(GLA_KEY_WIDTH, GLA_DV)],
        out_shape=[
            jax.ShapeDtypeStruct((DEC_BATCH * L, D_MODEL), F32),
            jax.ShapeDtypeStruct((DEC_BATCH, WINDOW, KV_WIDTH), F32),
            jax.ShapeDtypeStruct((DEC_BATCH, WINDOW, KV_WIDTH), F32),
            jax.ShapeDtypeStruct((DEC_BATCH, GLA_KEY_WIDTH, GLA_DV), F32),
        ],
        scratch_shapes=[
            pltpu.VMEM((G, ATT_HEADS * L, LANES), F32),
            pltpu.VMEM((G, L, KV_WIDTH), F32),
            pltpu.VMEM((G, L, KV_WIDTH), F32),
            pltpu.VMEM((G, GLA_HEADS * L, GLA_KEY_WIDTH), F32),
            pltpu.VMEM((G, 2 * L, GLA_KEY_WIDTH), F32),
            pltpu.VMEM((G, 2 * L, GLA_WIDTH), F32),
            pltpu.VMEM((G, 4 * L, GLA_KEY_WIDTH), F32),
            pltpu.VMEM((G, L, ATT_WIDTH), F32),
            pltpu.VMEM((G, L, GLA_WIDTH), F32),
        ],
        compiler_params=pltpu.CompilerParams(dimension_semantics=("arbitrary",),
                                             vmem_limit_bytes=VMEM_LIMIT),
        name="sample_layer",
    )(x, mod, cos, sin, *weights, ck, cv, s0)


def _rope_tables(pos):
    half = ATT_HEAD_DIM // 2
    inv = 1.0 / (ROPE_THETA ** (jnp.arange(half, dtype=F32) / half))
    ang = pos.astype(F32)[:, None] * inv[None, :]
    c, s = jnp.cos(ang), jnp.sin(ang)
    return jnp.tile(c, (1, 4)), jnp.concatenate([-s, s, -s, s], axis=1)


def _pair_heads(w, axis):
    shape = w.shape
    w = w.reshape(shape[:axis] + (ATT_KV_HEADS, ATT_GROUP, ATT_HEAD_DIM) + shape[axis + 1:])
    w = jnp.swapaxes(w, axis, axis + 1)
    return w.reshape(shape)


def kernel(x_prompt, x_sample, cache_win_k, cache_win_v, state_gla, c_prompt, c_sample, norm_g, w_ada, b_ada, w_in, q_norm_g, k_norm_g, attn_sinks, w_gla_gate, b_gla_gate, gla_norm_g, w_branch_att, w_branch_gla, w_out):
    assert w_in.shape[0] == 1, "single-layer trunk"
    w = w_in[0]
    lr_end = C_ZG + GLA_GATE_RANK
    w_r = jnp.concatenate([
        _pair_heads(w[:, 0:ATT_WIDTH], 1), w[:, C_K:C_ZA], _pair_heads(w[:, C_ZA:C_QG], 1),
        w[:, C_QG:C_ZG], w[:, lr_end:], w[:, C_ZG:lr_end],
        jnp.zeros((D_MODEL, C_END - C_LR - GLA_GATE_RANK), F32)], axis=1).astype(BF16)
    wgate = jnp.concatenate([w_gla_gate[0], jnp.zeros((LANES - GLA_GATE_RANK, GLA_KEY_WIDTH), F32)],
                            axis=0).astype(BF16)
    weights = (
        norm_g[0][None, :], w_r,
        jnp.tile(q_norm_g[0], 2)[None, :], jnp.tile(k_norm_g[0], 2)[None, :],
        attn_sinks[0], wgate, b_gla_gate[0][None, :], gla_norm_g[0][None, :],
        _pair_heads(w_branch_att[0], 0).astype(BF16), w_branch_gla[0].astype(BF16), w_out[0].astype(BF16),
    )

    c_all = jnp.concatenate([c_prompt, jnp.zeros((MOD_PAD - BATCH, D_MODEL), F32),
                             jnp.repeat(c_sample, DEC_SEQ, axis=0)], axis=0)
    mod = _modulation(c_all, w_ada[0], b_ada[0][None, :])
    mod_prompt = mod[:BATCH].reshape(BATCH, 3, D_MODEL)
    mod_sample = mod[MOD_PAD:]

    cos_p, sin_p = _rope_tables(jnp.arange(SEQ, dtype=jnp.int32))
    cos_s, sin_s = _rope_tables(PAST_LEN + jnp.arange(DEC_SEQ, dtype=jnp.int32))
    cos_s, sin_s = jnp.tile(cos_s, (SEQ_GROUP, 1)), jnp.tile(sin_s, (SEQ_GROUP, 1))

    y_p, wk_p, wv_p, st_p = _prompt_layer(x_prompt, mod_prompt, cos_p, sin_p, weights)
    y_s, wk_s, wv_s, st_s = _sample_layer(
        x_sample.reshape(DEC_BATCH * DEC_SEQ, D_MODEL), mod_sample, cos_s, sin_s, weights,
        cache_win_k[0].reshape(DEC_BATCH, WINDOW, KV_WIDTH),
        cache_win_v[0].reshape(DEC_BATCH, WINDOW, KV_WIDTH),
        state_gla[0].reshape(DEC_BATCH, GLA_KEY_WIDTH, GLA_DV))

    kv_shape = (1, -1, WINDOW, ATT_KV_HEADS, ATT_HEAD_DIM)
    st_shape = (1, -1, GLA_HEADS, GLA_DK, GLA_DV)
    return (y_p, y_s.reshape(DEC_BATCH, DEC_SEQ, D_MODEL),
            wk_p.reshape(kv_shape), wv_p.reshape(kv_shape), st_p.reshape(st_shape),
            wk_s.reshape(kv_shape), wv_s.reshape(kv_shape), st_s.reshape(st_shape))
```

```python
import numpy as np

import jax
import jax.numpy as jnp
from jax import lax
from jax.experimental import pallas as pl
from jax.experimental.pallas import tpu as pltpu

F32 = jnp.float32
BF16 = jnp.bfloat16

D_MODEL = 1024
BATCH = 2
SEQ = 8192
DEC_BATCH = 128
DEC_SEQ = 8
PAST_LEN = 16384
ATT_HEADS = 8
ATT_KV_HEADS = 2
ATT_GROUP = ATT_HEADS // ATT_KV_HEADS
ATT_HEAD_DIM = 64
ATT_WIDTH = ATT_HEADS * ATT_HEAD_DIM
KV_WIDTH = ATT_KV_HEADS * ATT_HEAD_DIM
WINDOW = 128
ATT_BLOCK = 128
ROPE_THETA = 10000.0
GLA_HEADS = 4
GLA_WIDTH = D_MODEL // 2
GLA_DV = GLA_WIDTH // GLA_HEADS
GLA_KEY_WIDTH = GLA_WIDTH // 2
GLA_DK = GLA_KEY_WIDTH // GLA_HEADS
GLA_GATE_RANK = 16
GLA_GATE_TAU = 16.0
EPS = 1e-6
NEG = -1e30

LANES = 128
HALF = LANES // 2
R_K, R_V, R_ZA = ATT_WIDTH, ATT_WIDTH + KV_WIDTH, ATT_WIDTH + 2 * KV_WIDTH
R_QG = R_ZA + ATT_WIDTH
R_KG = R_QG + GLA_KEY_WIDTH
R_VG = R_KG + GLA_KEY_WIDTH
R_LR = R_VG + GLA_WIDTH
R_ZG = R_LR + GLA_GATE_RANK
R_MA = R_ZG + GLA_WIDTH
R_MG = R_MA + D_MODEL
R_END = R_MG + D_MODEL

T_PROMPT = 256
GLA_BLOCK = 64
SEQ_GROUP = 16
T_SAMPLE = SEQ_GROUP * DEC_SEQ
MOD_ROWS = DEC_BATCH + 16
MOD_TILE = 512
VMEM_LIMIT = 56 * 1024 * 1024


def _sigmoid(x):
    return 1.0 / (1.0 + jnp.exp(-x))


def _dot(a, b):
    return jnp.dot(a, b, preferred_element_type=F32)


def _dot_nt(a, b):
    return lax.dot_general(a, b, (((1,), (1,)), ((), ())), preferred_element_type=F32)


def _dot_tn(a, b):
    return lax.dot_general(a, b, (((0,), (0,)), ((), ())), preferred_element_type=F32)


def _split3(x):
    hi = x.astype(BF16)
    r1 = x - hi.astype(F32)
    mid = r1.astype(BF16)
    lo = (r1 - mid.astype(F32)).astype(BF16)
    return hi, mid, lo


def _norm_rope_slab(xs, g, cos, sin):
    lane = lax.broadcasted_iota(jnp.int32, xs.shape, 1)
    lo = lane < HALF
    first = (lane & (HALF // 2)) == 0
    sq = xs * xs
    s_lo = jnp.sum(jnp.where(lo, sq, 0.0), axis=-1, keepdims=True)
    s_hi = jnp.sum(jnp.where(lo, 0.0, sq), axis=-1, keepdims=True)
    inv = jnp.where(lo, lax.rsqrt(s_lo * (1.0 / ATT_HEAD_DIM) + EPS),
                    lax.rsqrt(s_hi * (1.0 / ATT_HEAD_DIM) + EPS))
    xn = xs * inv * g
    swapped = jnp.where(first, pltpu.roll(xn, LANES - HALF // 2, 1), pltpu.roll(xn, HALF // 2, 1))
    return xn * cos + swapped * sin


def _log_decay(lrz, wgate_ref, bgate_ref):
    pre = _dot(lrz.astype(BF16), wgate_ref[...]) + bgate_ref[...]
    log_sig = jnp.minimum(pre, 0.0) - jnp.log1p(jnp.exp(-jnp.abs(pre)))
    return log_sig * (1.0 / GLA_GATE_TAU)


def _head_masks(rows):
    lane = lax.broadcasted_iota(jnp.int32, (rows, GLA_KEY_WIDTH), 1)
    return [(lane >= h * GLA_DK) & (lane < (h + 1) * GLA_DK) for h in range(GLA_HEADS)]


class _Proj:
    def __init__(self, h, wq_ref, wza_ref, wt_ref):
        self.h, self.wq_ref, self.wza_ref, self.wt_ref = h, wq_ref, wza_ref, wt_ref

    def rows(self, a, b):
        return _dot_nt(self.h, self.wt_ref[a:b, :])

    def q_att(self):
        return _dot_nt(self.h, self.wq_ref[...])

    def z_att(self):
        return _dot_nt(self.h, self.wza_ref[...])


def _layer_tail(x, gate, o_a, o_g, p, glang_ref, wba_ref, wbg_ref, wout_ref):
    z_a = p.z_att()
    o_a = o_a * (z_a * _sigmoid(z_a))
    slabs = []
    for h in range(GLA_HEADS):
        oh = o_g[:, h * GLA_DV:(h + 1) * GLA_DV]
        ms = jnp.mean(oh * oh, axis=-1, keepdims=True)
        slabs.append(oh * lax.rsqrt(ms + EPS) * glang_ref[...])
    z_g = p.rows(R_ZG, R_MA)
    o_g = jnp.concatenate(slabs, axis=1) * (z_g * _sigmoid(z_g))
    merged = (_sigmoid(p.rows(R_MA, R_MG)) * _dot(o_a.astype(BF16), wba_ref[...])
              + _sigmoid(p.rows(R_MG, R_END)) * _dot(o_g.astype(BF16), wbg_ref[...]))
    return x + gate * _dot(merged.astype(BF16), wout_ref[...])


def _mod_body(c_ref, w_ref, b_ref, o_ref, act_ref):
    @pl.when(pl.program_id(0) == 0)
    def _():
        c = c_ref[...]
        act_ref[...] = (c * _sigmoid(c)).astype(BF16)

    o_ref[...] = _dot(act_ref[...], w_ref[...].astype(BF16)) + b_ref[...]


def _modulation(c_all, w_ada, b_ada):
    rows = c_all.shape[0]
    return pl.pallas_call(
        _mod_body,
        grid=(3 * D_MODEL // MOD_TILE,),
        in_specs=[
            pl.BlockSpec((rows, D_MODEL), lambda i: (0, 0)),
            pl.BlockSpec((D_MODEL, MOD_TILE), lambda i: (0, i)),
            pl.BlockSpec((1, MOD_TILE), lambda i: (0, i)),
        ],
        out_specs=pl.BlockSpec((rows, MOD_TILE), lambda i: (0, i)),
        out_shape=jax.ShapeDtypeStruct((rows, 3 * D_MODEL), F32),
        scratch_shapes=[pltpu.VMEM((rows, D_MODEL), BF16)],
        compiler_params=pltpu.CompilerParams(dimension_semantics=("arbitrary",),
                                             vmem_limit_bytes=VMEM_LIMIT),
        name="adaln_mod",
    )(c_all, w_ada, b_ada)


def _attend_block(q_slabs, kcat, vcat, mask, sink_ref):
    lane = lax.broadcasted_iota(jnp.int32, (ATT_BLOCK, LANES), 1)
    lo = lane < HALF
    pieces = ([jnp.where(lo, s, 0.0).astype(BF16) for s in q_slabs]
              + [jnp.where(lo, 0.0, s).astype(BF16) for s in q_slabs])
    qstack = jnp.concatenate(pieces, axis=0)
    s = _dot_nt(qstack, kcat)
    probs, denoms = [], []
    for h in range(ATT_HEADS):
        sh = jnp.where(mask, s[h * ATT_BLOCK:(h + 1) * ATT_BLOCK], NEG)
        sink = sink_ref[h]
        m = jnp.maximum(jnp.max(sh, axis=-1, keepdims=True), sink)
        e = jnp.exp(sh - m)
        denoms.append(jnp.sum(e, axis=-1, keepdims=True) + jnp.exp(sink - m))
        probs.append(e.astype(BF16))
    o = _dot(jnp.concatenate(probs, axis=0), vcat)
    out = []
    for j in range(ATT_GROUP):
        o_lo = o[j * ATT_BLOCK:(j + 1) * ATT_BLOCK] * (1.0 / denoms[j])
        o_hi = o[(j + ATT_GROUP) * ATT_BLOCK:(j + ATT_GROUP + 1) * ATT_BLOCK] * (1.0 / denoms[j + ATT_GROUP])
        out.append(jnp.where(lo, o_lo, o_hi))
    return out


def _gla_block(qg, kg, vg, la, s_t, tri, hmasks, causal):
    hi, mid, lo = _split3(la)
    b = _dot(tri, hi) + _dot(tri, mid) + _dot(tri, lo)
    b_last = b[GLA_BLOCK - 1:GLA_BLOCK]
    q_hat = qg * jnp.exp(b)
    k_til = (kg * jnp.exp(-b)).astype(BF16)
    k_dec = kg * jnp.exp(b_last - b)
    qstack = jnp.concatenate([jnp.where(hm, q_hat, 0.0) for hm in hmasks], axis=0).astype(BF16)
    rhs = jnp.concatenate([s_t.astype(BF16), k_til], axis=0)
    r = _dot_nt(qstack, rhs)
    o_inter = r[:, :GLA_DV]
    a = jnp.where(causal, r[:, GLA_DV:], 0.0).astype(BF16)
    outs = []
    upd = jnp.zeros_like(s_t)
    for h in range(GLA_HEADS):
        vh = vg[:, h * GLA_DV:(h + 1) * GLA_DV].astype(BF16)
        rows = slice(h * GLA_BLOCK, (h + 1) * GLA_BLOCK)
        outs.append(o_inter[rows] + _dot(a[rows], vh))
        upd = upd + _dot_tn(vh, jnp.where(hmasks[h], k_dec, 0.0).astype(BF16))
    return jnp.concatenate(outs, axis=1), s_t * jnp.exp(b_last) + upd


def _prompt_body(x_ref, mod_ref, cos_ref, sin_ref, ng_ref, wq_ref, wza_ref, wt_ref, qng_ref, kng_ref, sink_ref,
                 wgate_ref, bgate_ref, glang_ref, wba_ref, wbg_ref, wout_ref,
                 y_ref, wk_ref, wv_ref, st_ref,
                 kprev_ref, vprev_ref, st_scr):
    j = pl.program_id(1)
    last = pl.num_programs(1) - 1

    @pl.when(j == 0)
    def _():
        kprev_ref[...] = jnp.zeros_like(kprev_ref)
        vprev_ref[...] = jnp.zeros_like(vprev_ref)
        st_scr[...] = jnp.zeros_like(st_scr)

    x = x_ref[...]
    shift, scale, gate = mod_ref[0:1, :], mod_ref[1:2, :], mod_ref[2:3, :]
    ms = jnp.mean(x * x, axis=-1, keepdims=True)
    h = (x * lax.rsqrt(ms + EPS) * ng_ref[...] * (1.0 + scale) + shift).astype(BF16)
    p = _Proj(h, wq_ref, wza_ref, wt_ref)

    cos, sin = cos_ref[...], sin_ref[...]
    q = p.q_att()
    q_slabs = [_norm_rope_slab(q[:, i * LANES:(i + 1) * LANES], qng_ref[...], cos, sin)
               * (ATT_HEAD_DIM ** -0.5) for i in range(ATT_GROUP)]
    k = _norm_rope_slab(p.rows(R_K, R_V), kng_ref[...], cos, sin)
    v = p.rows(R_V, R_ZA)

    @pl.when(j == last)
    def _():
        wk_ref[...] = k[T_PROMPT - WINDOW:].T
        wv_ref[...] = v[T_PROMPT - WINDOW:].T

    k16, v16 = k.astype(BF16), v.astype(BF16)
    r = lax.broadcasted_iota(jnp.int32, (ATT_BLOCK, 2 * ATT_BLOCK), 0)
    c = lax.broadcasted_iota(jnp.int32, (ATT_BLOCK, 2 * ATT_BLOCK), 1)
    band = ((c < ATT_BLOCK) & (c >= r)) | ((c >= ATT_BLOCK) & (c - ATT_BLOCK <= r))
    first_mask = band & ((c >= ATT_BLOCK) | (j > 0))

    o_rows = []
    for i in range(T_PROMPT // ATT_BLOCK):
        rows = slice(i * ATT_BLOCK, (i + 1) * ATT_BLOCK)
        if i == 0:
            kp, vp, mask = kprev_ref[...], vprev_ref[...], first_mask
        else:
            prev = slice((i - 1) * ATT_BLOCK, i * ATT_BLOCK)
            kp, vp, mask = k16[prev], v16[prev], band
        kcat = jnp.concatenate([kp, k16[rows]], axis=0)
        vcat = jnp.concatenate([vp, v16[rows]], axis=0)
        o_rows.append(jnp.concatenate(_attend_block([s[rows] for s in q_slabs], kcat, vcat, mask, sink_ref),
                                      axis=1))
    o_a = jnp.concatenate(o_rows, axis=0)
    kprev_ref[...] = k16[T_PROMPT - ATT_BLOCK:]
    vprev_ref[...] = v16[T_PROMPT - ATT_BLOCK:]

    qg = p.rows(R_QG, R_KG) * (GLA_DK ** -0.5)
    kg = p.rows(R_KG, R_VG)
    vg = p.rows(R_VG, R_LR)
    la = _log_decay(p.rows(R_LR, R_LR + LANES), wgate_ref, bgate_ref)
    tr = lax.broadcasted_iota(jnp.int32, (GLA_BLOCK, GLA_BLOCK), 0)
    tc = lax.broadcasted_iota(jnp.int32, (GLA_BLOCK, GLA_BLOCK), 1)
    tri = (tr >= tc).astype(BF16)
    sr = lax.broadcasted_iota(jnp.int32, (GLA_HEADS * GLA_BLOCK, GLA_BLOCK), 0)
    sc = lax.broadcasted_iota(jnp.int32, (GLA_HEADS * GLA_BLOCK, GLA_BLOCK), 1)
    causal = (sr & (GLA_BLOCK - 1)) >= sc
    hmasks = _head_masks(GLA_BLOCK)
    s_t = st_scr[...]
    og_rows = []
    for i in range(T_PROMPT // GLA_BLOCK):
        rows = slice(i * GLA_BLOCK, (i + 1) * GLA_BLOCK)
        o_blk, s_t = _gla_block(qg[rows], kg[rows], vg[rows], la[rows], s_t, tri, hmasks, causal)
        og_rows.append(o_blk)
    st_scr[...] = s_t
    o_g = jnp.concatenate(og_rows, axis=0)

    @pl.when(j == last)
    def _():
        st_ref[...] = s_t.T

    y_ref[...] = _layer_tail(x, gate, o_a, o_g, p, glang_ref, wba_ref, wbg_ref, wout_ref)


def _const_spec(shape, nargs):
    zeros = (0,) * len(shape)
    if nargs == 1:
        return pl.BlockSpec(shape, lambda i: zeros)
    return pl.BlockSpec(shape, lambda b, j: zeros)


def _weight_specs(nargs):
    smem = pl.BlockSpec(memory_space=pltpu.SMEM)
    return [
        _const_spec((1, D_MODEL), nargs),
        _const_spec((ATT_WIDTH, D_MODEL), nargs),
        _const_spec((ATT_WIDTH, D_MODEL), nargs),
        _const_spec((R_END, D_MODEL), nargs),
        _const_spec((1, LANES), nargs),
        _const_spec((1, LANES), nargs),
        smem,
        _const_spec((LANES, GLA_KEY_WIDTH), nargs),
        _const_spec((1, GLA_KEY_WIDTH), nargs),
        _const_spec((1, GLA_DV), nargs),
        _const_spec((ATT_WIDTH, D_MODEL), nargs),
        _const_spec((GLA_WIDTH, D_MODEL), nargs),
        _const_spec((D_MODEL, D_MODEL), nargs),
    ]


def _prompt_layer(x, mod, cos, sin, weights):
    nblk = SEQ // T_PROMPT
    return pl.pallas_call(
        _prompt_body,
        grid=(BATCH, nblk),
        in_specs=[
            pl.BlockSpec((None, T_PROMPT, D_MODEL), lambda b, j: (b, j, 0)),
            pl.BlockSpec((None, 3, D_MODEL), lambda b, j: (b, 0, 0)),
            pl.BlockSpec((T_PROMPT, LANES), lambda b, j: (j, 0)),
            pl.BlockSpec((T_PROMPT, LANES), lambda b, j: (j, 0)),
        ] + _weight_specs(2),
        out_specs=[
            pl.BlockSpec((None, T_PROMPT, D_MODEL), lambda b, j: (b, j, 0)),
            pl.BlockSpec((None, KV_WIDTH, WINDOW), lambda b, j: (b, 0, 0)),
            pl.BlockSpec((None, KV_WIDTH, WINDOW), lambda b, j: (b, 0, 0)),
            pl.BlockSpec((None, GLA_KEY_WIDTH, GLA_DV), lambda b, j: (b, 0, 0)),
        ],
        out_shape=[
            jax.ShapeDtypeStruct((BATCH, SEQ, D_MODEL), F32),
            jax.ShapeDtypeStruct((BATCH, KV_WIDTH, WINDOW), F32),
            jax.ShapeDtypeStruct((BATCH, KV_WIDTH, WINDOW), F32),
            jax.ShapeDtypeStruct((BATCH, GLA_KEY_WIDTH, GLA_DV), F32),
        ],
        scratch_shapes=[
            pltpu.VMEM((ATT_BLOCK, KV_WIDTH), BF16),
            pltpu.VMEM((ATT_BLOCK, KV_WIDTH), BF16),
            pltpu.VMEM((GLA_DV, GLA_KEY_WIDTH), F32),
        ],
        compiler_params=pltpu.CompilerParams(dimension_semantics=("arbitrary", "arbitrary"),
                                             vmem_limit_bytes=VMEM_LIMIT),
        name="prompt_layer",
    )(x, mod, cos, sin, *weights)


def _sample_body(x_ref, mod_ref, cos_ref, sin_ref, ng_ref, wq_ref, wza_ref, wt_ref, qng_ref, kng_ref, sink_ref,
                 wgate_ref, bgate_ref, glang_ref, wba_ref, wbg_ref, wout_ref,
                 ck_ref, cv_ref, s0_ref,
                 y_ref, ok_ref, ov_ref, s1_ref,
                 qrows_scr, snew_scr, oc_scr, pnew_scr, rden_scr, qhat_scr, kdec_scr, vg_scr, la3_scr, oi_scr):
    G, L, T = SEQ_GROUP, DEC_SEQ, T_SAMPLE
    x = x_ref[...]

    def per_token(a, b):
        return jnp.concatenate([jnp.broadcast_to(mod_ref[n:n + 1, a:b], (L, b - a)) for n in range(G)], axis=0)

    shift = per_token(0, D_MODEL)
    scale = per_token(D_MODEL, 2 * D_MODEL)
    gate = per_token(2 * D_MODEL, 3 * D_MODEL)
    ms = jnp.mean(x * x, axis=-1, keepdims=True)
    h = (x * lax.rsqrt(ms + EPS) * ng_ref[...] * (1.0 + scale) + shift).astype(BF16)
    p = _Proj(h, wq_ref, wza_ref, wt_ref)

    cos, sin = cos_ref[...], sin_ref[...]
    lane = lax.broadcasted_iota(jnp.int32, (T, LANES), 1)
    lo = lane < HALF
    q = p.q_att()
    for i in range(ATT_GROUP):
        s = _norm_rope_slab(q[:, i * LANES:(i + 1) * LANES], qng_ref[...], cos, sin) * (ATT_HEAD_DIM ** -0.5)
        qrows_scr[:, i * L:(i + 1) * L, :] = jnp.where(lo, s, 0.0).reshape(G, L, LANES)
        qrows_scr[:, (i + ATT_GROUP) * L:(i + ATT_GROUP + 1) * L, :] = jnp.where(lo, 0.0, s).reshape(G, L, LANES)
    k_new = _norm_rope_slab(p.rows(R_K, R_V), kng_ref[...], cos, sin)
    v_new = p.rows(R_V, R_ZA)
    k_new_t, v_new_t = k_new.T, v_new.T
    nrow = ATT_HEADS * L
    snew_scr[...] = _dot(qrows_scr[...].reshape(G * nrow, LANES).astype(BF16),
                         k_new_t.astype(BF16)).reshape(G, nrow, T)

    qg = p.rows(R_QG, R_KG) * (GLA_DK ** -0.5)
    kg = p.rows(R_KG, R_VG)
    vg = p.rows(R_VG, R_LR)
    la = _log_decay(p.rows(R_LR, R_LR + LANES), wgate_ref, bgate_ref)
    tr = lax.broadcasted_iota(jnp.int32, (T, T), 0)
    tc = lax.broadcasted_iota(jnp.int32, (T, T), 1)
    same_seq = (tr // L) == (tc // L)
    causal = same_seq & (tc <= tr)
    hi, mid, lo3 = _split3(la)
    tri = causal.astype(BF16)
    blk = same_seq.astype(BF16)
    b = _dot(tri, hi) + _dot(tri, mid) + _dot(tri, lo3)
    b_last = _dot(blk, hi) + _dot(blk, mid) + _dot(blk, lo3)
    q_hat = qg * jnp.exp(b)
    k_til = (kg * jnp.exp(-b)).astype(BF16)
    k_dec = kg * jnp.exp(b_last - b)
    hmasks = _head_masks(T)
    zeros_l = jnp.zeros((G, L, GLA_KEY_WIDTH), F32)
    for hh in range(GLA_HEADS):
        qhat_scr[:, hh * L:(hh + 1) * L, :] = jnp.where(hmasks[hh], q_hat, 0.0).reshape(G, L, GLA_KEY_WIDTH)
    kdec_scr[:, 0:L, :] = k_dec.reshape(G, L, GLA_KEY_WIDTH)
    kdec_scr[:, L:2 * L, :] = zeros_l
    vg_scr[:, 0:L, :] = vg.reshape(G, L, GLA_WIDTH)
    vg_scr[:, L:2 * L, :] = jnp.zeros((G, L, GLA_WIDTH), F32)
    la3_scr[:, 0:L, :] = hi.astype(F32).reshape(G, L, GLA_KEY_WIDTH)
    la3_scr[:, L:2 * L, :] = mid.astype(F32).reshape(G, L, GLA_KEY_WIDTH)
    la3_scr[:, 2 * L:3 * L, :] = lo3.astype(F32).reshape(G, L, GLA_KEY_WIDTH)
    la3_scr[:, 3 * L:4 * L, :] = zeros_l

    rr = lax.broadcasted_iota(jnp.int32, (nrow, LANES), 0) & (L - 1)
    cc = lax.broadcasted_iota(jnp.int32, (nrow, LANES), 1)
    cache_mask = cc >= rr
    sink_col = jnp.concatenate([jnp.full((L, 1), sink_ref[hh], F32) for hh in range(ATT_HEADS)], axis=0)
    keep_old = lax.broadcasted_iota(jnp.int32, (KV_WIDTH, WINDOW), 1) < WINDOW - L
    ones16 = jnp.ones((4 * L, GLA_DV), BF16)

    def seq_body(n, carry):
        kc_t, vc_t = ck_ref[n], cv_ref[n]
        new_shift = WINDOW - L - n * L
        ok_ref[n] = jnp.where(keep_old, pltpu.roll(kc_t, WINDOW - L, 1), pltpu.roll(k_new_t, new_shift, 1))
        ov_ref[n] = jnp.where(keep_old, pltpu.roll(vc_t, WINDOW - L, 1), pltpu.roll(v_new_t, new_shift, 1))
        s_c = jnp.where(cache_mask, _dot(qrows_scr[n].astype(BF16), kc_t.astype(BF16)), NEG)
        s_n = jnp.where((cc >= n * L) & (cc <= n * L + rr), snew_scr[n], NEG)
        m = jnp.maximum(jnp.maximum(jnp.max(s_c, axis=-1, keepdims=True), jnp.max(s_n, axis=-1, keepdims=True)),
                        sink_col)
        e_c, e_n = jnp.exp(s_c - m), jnp.exp(s_n - m)
        denom = (jnp.sum(e_c, axis=-1, keepdims=True) + jnp.sum(e_n, axis=-1, keepdims=True)
                 + jnp.exp(sink_col - m))
        oc_scr[n] = _dot_nt(e_c.astype(BF16), vc_t.astype(BF16))
        pnew_scr[n] = e_n
        rden_scr[n] = jnp.broadcast_to(1.0 / denom, (nrow, LANES))

        s0 = s0_ref[n]
        oi_scr[n] = _dot(qhat_scr[n].astype(BF16), s0.astype(BF16))
        kv = _dot_tn(kdec_scr[n].astype(BF16), vg_scr[n].astype(BF16))
        upd = jnp.concatenate([kv[hh * GLA_DK:(hh + 1) * GLA_DK, hh * GLA_DV:(hh + 1) * GLA_DV]
                               for hh in range(GLA_HEADS)], axis=0)
        b_col = _dot_tn(la3_scr[n].astype(BF16), ones16)
        s1_ref[n] = jnp.exp(b_col) * s0 + upd
        return carry

    lax.fori_loop(0, G, seq_body, 0)

    o_new = _dot(pnew_scr[...].reshape(G * nrow, T).astype(BF16), v_new.astype(BF16))
    o_att = ((oc_scr[...].reshape(G * nrow, KV_WIDTH) + o_new)
             * rden_scr[...].reshape(G * nrow, LANES)).reshape(G, nrow, KV_WIDTH)
    o_a = jnp.concatenate(
        [jnp.where(lo, o_att[:, i * L:(i + 1) * L, :].reshape(T, LANES),
                   o_att[:, (i + ATT_GROUP) * L:(i + ATT_GROUP + 1) * L, :].reshape(T, LANES))
         for i in range(ATT_GROUP)], axis=1)
    oi = oi_scr[...]
    o_inter = jnp.concatenate([oi[:, hh * L:(hh + 1) * L, :].reshape(T, GLA_DV) for hh in range(GLA_HEADS)],
                              axis=1)
    intra = []
    for hh in range(GLA_HEADS):
        a = _dot_nt(jnp.where(hmasks[hh], q_hat, 0.0).astype(BF16), k_til)
        a = jnp.where(causal, a, 0.0).astype(BF16)
        intra.append(_dot(a, vg[:, hh * GLA_DV:(hh + 1) * GLA_DV].astype(BF16)))
    o_g = o_inter + jnp.concatenate(intra, axis=1)

    y_ref[...] = _layer_tail(x, gate, o_a, o_g, p, glang_ref, wba_ref, wbg_ref, wout_ref)


def _sample_layer(x, mod, cos, sin, weights, ck, cv, s0):
    G, L, T = SEQ_GROUP, DEC_SEQ, T_SAMPLE
    nrow = ATT_HEADS * L
    row_spec = lambda w: pl.BlockSpec((T, w), lambda i: (i, 0))
    seq_spec = lambda a, b: pl.BlockSpec((G, a, b), lambda i: (i, 0, 0))
    return pl.pallas_call(
        _sample_body,
        grid=(DEC_BATCH // G,),
        in_specs=[row_spec(D_MODEL), pl.BlockSpec((G, 3 * D_MODEL), lambda i: (i, 0)),
                  _const_spec((T, LANES), 1), _const_spec((T, LANES), 1)]
                 + _weight_specs(1)
                 + [seq_spec(KV_WIDTH, WINDOW), seq_spec(KV_WIDTH, WINDOW), seq_spec(GLA_KEY_WIDTH, GLA_DV)],
        out_specs=[row_spec(D_MODEL), seq_spec(KV_WIDTH, WINDOW), seq_spec(KV_WIDTH, WINDOW),
                   seq_spec(GLA_KEY_WIDTH, GLA_DV)],
        out_shape=[
            jax.ShapeDtypeStruct((DEC_BATCH * L, D_MODEL), F32),
            jax.ShapeDtypeStruct((DEC_BATCH, KV_WIDTH, WINDOW), F32),
            jax.ShapeDtypeStruct((DEC_BATCH, KV_WIDTH, WINDOW), F32),
            jax.ShapeDtypeStruct((DEC_BATCH, GLA_KEY_WIDTH, GLA_DV), F32),
        ],
        scratch_shapes=[
            pltpu.VMEM((G, nrow, LANES), F32),
            pltpu.VMEM((G, nrow, T), F32),
            pltpu.VMEM((G, nrow, KV_WIDTH), F32),
            pltpu.VMEM((G, nrow, T), F32),
            pltpu.VMEM((G, nrow, LANES), F32),
            pltpu.VMEM((G, GLA_HEADS * L, GLA_KEY_WIDTH), F32),
            pltpu.VMEM((G, 2 * L, GLA_KEY_WIDTH), F32),
            pltpu.VMEM((G, 2 * L, GLA_WIDTH), F32),
            pltpu.VMEM((G, 4 * L, GLA_KEY_WIDTH), F32),
            pltpu.VMEM((G, GLA_HEADS * L, GLA_DV), F32),
        ],
        compiler_params=pltpu.CompilerParams(dimension_semantics=("arbitrary",),
                                             vmem_limit_bytes=VMEM_LIMIT),
        name="sample_layer",
    )(x, mod, cos, sin, *weights, ck, cv, s0)


def _rope_tables(pos):
    half = ATT_HEAD_DIM // 2
    inv = 1.0 / (ROPE_THETA ** (np.arange(half, dtype=np.float64) / half))
    ang = pos.astype(np.float64)[:, None] * inv[None, :]
    c, s = np.cos(ang), np.sin(ang)
    return (np.tile(c, (1, 4)).astype(np.float32),
            np.concatenate([-s, s, -s, s], axis=1).astype(np.float32))


def _pair_heads(w):
    shape = w.shape
    w = w.reshape((ATT_KV_HEADS, ATT_GROUP, ATT_HEAD_DIM) + shape[1:])
    return jnp.swapaxes(w, 0, 1).reshape(shape)


def kernel(x_prompt, x_sample, cache_win_k, cache_win_v, state_gla, c_prompt, c_sample, norm_g, w_ada, b_ada, w_in, q_norm_g, k_norm_g, attn_sinks, w_gla_gate, b_gla_gate, gla_norm_g, w_branch_att, w_branch_gla, w_out):
    assert w_in.shape == (1, D_MODEL, R_END), "single-layer trunk"
    wt = jnp.swapaxes(w_in[0], 0, 1)
    wgate = jnp.concatenate([w_gla_gate[0], jnp.zeros((LANES - GLA_GATE_RANK, GLA_KEY_WIDTH), F32)],
                            axis=0).astype(BF16)
    weights = (
        norm_g[0][None, :],
        _pair_heads(wt[0:R_K]).astype(BF16), _pair_heads(wt[R_ZA:R_QG]).astype(BF16), wt.astype(BF16),
        jnp.tile(q_norm_g[0], 2)[None, :], jnp.tile(k_norm_g[0], 2)[None, :],
        attn_sinks[0], wgate, b_gla_gate[0][None, :], gla_norm_g[0][None, :],
        _pair_heads(w_branch_att[0]).astype(BF16), w_branch_gla[0].astype(BF16), w_out[0].astype(BF16),
    )

    c_all = jnp.concatenate([c_sample, c_prompt, jnp.zeros((MOD_ROWS - DEC_BATCH - BATCH, D_MODEL), F32)], axis=0)
    mod = _modulation(c_all, w_ada[0], b_ada[0][None, :])
    mod_prompt = mod[DEC_BATCH:DEC_BATCH + BATCH].reshape(BATCH, 3, D_MODEL)

    cos_p, sin_p = _rope_tables(np.arange(SEQ))
    cos_s, sin_s = _rope_tables(PAST_LEN + np.arange(DEC_SEQ))
    cos_s, sin_s = np.tile(cos_s, (SEQ_GROUP, 1)), np.tile(sin_s, (SEQ_GROUP, 1))

    def kv_rows(c):
        return jnp.transpose(c, (0, 2, 3, 1)).reshape(c.shape[0], KV_WIDTH, WINDOW)

    def kv_out(c):
        return jnp.transpose(c.reshape(c.shape[0], ATT_KV_HEADS, ATT_HEAD_DIM, WINDOW), (0, 3, 1, 2))[None]

    y_p, wk_p, wv_p, st_p = _prompt_layer(x_prompt, mod_prompt, cos_p, sin_p, weights)
    y_s, wk_s, wv_s, st_s = _sample_layer(
        x_sample.reshape(DEC_BATCH * DEC_SEQ, D_MODEL), mod, cos_s, sin_s, weights,
        kv_rows(cache_win_k[0]), kv_rows(cache_win_v[0]),
        state_gla[0].reshape(DEC_BATCH, GLA_KEY_WIDTH, GLA_DV))

    st_shape = (1, -1, GLA_HEADS, GLA_DK, GLA_DV)
    return (y_p, y_s.reshape(DEC_BATCH, DEC_SEQ, D_MODEL),
            kv_out(wk_p), kv_out(wv_p), st_p.reshape(st_shape),
            kv_out(wk_s), kv_out(wv_s), st_s.reshape(st_shape))
```

```python
import numpy as np

import jax
import jax.numpy as jnp
from jax import lax
from jax.experimental import pallas as pl
from jax.experimental.pallas import tpu as pltpu

F32 = jnp.float32
BF16 = jnp.bfloat16

D_MODEL = 1024
BATCH = 2
SEQ = 8192
DEC_BATCH = 128
DEC_SEQ = 8
PAST_LEN = 16384
ATT_HEADS = 8
ATT_KV_HEADS = 2
ATT_GROUP = ATT_HEADS // ATT_KV_HEADS
ATT_HEAD_DIM = 64
ATT_WIDTH = ATT_HEADS * ATT_HEAD_DIM
KV_WIDTH = ATT_KV_HEADS * ATT_HEAD_DIM
WINDOW = 128
ATT_BLOCK = 128
ROPE_THETA = 10000.0
GLA_HEADS = 4
GLA_WIDTH = D_MODEL // 2
GLA_DV = GLA_WIDTH // GLA_HEADS
GLA_KEY_WIDTH = GLA_WIDTH // 2
GLA_DK = GLA_KEY_WIDTH // GLA_HEADS
GLA_GATE_RANK = 16
GLA_GATE_TAU = 16.0
EPS = 1e-6
NEG = -1e30

LANES = 128
HALF = LANES // 2
R_K, R_V, R_ZA = ATT_WIDTH, ATT_WIDTH + KV_WIDTH, ATT_WIDTH + 2 * KV_WIDTH
R_QG = R_ZA + ATT_WIDTH
R_KG = R_QG + GLA_KEY_WIDTH
R_VG = R_KG + GLA_KEY_WIDTH
R_LR = R_VG + GLA_WIDTH
R_ZG = R_LR + GLA_GATE_RANK
R_MA = R_ZG + GLA_WIDTH
R_MG = R_MA + D_MODEL
R_END = R_MG + D_MODEL

T_PROMPT = 512
GLA_BLOCK = 64
SEQ_GROUP = 16
T_SAMPLE = SEQ_GROUP * DEC_SEQ
MOD_ROWS = DEC_BATCH + 16
MOD_TILE = 512
VMEM_LIMIT = 56 * 1024 * 1024


def _sigmoid(x):
    return 1.0 / (1.0 + jnp.exp(-x))


def _dot(a, b):
    return jnp.dot(a, b, preferred_element_type=F32)


def _dot_nt(a, b):
    return lax.dot_general(a, b, (((1,), (1,)), ((), ())), preferred_element_type=F32)


def _dot_tn(a, b):
    return lax.dot_general(a, b, (((0,), (0,)), ((), ())), preferred_element_type=F32)


def _split3(x):
    hi = x.astype(BF16)
    r1 = x - hi.astype(F32)
    mid = r1.astype(BF16)
    lo = (r1 - mid.astype(F32)).astype(BF16)
    return hi, mid, lo


def _norm_rope_slab(xs, g, cos, sin):
    lane = lax.broadcasted_iota(jnp.int32, xs.shape, 1)
    lo = lane < HALF
    first = (lane & (HALF // 2)) == 0
    sq = xs * xs
    s_lo = jnp.sum(jnp.where(lo, sq, 0.0), axis=-1, keepdims=True)
    s_hi = jnp.sum(jnp.where(lo, 0.0, sq), axis=-1, keepdims=True)
    inv = jnp.where(lo, lax.rsqrt(s_lo * (1.0 / ATT_HEAD_DIM) + EPS),
                    lax.rsqrt(s_hi * (1.0 / ATT_HEAD_DIM) + EPS))
    xn = xs * inv * g
    swapped = jnp.where(first, pltpu.roll(xn, LANES - HALF // 2, 1), pltpu.roll(xn, HALF // 2, 1))
    return xn * cos + swapped * sin


def _log_decay(lrz, wgate_ref, bgate_ref):
    pre = _dot(lrz.astype(BF16), wgate_ref[...]) + bgate_ref[...]
    log_sig = jnp.minimum(pre, 0.0) - jnp.log1p(jnp.exp(-jnp.abs(pre)))
    return log_sig * (1.0 / GLA_GATE_TAU)


def _head_masks(rows):
    lane = lax.broadcasted_iota(jnp.int32, (rows, GLA_KEY_WIDTH), 1)
    return [(lane >= h * GLA_DK) & (lane < (h + 1) * GLA_DK) for h in range(GLA_HEADS)]


class _Proj:
    def __init__(self, h, wq_ref, wza_ref, wt_ref):
        self.h, self.wq_ref, self.wza_ref, self.wt_ref = h, wq_ref, wza_ref, wt_ref

    def rows(self, a, b):
        return _dot_nt(self.h, self.wt_ref[a:b, :])

    def q_att(self):
        return _dot_nt(self.h, self.wq_ref[...])

    def z_att(self):
        return _dot_nt(self.h, self.wza_ref[...])


def _gates(p):
    z_a = p.z_att()
    z_g = p.rows(R_ZG, R_MA)
    return (z_a * _sigmoid(z_a), z_g * _sigmoid(z_g),
            _sigmoid(p.rows(R_MA, R_MG)), _sigmoid(p.rows(R_MG, R_END)))


def _layer_tail(x, gate, o_a, o_g, gates, glang_ref, wba_ref, wbg_ref, wout_ref):
    silu_a, silu_g, merge_a, merge_g = gates
    o_a = o_a * silu_a
    slabs = []
    for h in range(GLA_HEADS):
        oh = o_g[:, h * GLA_DV:(h + 1) * GLA_DV]
        ms = jnp.mean(oh * oh, axis=-1, keepdims=True)
        slabs.append(oh * lax.rsqrt(ms + EPS) * glang_ref[...])
    o_g = jnp.concatenate(slabs, axis=1) * silu_g
    merged = (merge_a * _dot(o_a.astype(BF16), wba_ref[...])
              + merge_g * _dot(o_g.astype(BF16), wbg_ref[...]))
    return x + gate * _dot(merged.astype(BF16), wout_ref[...])


def _mod_body(c_ref, w_ref, b_ref, o_ref, act_ref):
    @pl.when(pl.program_id(0) == 0)
    def _():
        c = c_ref[...]
        act_ref[...] = (c * _sigmoid(c)).astype(BF16)

    o_ref[...] = _dot(act_ref[...], w_ref[...].astype(BF16)) + b_ref[...]


def _modulation(c_all, w_ada, b_ada):
    rows = c_all.shape[0]
    return pl.pallas_call(
        _mod_body,
        grid=(3 * D_MODEL // MOD_TILE,),
        in_specs=[
            pl.BlockSpec((rows, D_MODEL), lambda i: (0, 0)),
            pl.BlockSpec((D_MODEL, MOD_TILE), lambda i: (0, i)),
            pl.BlockSpec((1, MOD_TILE), lambda i: (0, i)),
        ],
        out_specs=pl.BlockSpec((rows, MOD_TILE), lambda i: (0, i)),
        out_shape=jax.ShapeDtypeStruct((rows, 3 * D_MODEL), F32),
        scratch_shapes=[pltpu.VMEM((rows, D_MODEL), BF16)],
        compiler_params=pltpu.CompilerParams(dimension_semantics=("arbitrary",),
                                             vmem_limit_bytes=VMEM_LIMIT),
        name="adaln_mod",
    )(c_all, w_ada, b_ada)


def _softmax_block(s, mask, sink_ref):
    probs, rden = [], []
    for h in range(ATT_HEADS):
        sh = jnp.where(mask, s[h * ATT_BLOCK:(h + 1) * ATT_BLOCK], NEG)
        sink = sink_ref[h]
        m = jnp.maximum(jnp.max(sh, axis=-1, keepdims=True), sink)
        e = jnp.exp(sh - m)
        rden.append(1.0 / (jnp.sum(e, axis=-1, keepdims=True) + jnp.exp(sink - m)))
        probs.append(e.astype(BF16))
    return jnp.concatenate(probs, axis=0), rden


def _prompt_body(x_ref, mod_ref, cos_ref, sin_ref, ng_ref, wq_ref, wza_ref, wt_ref, qng_ref, kng_ref, sink_ref,
                 wgate_ref, bgate_ref, glang_ref, wba_ref, wbg_ref, wout_ref,
                 y_ref, wk_ref, wv_ref, st_ref,
                 kprev_ref, vprev_ref, st_scr):
    T, C = T_PROMPT, GLA_BLOCK
    n_att, n_gla = T // ATT_BLOCK, T // C
    j = pl.program_id(1)
    last = pl.num_programs(1) - 1

    @pl.when(j == 0)
    def _():
        kprev_ref[...] = jnp.zeros_like(kprev_ref)
        vprev_ref[...] = jnp.zeros_like(vprev_ref)
        st_scr[...] = jnp.zeros_like(st_scr)

    x = x_ref[...]
    shift, scale, gate = mod_ref[0:1, :], mod_ref[1:2, :], mod_ref[2:3, :]
    ms = jnp.mean(x * x, axis=-1, keepdims=True)
    h = (x * lax.rsqrt(ms + EPS) * ng_ref[...] * (1.0 + scale) + shift).astype(BF16)
    p = _Proj(h, wq_ref, wza_ref, wt_ref)
    cos, sin = cos_ref[...], sin_ref[...]
    lane = lax.broadcasted_iota(jnp.int32, (ATT_BLOCK, LANES), 1)
    lo = lane < HALF
    v_ = {}

    def a1_qkv():
        v_["q"], v_["k_raw"], v_["v"] = p.q_att(), p.rows(R_K, R_V), p.rows(R_V, R_ZA)

    def a2_norm_rope():
        q = v_["q"]
        v_["q_slabs"] = [_norm_rope_slab(q[:, i * LANES:(i + 1) * LANES], qng_ref[...], cos, sin)
                         * (ATT_HEAD_DIM ** -0.5) for i in range(ATT_GROUP)]
        k = _norm_rope_slab(v_["k_raw"], kng_ref[...], cos, sin)
        v = v_["v"]

        @pl.when(j == last)
        def _():
            wk_ref[...] = k[T - WINDOW:].T
            wv_ref[...] = v[T - WINDOW:].T

        v_["k16"], v_["v16"] = k.astype(BF16), v.astype(BF16)

    def a3_scores():
        k16, v16, q_slabs = v_["k16"], v_["v16"], v_["q_slabs"]
        r = lax.broadcasted_iota(jnp.int32, (ATT_BLOCK, 2 * ATT_BLOCK), 0)
        c = lax.broadcasted_iota(jnp.int32, (ATT_BLOCK, 2 * ATT_BLOCK), 1)
        band = ((c < ATT_BLOCK) & (c >= r)) | ((c >= ATT_BLOCK) & (c - ATT_BLOCK <= r))
        v_["masks"] = [band & ((c >= ATT_BLOCK) | (j > 0))] + [band] * (n_att - 1)
        scores, vcats = [], []
        for i in range(n_att):
            rows = slice(i * ATT_BLOCK, (i + 1) * ATT_BLOCK)
            if i == 0:
                kp, vp = kprev_ref[...], vprev_ref[...]
            else:
                prev = slice((i - 1) * ATT_BLOCK, i * ATT_BLOCK)
                kp, vp = k16[prev], v16[prev]
            pieces = ([jnp.where(lo, s[rows], 0.0).astype(BF16) for s in q_slabs]
                      + [jnp.where(lo, 0.0, s[rows]).astype(BF16) for s in q_slabs])
            scores.append(_dot_nt(jnp.concatenate(pieces, axis=0), jnp.concatenate([kp, k16[rows]], axis=0)))
            vcats.append(jnp.concatenate([vp, v16[rows]], axis=0))
        kprev_ref[...] = k16[T - ATT_BLOCK:]
        vprev_ref[...] = v16[T - ATT_BLOCK:]
        v_["scores"], v_["vcats"] = scores, vcats

    def a4_softmax():
        v_["soft"] = [_softmax_block(v_["scores"][i], v_["masks"][i], sink_ref) for i in range(n_att)]

    def a5_values():
        o_rows = []
        for i in range(n_att):
            probs, rden = v_["soft"][i]
            o = _dot(probs, v_["vcats"][i])
            o_rows.append(jnp.concatenate(
                [jnp.where(lo, o[g * ATT_BLOCK:(g + 1) * ATT_BLOCK] * rden[g],
                           o[(g + ATT_GROUP) * ATT_BLOCK:(g + ATT_GROUP + 1) * ATT_BLOCK] * rden[g + ATT_GROUP])
                 for g in range(ATT_GROUP)], axis=1))
        v_["o_a"] = jnp.concatenate(o_rows, axis=0)

    def g1_decay():
        v_["la"] = _log_decay(p.rows(R_LR, R_LR + LANES), wgate_ref, bgate_ref)

    def g2_qkv():
        v_["qg"] = p.rows(R_QG, R_KG) * (GLA_DK ** -0.5)
        v_["kg"] = p.rows(R_KG, R_VG)
        v_["vg16"] = p.rows(R_VG, R_LR).astype(BF16)

    def g3_cumsum():
        tr = lax.broadcasted_iota(jnp.int32, (T, T), 0)
        tc = lax.broadcasted_iota(jnp.int32, (T, T), 1)
        tri = (((tr // C) == (tc // C)) & (tc <= tr)).astype(BF16)
        hi, mid, lo3 = _split3(v_["la"])
        b = _dot(tri, hi) + _dot(tri, mid) + _dot(tri, lo3)
        v_["b"] = b
        v_["b_last"] = [b[(i + 1) * C - 1:(i + 1) * C] for i in range(n_gla)]

    def g4_decayed():
        b, qg, kg = v_["b"], v_["qg"], v_["kg"]
        b_end = jnp.concatenate([jnp.broadcast_to(bl, (C, GLA_KEY_WIDTH)) for bl in v_["b_last"]], axis=0)
        q_hat = qg * jnp.exp(b)
        v_["k_til"] = (kg * jnp.exp(-b)).astype(BF16)
        k_dec = kg * jnp.exp(b_end - b)
        hmasks = _head_masks(T)
        v_["q_m"] = [jnp.where(hm, q_hat, 0.0).astype(BF16) for hm in hmasks]
        v_["k_m"] = [jnp.where(hm, k_dec, 0.0).astype(BF16) for hm in hmasks]

    def g5_states():
        vg16 = v_["vg16"]
        states = [st_scr[...]]
        for i in range(n_gla):
            rows = slice(i * C, (i + 1) * C)
            v_stack = jnp.concatenate([vg16[rows, hh * GLA_DV:(hh + 1) * GLA_DV] for hh in range(GLA_HEADS)],
                                      axis=0)
            k_stack = jnp.concatenate([km[rows] for km in v_["k_m"]], axis=0)
            states.append(states[-1] * jnp.exp(v_["b_last"][i]) + _dot_tn(v_stack, k_stack))
        st_scr[...] = states[-1]

        @pl.when(j == last)
        def _():
            st_ref[...] = states[-1].T

        v_["states"] = states

    def g6_mixed():
        mixed = []
        for i in range(n_gla):
            rows = slice(i * C, (i + 1) * C)
            q_stack = jnp.concatenate([qm[rows] for qm in v_["q_m"]], axis=0)
            rhs = jnp.concatenate([v_["states"][i].astype(BF16), v_["k_til"][rows]], axis=0)
            mixed.append(_dot_nt(q_stack, rhs))
        v_["mixed"] = mixed

    def g7_outputs():
        sr = lax.broadcasted_iota(jnp.int32, (GLA_HEADS * C, C), 0)
        sc = lax.broadcasted_iota(jnp.int32, (GLA_HEADS * C, C), 1)
        causal = (sr & (C - 1)) >= sc
        vg16 = v_["vg16"]
        og_rows = []
        for i in range(n_gla):
            rows = slice(i * C, (i + 1) * C)
            m_i = v_["mixed"][i]
            a = jnp.where(causal, m_i[:, GLA_DV:], 0.0).astype(BF16)
            og_rows.append(jnp.concatenate(
                [m_i[hh * C:(hh + 1) * C, :GLA_DV]
                 + _dot(a[hh * C:(hh + 1) * C], vg16[rows, hh * GLA_DV:(hh + 1) * GLA_DV])
                 for hh in range(GLA_HEADS)], axis=1))
        v_["o_g"] = jnp.concatenate(og_rows, axis=0)

    def t1_silu_att():
        z = p.z_att()
        v_["silu_a"] = z * _sigmoid(z)

    def t2_silu_gla():
        z = p.rows(R_ZG, R_MA)
        v_["silu_g"] = z * _sigmoid(z)

    def t3_merge_att():
        v_["merge_a"] = _sigmoid(p.rows(R_MA, R_MG))

    def t4_merge_gla():
        v_["merge_g"] = _sigmoid(p.rows(R_MG, R_END))

    def t5_tail():
        gates = (v_["silu_a"], v_["silu_g"], v_["merge_a"], v_["merge_g"])
        y_ref[...] = _layer_tail(x, gate, v_["o_a"], v_["o_g"], gates, glang_ref, wba_ref, wbg_ref, wout_ref)

    for stage in (a1_qkv, g1_decay, g2_qkv, a2_norm_rope, g3_cumsum, a3_scores, t1_silu_att, t2_silu_gla,
                  g4_decayed, a4_softmax, t3_merge_att, g5_states, a5_values, t4_merge_gla, g6_mixed,
                  g7_outputs, t5_tail):
        stage()


def _const_spec(shape, nargs):
    zeros = (0,) * len(shape)
    if nargs == 1:
        return pl.BlockSpec(shape, lambda i: zeros)
    return pl.BlockSpec(shape, lambda b, j: zeros)


def _weight_specs(nargs):
    smem = pl.BlockSpec(memory_space=pltpu.SMEM)
    return [
        _const_spec((1, D_MODEL), nargs),
        _const_spec((ATT_WIDTH, D_MODEL), nargs),
        _const_spec((ATT_WIDTH, D_MODEL), nargs),
        _const_spec((R_END, D_MODEL), nargs),
        _const_spec((1, LANES), nargs),
        _const_spec((1, LANES), nargs),
        smem,
        _const_spec((LANES, GLA_KEY_WIDTH), nargs),
        _const_spec((1, GLA_KEY_WIDTH), nargs),
        _const_spec((1, GLA_DV), nargs),
        _const_spec((ATT_WIDTH, D_MODEL), nargs),
        _const_spec((GLA_WIDTH, D_MODEL), nargs),
        _const_spec((D_MODEL, D_MODEL), nargs),
    ]


def _prompt_layer(x, mod, cos, sin, weights):
    nblk = SEQ // T_PROMPT
    return pl.pallas_call(
        _prompt_body,
        grid=(BATCH, nblk),
        in_specs=[
            pl.BlockSpec((None, T_PROMPT, D_MODEL), lambda b, j: (b, j, 0)),
            pl.BlockSpec((None, 3, D_MODEL), lambda b, j: (b, 0, 0)),
            pl.BlockSpec((T_PROMPT, LANES), lambda b, j: (j, 0)),
            pl.BlockSpec((T_PROMPT, LANES), lambda b, j: (j, 0)),
        ] + _weight_specs(2),
        out_specs=[
            pl.BlockSpec((None, T_PROMPT, D_MODEL), lambda b, j: (b, j, 0)),
            pl.BlockSpec((None, KV_WIDTH, WINDOW), lambda b, j: (b, 0, 0)),
            pl.BlockSpec((None, KV_WIDTH, WINDOW), lambda b, j: (b, 0, 0)),
            pl.BlockSpec((None, GLA_KEY_WIDTH, GLA_DV), lambda b, j: (b, 0, 0)),
        ],
        out_shape=[
            jax.ShapeDtypeStruct((BATCH, SEQ, D_MODEL), F32),
            jax.ShapeDtypeStruct((BATCH, KV_WIDTH, WINDOW), F32),
            jax.ShapeDtypeStruct((BATCH, KV_WIDTH, WINDOW), F32),
            jax.ShapeDtypeStruct((BATCH, GLA_KEY_WIDTH, GLA_DV), F32),
        ],
        scratch_shapes=[
            pltpu.VMEM((ATT_BLOCK, KV_WIDTH), BF16),
            pltpu.VMEM((ATT_BLOCK, KV_WIDTH), BF16),
            pltpu.VMEM((GLA_DV, GLA_KEY_WIDTH), F32),
        ],
        compiler_params=pltpu.CompilerParams(dimension_semantics=("arbitrary", "arbitrary"),
                                             vmem_limit_bytes=VMEM_LIMIT),
        name="prompt_layer",
    )(x, mod, cos, sin, *weights)


def _sample_body(x_ref, mod_ref, cos_ref, sin_ref, ng_ref, wq_ref, wza_ref, wt_ref, qng_ref, kng_ref, sink_ref,
                 wgate_ref, bgate_ref, glang_ref, wba_ref, wbg_ref, wout_ref,
                 ck_ref, cv_ref, s0_ref,
                 y_ref, ok_ref, ov_ref, s1_ref,
                 qrows_scr, snew_scr, oc_scr, pnew_scr, rden_scr, qhat_scr, kdec_scr, vg_scr, la3_scr, oi_scr):
    G, L, T = SEQ_GROUP, DEC_SEQ, T_SAMPLE
    x = x_ref[...]

    def per_token(a, b):
        return jnp.concatenate([jnp.broadcast_to(mod_ref[n:n + 1, a:b], (L, b - a)) for n in range(G)], axis=0)

    shift = per_token(0, D_MODEL)
    scale = per_token(D_MODEL, 2 * D_MODEL)
    gate = per_token(2 * D_MODEL, 3 * D_MODEL)
    ms = jnp.mean(x * x, axis=-1, keepdims=True)
    h = (x * lax.rsqrt(ms + EPS) * ng_ref[...] * (1.0 + scale) + shift).astype(BF16)
    p = _Proj(h, wq_ref, wza_ref, wt_ref)

    cos, sin = cos_ref[...], sin_ref[...]
    lane = lax.broadcasted_iota(jnp.int32, (T, LANES), 1)
    lo = lane < HALF
    q = p.q_att()
    for i in range(ATT_GROUP):
        s = _norm_rope_slab(q[:, i * LANES:(i + 1) * LANES], qng_ref[...], cos, sin) * (ATT_HEAD_DIM ** -0.5)
        qrows_scr[:, i * L:(i + 1) * L, :] = jnp.where(lo, s, 0.0).reshape(G, L, LANES)
        qrows_scr[:, (i + ATT_GROUP) * L:(i + ATT_GROUP + 1) * L, :] = jnp.where(lo, 0.0, s).reshape(G, L, LANES)
    k_new = _norm_rope_slab(p.rows(R_K, R_V), kng_ref[...], cos, sin)
    v_new = p.rows(R_V, R_ZA)
    k_new_t, v_new_t = k_new.T, v_new.T
    nrow = ATT_HEADS * L
    snew_scr[...] = _dot(qrows_scr[...].reshape(G * nrow, LANES).astype(BF16),
                         k_new_t.astype(BF16)).reshape(G, nrow, T)

    qg = p.rows(R_QG, R_KG) * (GLA_DK ** -0.5)
    kg = p.rows(R_KG, R_VG)
    vg = p.rows(R_VG, R_LR)
    la = _log_decay(p.rows(R_LR, R_LR + LANES), wgate_ref, bgate_ref)
    tr = lax.broadcasted_iota(jnp.int32, (T, T), 0)
    tc = lax.broadcasted_iota(jnp.int32, (T, T), 1)
    same_seq = (tr // L) == (tc // L)
    causal = same_seq & (tc <= tr)
    hi, mid, lo3 = _split3(la)
    tri = causal.astype(BF16)
    blk = same_seq.astype(BF16)
    b = _dot(tri, hi) + _dot(tri, mid) + _dot(tri, lo3)
    b_last = _dot(blk, hi) + _dot(blk, mid) + _dot(blk, lo3)
    q_hat = qg * jnp.exp(b)
    k_til = (kg * jnp.exp(-b)).astype(BF16)
    k_dec = kg * jnp.exp(b_last - b)
    hmasks = _head_masks(T)
    zeros_l = jnp.zeros((G, L, GLA_KEY_WIDTH), F32)
    for hh in range(GLA_HEADS):
        qhat_scr[:, hh * L:(hh + 1) * L, :] = jnp.where(hmasks[hh], q_hat, 0.0).reshape(G, L, GLA_KEY_WIDTH)
    kdec_scr[:, 0:L, :] = k_dec.reshape(G, L, GLA_KEY_WIDTH)
    kdec_scr[:, L:2 * L, :] = zeros_l
    vg_scr[:, 0:L, :] = vg.reshape(G, L, GLA_WIDTH)
    vg_scr[:, L:2 * L, :] = jnp.zeros((G, L, GLA_WIDTH), F32)
    la3_scr[:, 0:L, :] = hi.astype(F32).reshape(G, L, GLA_KEY_WIDTH)
    la3_scr[:, L:2 * L, :] = mid.astype(F32).reshape(G, L, GLA_KEY_WIDTH)
    la3_scr[:, 2 * L:3 * L, :] = lo3.astype(F32).reshape(G, L, GLA_KEY_WIDTH)
    la3_scr[:, 3 * L:4 * L, :] = zeros_l

    rr = lax.broadcasted_iota(jnp.int32, (nrow, LANES), 0) & (L - 1)
    cc = lax.broadcasted_iota(jnp.int32, (nrow, LANES), 1)
    cache_mask = cc >= rr
    sink_col = jnp.concatenate([jnp.full((L, 1), sink_ref[hh], F32) for hh in range(ATT_HEADS)], axis=0)
    keep_old = lax.broadcasted_iota(jnp.int32, (KV_WIDTH, WINDOW), 1) < WINDOW - L
    ones16 = jnp.ones((4 * L, GLA_DV), BF16)

    def seq_body(n, carry):
        kc_t, vc_t = ck_ref[n], cv_ref[n]
        new_shift = WINDOW - L - n * L
        ok_ref[n] = jnp.where(keep_old, pltpu.roll(kc_t, WINDOW - L, 1), pltpu.roll(k_new_t, new_shift, 1))
        ov_ref[n] = jnp.where(keep_old, pltpu.roll(vc_t, WINDOW - L, 1), pltpu.roll(v_new_t, new_shift, 1))
        s_c = jnp.where(cache_mask, _dot(qrows_scr[n].astype(BF16), kc_t.astype(BF16)), NEG)
        s_n = jnp.where((cc >= n * L) & (cc <= n * L + rr), snew_scr[n], NEG)
        m = jnp.maximum(jnp.maximum(jnp.max(s_c, axis=-1, keepdims=True), jnp.max(s_n, axis=-1, keepdims=True)),
                        sink_col)
        e_c, e_n = jnp.exp(s_c - m), jnp.exp(s_n - m)
        denom = (jnp.sum(e_c, axis=-1, keepdims=True) + jnp.sum(e_n, axis=-1, keepdims=True)
                 + jnp.exp(sink_col - m))
        oc_scr[n] = _dot_nt(e_c.astype(BF16), vc_t.astype(BF16))
        pnew_scr[n] = e_n
        rden_scr[n] = jnp.broadcast_to(1.0 / denom, (nrow, LANES))

        s0 = s0_ref[n]
        oi_scr[n] = _dot(qhat_scr[n].astype(BF16), s0.astype(BF16))
        kv = _dot_tn(kdec_scr[n].astype(BF16), vg_scr[n].astype(BF16))
        upd = jnp.concatenate([kv[hh * GLA_DK:(hh + 1) * GLA_DK, hh * GLA_DV:(hh + 1) * GLA_DV]
                               for hh in range(GLA_HEADS)], axis=0)
        b_col = _dot_tn(la3_scr[n].astype(BF16), ones16)
        s1_ref[n] = jnp.exp(b_col) * s0 + upd
        return carry

    lax.fori_loop(0, G, seq_body, 0)

    o_new = _dot(pnew_scr[...].reshape(G * nrow, T).astype(BF16), v_new.astype(BF16))
    o_att = ((oc_scr[...].reshape(G * nrow, KV_WIDTH) + o_new)
             * rden_scr[...].reshape(G * nrow, LANES)).reshape(G, nrow, KV_WIDTH)
    o_a = jnp.concatenate(
        [jnp.where(lo, o_att[:, i * L:(i + 1) * L, :].reshape(T, LANES),
                   o_att[:, (i + ATT_GROUP) * L:(i + ATT_GROUP + 1) * L, :].reshape(T, LANES))
         for i in range(ATT_GROUP)], axis=1)
    oi = oi_scr[...]
    o_inter = jnp.concatenate([oi[:, hh * L:(hh + 1) * L, :].reshape(T, GLA_DV) for hh in range(GLA_HEADS)],
                              axis=1)
    intra = []
    for hh in range(GLA_HEADS):
        a = _dot_nt(jnp.where(hmasks[hh], q_hat, 0.0).astype(BF16), k_til)
        a = jnp.where(causal, a, 0.0).astype(BF16)
        intra.append(_dot(a, vg[:, hh * GLA_DV:(hh + 1) * GLA_DV].astype(BF16)))
    o_g = o_inter + jnp.concatenate(intra, axis=1)

    y_ref[...] = _layer_tail(x, gate, o_a, o_g, _gates(p), glang_ref, wba_ref, wbg_ref, wout_ref)


def _sample_layer(x, mod, cos, sin, weights, ck, cv, s0):
    G, L, T = SEQ_GROUP, DEC_SEQ, T_SAMPLE
    nrow = ATT_HEADS * L
    row_spec = lambda w: pl.BlockSpec((T, w), lambda i: (i, 0))
    seq_spec = lambda a, b: pl.BlockSpec((G, a, b), lambda i: (i, 0, 0))
    return pl.pallas_call(
        _sample_body,
        grid=(DEC_BATCH // G,),
        in_specs=[row_spec(D_MODEL), pl.BlockSpec((G, 3 * D_MODEL), lambda i: (i, 0)),
                  _const_spec((T, LANES), 1), _const_spec((T, LANES), 1)]
                 + _weight_specs(1)
                 + [seq_spec(KV_WIDTH, WINDOW), seq_spec(KV_WIDTH, WINDOW), seq_spec(GLA_KEY_WIDTH, GLA_DV)],
        out_specs=[row_spec(D_MODEL), seq_spec(KV_WIDTH, WINDOW), seq_spec(KV_WIDTH, WINDOW),
                   seq_spec(GLA_KEY_WIDTH, GLA_DV)],
        out_shape=[
            jax.ShapeDtypeStruct((DEC_BATCH * L, D_MODEL), F32),
            jax.ShapeDtypeStruct((DEC_BATCH, KV_WIDTH, WINDOW), F32),
            jax.ShapeDtypeStruct((DEC_BATCH, KV_WIDTH, WINDOW), F32),
            jax.ShapeDtypeStruct((DEC_BATCH, GLA_KEY_WIDTH, GLA_DV), F32),
        ],
        scratch_shapes=[
            pltpu.VMEM((G, nrow, LANES), F32),
            pltpu.VMEM((G, nrow, T), F32),
            pltpu.VMEM((G, nrow, KV_WIDTH), F32),
            pltpu.VMEM((G, nrow, T), F32),
            pltpu.VMEM((G, nrow, LANES), F32),
            pltpu.VMEM((G, GLA_HEADS * L, GLA_KEY_WIDTH), F32),
            pltpu.VMEM((G, 2 * L, GLA_KEY_WIDTH), F32),
            pltpu.VMEM((G, 2 * L, GLA_WIDTH), F32),
            pltpu.VMEM((G, 4 * L, GLA_KEY_WIDTH), F32),
            pltpu.VMEM((G, GLA_HEADS * L, GLA_DV), F32),
        ],
        compiler_params=pltpu.CompilerParams(dimension_semantics=("arbitrary",),
                                             vmem_limit_bytes=VMEM_LIMIT),
        name="sample_layer",
    )(x, mod, cos, sin, *weights, ck, cv, s0)


def _rope_tables(pos):
    half = ATT_HEAD_DIM // 2
    inv = 1.0 / (ROPE_THETA ** (np.arange(half, dtype=np.float64) / half))
    ang = pos.astype(np.float64)[:, None] * inv[None, :]
    c, s = np.cos(ang), np.sin(ang)
    return (np.tile(c, (1, 4)).astype(np.float32),
            np.concatenate([-s, s, -s, s], axis=1).astype(np.float32))


def _pair_heads(w):
    shape = w.shape
    w = w.reshape((ATT_KV_HEADS, ATT_GROUP, ATT_HEAD_DIM) + shape[1:])
    return jnp.swapaxes(w, 0, 1).reshape(shape)


def kernel(x_prompt, x_sample, cache_win_k, cache_win_v, state_gla, c_prompt, c_sample, norm_g, w_ada, b_ada, w_in, q_norm_g, k_norm_g, attn_sinks, w_gla_gate, b_gla_gate, gla_norm_g, w_branch_att, w_branch_gla, w_out):
    assert w_in.shape == (1, D_MODEL, R_END), "single-layer trunk"
    wt = jnp.swapaxes(w_in[0], 0, 1)
    wgate = jnp.concatenate([w_gla_gate[0], jnp.zeros((LANES - GLA_GATE_RANK, GLA_KEY_WIDTH), F32)],
                            axis=0).astype(BF16)
    weights = (
        norm_g[0][None, :],
        _pair_heads(wt[0:R_K]).astype(BF16), _pair_heads(wt[R_ZA:R_QG]).astype(BF16), wt.astype(BF16),
        jnp.tile(q_norm_g[0], 2)[None, :], jnp.tile(k_norm_g[0], 2)[None, :],
        attn_sinks[0], wgate, b_gla_gate[0][None, :], gla_norm_g[0][None, :],
        _pair_heads(w_branch_att[0]).astype(BF16), w_branch_gla[0].astype(BF16), w_out[0].astype(BF16),
    )

    c_all = jnp.concatenate([c_sample, c_prompt, jnp.zeros((MOD_ROWS - DEC_BATCH - BATCH, D_MODEL), F32)], axis=0)
    mod = _modulation(c_all, w_ada[0], b_ada[0][None, :])
    mod_prompt = mod[DEC_BATCH:DEC_BATCH + BATCH].reshape(BATCH, 3, D_MODEL)

    cos_p, sin_p = _rope_tables(np.arange(SEQ))
    cos_s, sin_s = _rope_tables(PAST_LEN + np.arange(DEC_SEQ))
    cos_s, sin_s = np.tile(cos_s, (SEQ_GROUP, 1)), np.tile(sin_s, (SEQ_GROUP, 1))

    def kv_rows(c):
        return jnp.transpose(c, (0, 2, 3, 1)).reshape(c.shape[0], KV_WIDTH, WINDOW)

    def kv_out(c):
        return jnp.transpose(c.reshape(c.shape[0], ATT_KV_HEADS, ATT_HEAD_DIM, WINDOW), (0, 3, 1, 2))[None]

    y_p, wk_p, wv_p, st_p = _prompt_layer(x_prompt, mod_prompt, cos_p, sin_p, weights)
    y_s, wk_s, wv_s, st_s = _sample_layer(
        x_sample.reshape(DEC_BATCH * DEC_SEQ, D_MODEL), mod, cos_s, sin_s, weights,
        kv_rows(cache_win_k[0]), kv_rows(cache_win_v[0]),
        state_gla[0].reshape(DEC_BATCH, GLA_KEY_WIDTH, GLA_DV))

    st_shape = (1, -1, GLA_HEADS, GLA_DK, GLA_DV)
    return (y_p, y_s.reshape(DEC_BATCH, DEC_SEQ, D_MODEL),
            kv_out(wk_p), kv_out(wv_p), st_p.reshape(st_shape),
            kv_out(wk_s), kv_out(wv_s), st_s.reshape(st_shape))
```

```python
import numpy as np

import jax
import jax.numpy as jnp
from jax import lax
from jax.experimental import pallas as pl
from jax.experimental.pallas import tpu as pltpu

F32 = jnp.float32
BF16 = jnp.bfloat16

D_MODEL = 1024
BATCH = 2
SEQ = 8192
DEC_BATCH = 128
DEC_SEQ = 8
PAST_LEN = 16384
ATT_HEADS = 8
ATT_KV_HEADS = 2
ATT_GROUP = ATT_HEADS // ATT_KV_HEADS
ATT_HEAD_DIM = 64
ATT_WIDTH = ATT_HEADS * ATT_HEAD_DIM
KV_WIDTH = ATT_KV_HEADS * ATT_HEAD_DIM
WINDOW = 128
ATT_BLOCK = 128
ROPE_THETA = 10000.0
GLA_HEADS = 4
GLA_WIDTH = D_MODEL // 2
GLA_DV = GLA_WIDTH // GLA_HEADS
GLA_KEY_WIDTH = GLA_WIDTH // 2
GLA_DK = GLA_KEY_WIDTH // GLA_HEADS
GLA_GATE_RANK = 16
GLA_GATE_TAU = 16.0
EPS = 1e-6
NEG = -1e30

LANES = 128
HALF = LANES // 2
R_K, R_V, R_ZA = ATT_WIDTH, ATT_WIDTH + KV_WIDTH, ATT_WIDTH + 2 * KV_WIDTH
R_QG = R_ZA + ATT_WIDTH
R_KG = R_QG + GLA_KEY_WIDTH
R_VG = R_KG + GLA_KEY_WIDTH
R_LR = R_VG + GLA_WIDTH
R_ZG = R_LR + GLA_GATE_RANK
R_MA = R_ZG + GLA_WIDTH
R_MG = R_MA + D_MODEL
R_END = R_MG + D_MODEL

T_PROMPT = 512
GLA_BLOCK = 64
SEQ_GROUP = 16
T_SAMPLE = SEQ_GROUP * DEC_SEQ
SEQ_UNROLL = 4
MOD_ROWS = DEC_BATCH + 16
MOD_TILE = 512
VMEM_LIMIT = 56 * 1024 * 1024


def _sigmoid(x):
    return 1.0 / (1.0 + jnp.exp(-x))


def _dot(a, b):
    return jnp.dot(a, b, preferred_element_type=F32)


def _dot_nt(a, b):
    return lax.dot_general(a, b, (((1,), (1,)), ((), ())), preferred_element_type=F32)


def _dot_tn(a, b):
    return lax.dot_general(a, b, (((0,), (0,)), ((), ())), preferred_element_type=F32)


def _split3(x):
    hi = x.astype(BF16)
    r1 = x - hi.astype(F32)
    mid = r1.astype(BF16)
    lo = (r1 - mid.astype(F32)).astype(BF16)
    return hi, mid, lo


def _norm_rope_slab(xs, g, cos, sin):
    lane = lax.broadcasted_iota(jnp.int32, xs.shape, 1)
    lo = lane < HALF
    first = (lane & (HALF // 2)) == 0
    sq = xs * xs
    s_lo = jnp.sum(jnp.where(lo, sq, 0.0), axis=-1, keepdims=True)
    s_hi = jnp.sum(jnp.where(lo, 0.0, sq), axis=-1, keepdims=True)
    inv = jnp.where(lo, lax.rsqrt(s_lo * (1.0 / ATT_HEAD_DIM) + EPS),
                    lax.rsqrt(s_hi * (1.0 / ATT_HEAD_DIM) + EPS))
    xn = xs * inv * g
    swapped = jnp.where(first, pltpu.roll(xn, LANES - HALF // 2, 1), pltpu.roll(xn, HALF // 2, 1))
    return xn * cos + swapped * sin


def _log_decay(lrz, wgate_ref, bgate_ref):
    pre = _dot(lrz.astype(BF16), wgate_ref[...]) + bgate_ref[...]
    log_sig = jnp.minimum(pre, 0.0) - jnp.log1p(jnp.exp(-jnp.abs(pre)))
    return log_sig * (1.0 / GLA_GATE_TAU)


def _head_masks(rows):
    lane = lax.broadcasted_iota(jnp.int32, (rows, GLA_KEY_WIDTH), 1)
    return [(lane >= h * GLA_DK) & (lane < (h + 1) * GLA_DK) for h in range(GLA_HEADS)]


class _Proj:
    def __init__(self, h, wq_ref, wza_ref, wt_ref):
        self.h, self.wq_ref, self.wza_ref, self.wt_ref = h, wq_ref, wza_ref, wt_ref

    def rows(self, a, b):
        return _dot_nt(self.h, self.wt_ref[a:b, :])

    def q_att(self):
        return _dot_nt(self.h, self.wq_ref[...])

    def z_att(self):
        return _dot_nt(self.h, self.wza_ref[...])


def _gates(p):
    z_a = p.z_att()
    z_g = p.rows(R_ZG, R_MA)
    return (z_a * _sigmoid(z_a), z_g * _sigmoid(z_g),
            _sigmoid(p.rows(R_MA, R_MG)), _sigmoid(p.rows(R_MG, R_END)))


def _layer_tail(x, gate, o_a, o_g, gates, glang_ref, wba_ref, wbg_ref, wout_ref):
    silu_a, silu_g, merge_a, merge_g = gates
    o_a = o_a * silu_a
    slabs = []
    for h in range(GLA_HEADS):
        oh = o_g[:, h * GLA_DV:(h + 1) * GLA_DV]
        ms = jnp.mean(oh * oh, axis=-1, keepdims=True)
        slabs.append(oh * lax.rsqrt(ms + EPS) * glang_ref[...])
    o_g = jnp.concatenate(slabs, axis=1) * silu_g
    merged = (merge_a * _dot(o_a.astype(BF16), wba_ref[...])
              + merge_g * _dot(o_g.astype(BF16), wbg_ref[...]))
    return x + gate * _dot(merged.astype(BF16), wout_ref[...])


def _mod_body(c_ref, w_ref, b_ref, o_ref, act_ref):
    @pl.when(pl.program_id(0) == 0)
    def _():
        c = c_ref[...]
        act_ref[...] = (c * _sigmoid(c)).astype(BF16)

    o_ref[...] = _dot(act_ref[...], w_ref[...].astype(BF16)) + b_ref[...]


def _modulation(c_all, w_ada, b_ada):
    rows = c_all.shape[0]
    return pl.pallas_call(
        _mod_body,
        grid=(3 * D_MODEL // MOD_TILE,),
        in_specs=[
            pl.BlockSpec((rows, D_MODEL), lambda i: (0, 0)),
            pl.BlockSpec((D_MODEL, MOD_TILE), lambda i: (0, i)),
            pl.BlockSpec((1, MOD_TILE), lambda i: (0, i)),
        ],
        out_specs=pl.BlockSpec((rows, MOD_TILE), lambda i: (0, i)),
        out_shape=jax.ShapeDtypeStruct((rows, 3 * D_MODEL), F32),
        scratch_shapes=[pltpu.VMEM((rows, D_MODEL), BF16)],
        compiler_params=pltpu.CompilerParams(dimension_semantics=("arbitrary",),
                                             vmem_limit_bytes=VMEM_LIMIT),
        name="adaln_mod",
    )(c_all, w_ada, b_ada)


def _softmax_block(s, mask, sink_ref):
    probs, rden = [], []
    for h in range(ATT_HEADS):
        sh = jnp.where(mask, s[h * ATT_BLOCK:(h + 1) * ATT_BLOCK], NEG)
        sink = sink_ref[h]
        m = jnp.maximum(jnp.max(sh, axis=-1, keepdims=True), sink)
        e = jnp.exp(sh - m)
        rden.append(1.0 / (jnp.sum(e, axis=-1, keepdims=True) + jnp.exp(sink - m)))
        probs.append(e.astype(BF16))
    return jnp.concatenate(probs, axis=0), rden


def _prompt_body(x_ref, mod_ref, cos_ref, sin_ref, ng_ref, wq_ref, wza_ref, wt_ref, qng_ref, kng_ref, sink_ref,
                 wgate_ref, bgate_ref, glang_ref, wba_ref, wbg_ref, wout_ref,
                 y_ref, wk_ref, wv_ref, st_ref,
                 kprev_ref, vprev_ref, st_scr):
    T, C = T_PROMPT, GLA_BLOCK
    n_att, n_gla = T // ATT_BLOCK, T // C
    j = pl.program_id(1)
    last = pl.num_programs(1) - 1

    @pl.when(j == 0)
    def _():
        kprev_ref[...] = jnp.zeros_like(kprev_ref)
        vprev_ref[...] = jnp.zeros_like(vprev_ref)
        st_scr[...] = jnp.zeros_like(st_scr)

    x = x_ref[...]
    shift, scale, gate = mod_ref[0:1, :], mod_ref[1:2, :], mod_ref[2:3, :]
    ms = jnp.mean(x * x, axis=-1, keepdims=True)
    h = (x * lax.rsqrt(ms + EPS) * ng_ref[...] * (1.0 + scale) + shift).astype(BF16)
    p = _Proj(h, wq_ref, wza_ref, wt_ref)
    cos, sin = cos_ref[...], sin_ref[...]
    lane = lax.broadcasted_iota(jnp.int32, (ATT_BLOCK, LANES), 1)
    lo = lane < HALF
    v_ = {}

    def a1_qkv():
        v_["q"], v_["k_raw"], v_["v"] = p.q_att(), p.rows(R_K, R_V), p.rows(R_V, R_ZA)

    def a2_norm_rope():
        q = v_["q"]
        v_["q_slabs"] = [_norm_rope_slab(q[:, i * LANES:(i + 1) * LANES], qng_ref[...], cos, sin)
                         * (ATT_HEAD_DIM ** -0.5) for i in range(ATT_GROUP)]
        k = _norm_rope_slab(v_["k_raw"], kng_ref[...], cos, sin)
        v = v_["v"]

        @pl.when(j == last)
        def _():
            wk_ref[...] = k[T - WINDOW:].T
            wv_ref[...] = v[T - WINDOW:].T

        v_["k16"], v_["v16"] = k.astype(BF16), v.astype(BF16)

    def a3_scores():
        k16, v16, q_slabs = v_["k16"], v_["v16"], v_["q_slabs"]
        r = lax.broadcasted_iota(jnp.int32, (ATT_BLOCK, 2 * ATT_BLOCK), 0)
        c = lax.broadcasted_iota(jnp.int32, (ATT_BLOCK, 2 * ATT_BLOCK), 1)
        band = ((c < ATT_BLOCK) & (c >= r)) | ((c >= ATT_BLOCK) & (c - ATT_BLOCK <= r))
        v_["masks"] = [band & ((c >= ATT_BLOCK) | (j > 0))] + [band] * (n_att - 1)
        scores, vcats = [], []
        for i in range(n_att):
            rows = slice(i * ATT_BLOCK, (i + 1) * ATT_BLOCK)
            if i == 0:
                kp, vp = kprev_ref[...], vprev_ref[...]
            else:
                prev = slice((i - 1) * ATT_BLOCK, i * ATT_BLOCK)
                kp, vp = k16[prev], v16[prev]
            pieces = ([jnp.where(lo, s[rows], 0.0).astype(BF16) for s in q_slabs]
                      + [jnp.where(lo, 0.0, s[rows]).astype(BF16) for s in q_slabs])
            scores.append(_dot_nt(jnp.concatenate(pieces, axis=0), jnp.concatenate([kp, k16[rows]], axis=0)))
            vcats.append(jnp.concatenate([vp, v16[rows]], axis=0))
        kprev_ref[...] = k16[T - ATT_BLOCK:]
        vprev_ref[...] = v16[T - ATT_BLOCK:]
        v_["scores"], v_["vcats"] = scores, vcats

    def a4_softmax():
        v_["soft"] = [_softmax_block(v_["scores"][i], v_["masks"][i], sink_ref) for i in range(n_att)]

    def a5_values():
        o_rows = []
        for i in range(n_att):
            probs, rden = v_["soft"][i]
            o = _dot(probs, v_["vcats"][i])
            o_rows.append(jnp.concatenate(
                [jnp.where(lo, o[g * ATT_BLOCK:(g + 1) * ATT_BLOCK] * rden[g],
                           o[(g + ATT_GROUP) * ATT_BLOCK:(g + ATT_GROUP + 1) * ATT_BLOCK] * rden[g + ATT_GROUP])
                 for g in range(ATT_GROUP)], axis=1))
        v_["o_a"] = jnp.concatenate(o_rows, axis=0)

    def g1_decay():
        v_["la"] = _log_decay(p.rows(R_LR, R_LR + LANES), wgate_ref, bgate_ref)

    def g2_qkv():
        v_["qg"] = p.rows(R_QG, R_KG) * (GLA_DK ** -0.5)
        v_["kg"] = p.rows(R_KG, R_VG)
        v_["vg16"] = p.rows(R_VG, R_LR).astype(BF16)

    def g3_cumsum():
        tr = lax.broadcasted_iota(jnp.int32, (T, T), 0)
        tc = lax.broadcasted_iota(jnp.int32, (T, T), 1)
        tri = (((tr // C) == (tc // C)) & (tc <= tr)).astype(BF16)
        hi, mid, lo3 = _split3(v_["la"])
        b = _dot(tri, hi) + _dot(tri, mid) + _dot(tri, lo3)
        v_["b"] = b
        v_["b_last"] = [b[(i + 1) * C - 1:(i + 1) * C] for i in range(n_gla)]

    def g4_decayed():
        b, qg, kg = v_["b"], v_["qg"], v_["kg"]
        b_end = jnp.concatenate([jnp.broadcast_to(bl, (C, GLA_KEY_WIDTH)) for bl in v_["b_last"]], axis=0)
        q_hat = qg * jnp.exp(b)
        v_["k_til"] = (kg * jnp.exp(-b)).astype(BF16)
        k_dec = kg * jnp.exp(b_end - b)
        hmasks = _head_masks(T)
        v_["q_m"] = [jnp.where(hm, q_hat, 0.0).astype(BF16) for hm in hmasks]
        v_["k_m"] = [jnp.where(hm, k_dec, 0.0).astype(BF16) for hm in hmasks]

    def g5_states():
        vg16 = v_["vg16"]
        states = [st_scr[...]]
        for i in range(n_gla):
            rows = slice(i * C, (i + 1) * C)
            v_stack = jnp.concatenate([vg16[rows, hh * GLA_DV:(hh + 1) * GLA_DV] for hh in range(GLA_HEADS)],
                                      axis=0)
            k_stack = jnp.concatenate([km[rows] for km in v_["k_m"]], axis=0)
            states.append(states[-1] * jnp.exp(v_["b_last"][i]) + _dot_tn(v_stack, k_stack))
        st_scr[...] = states[-1]

        @pl.when(j == last)
        def _():
            st_ref[...] = states[-1].T

        v_["states"] = states

    def g6_mixed():
        mixed = []
        for i in range(n_gla):
            rows = slice(i * C, (i + 1) * C)
            q_stack = jnp.concatenate([qm[rows] for qm in v_["q_m"]], axis=0)
            rhs = jnp.concatenate([v_["states"][i].astype(BF16), v_["k_til"][rows]], axis=0)
            mixed.append(_dot_nt(q_stack, rhs))
        v_["mixed"] = mixed

    def g7_outputs():
        sr = lax.broadcasted_iota(jnp.int32, (GLA_HEADS * C, C), 0)
        sc = lax.broadcasted_iota(jnp.int32, (GLA_HEADS * C, C), 1)
        causal = (sr & (C - 1)) >= sc
        vg16 = v_["vg16"]
        og_rows = []
        for i in range(n_gla):
            rows = slice(i * C, (i + 1) * C)
            m_i = v_["mixed"][i]
            a = jnp.where(causal, m_i[:, GLA_DV:], 0.0).astype(BF16)
            og_rows.append(jnp.concatenate(
                [m_i[hh * C:(hh + 1) * C, :GLA_DV]
                 + _dot(a[hh * C:(hh + 1) * C], vg16[rows, hh * GLA_DV:(hh + 1) * GLA_DV])
                 for hh in range(GLA_HEADS)], axis=1))
        v_["o_g"] = jnp.concatenate(og_rows, axis=0)

    def t1_silu_att():
        z = p.z_att()
        v_["silu_a"] = z * _sigmoid(z)

    def t2_silu_gla():
        z = p.rows(R_ZG, R_MA)
        v_["silu_g"] = z * _sigmoid(z)

    def t3_merge_att():
        v_["merge_a"] = _sigmoid(p.rows(R_MA, R_MG))

    def t4_merge_gla():
        v_["merge_g"] = _sigmoid(p.rows(R_MG, R_END))

    def t5_tail():
        gates = (v_["silu_a"], v_["silu_g"], v_["merge_a"], v_["merge_g"])
        y_ref[...] = _layer_tail(x, gate, v_["o_a"], v_["o_g"], gates, glang_ref, wba_ref, wbg_ref, wout_ref)

    for stage in (a1_qkv, g1_decay, g2_qkv, a2_norm_rope, g3_cumsum, a3_scores, t1_silu_att, t2_silu_gla,
                  g4_decayed, a4_softmax, t3_merge_att, g5_states, a5_values, t4_merge_gla, g6_mixed,
                  g7_outputs, t5_tail):
        stage()


def _const_spec(shape, nargs):
    zeros = (0,) * len(shape)
    if nargs == 1:
        return pl.BlockSpec(shape, lambda i: zeros)
    return pl.BlockSpec(shape, lambda b, j: zeros)


def _weight_specs(nargs):
    smem = pl.BlockSpec(memory_space=pltpu.SMEM)
    return [
        _const_spec((1, D_MODEL), nargs),
        _const_spec((ATT_WIDTH, D_MODEL), nargs),
        _const_spec((ATT_WIDTH, D_MODEL), nargs),
        _const_spec((R_END, D_MODEL), nargs),
        _const_spec((1, LANES), nargs),
        _const_spec((1, LANES), nargs),
        smem,
        _const_spec((LANES, GLA_KEY_WIDTH), nargs),
        _const_spec((1, GLA_KEY_WIDTH), nargs),
        _const_spec((1, GLA_DV), nargs),
        _const_spec((ATT_WIDTH, D_MODEL), nargs),
        _const_spec((GLA_WIDTH, D_MODEL), nargs),
        _const_spec((D_MODEL, D_MODEL), nargs),
    ]


def _prompt_layer(x, mod, cos, sin, weights):
    nblk = SEQ // T_PROMPT
    return pl.pallas_call(
        _prompt_body,
        grid=(BATCH, nblk),
        in_specs=[
            pl.BlockSpec((None, T_PROMPT, D_MODEL), lambda b, j: (b, j, 0)),
            pl.BlockSpec((None, 3, D_MODEL), lambda b, j: (b, 0, 0)),
            pl.BlockSpec((T_PROMPT, LANES), lambda b, j: (j, 0)),
            pl.BlockSpec((T_PROMPT, LANES), lambda b, j: (j, 0)),
        ] + _weight_specs(2),
        out_specs=[
            pl.BlockSpec((None, T_PROMPT, D_MODEL), lambda b, j: (b, j, 0)),
            pl.BlockSpec((None, KV_WIDTH, WINDOW), lambda b, j: (b, 0, 0)),
            pl.BlockSpec((None, KV_WIDTH, WINDOW), lambda b, j: (b, 0, 0)),
            pl.BlockSpec((None, GLA_KEY_WIDTH, GLA_DV), lambda b, j: (b, 0, 0)),
        ],
        out_shape=[
            jax.ShapeDtypeStruct((BATCH, SEQ, D_MODEL), F32),
            jax.ShapeDtypeStruct((BATCH, KV_WIDTH, WINDOW), F32),
            jax.ShapeDtypeStruct((BATCH, KV_WIDTH, WINDOW), F32),
            jax.ShapeDtypeStruct((BATCH, GLA_KEY_WIDTH, GLA_DV), F32),
        ],
        scratch_shapes=[
            pltpu.VMEM((ATT_BLOCK, KV_WIDTH), BF16),
            pltpu.VMEM((ATT_BLOCK, KV_WIDTH), BF16),
            pltpu.VMEM((GLA_DV, GLA_KEY_WIDTH), F32),
        ],
        compiler_params=pltpu.CompilerParams(dimension_semantics=("arbitrary", "arbitrary"),
                                             vmem_limit_bytes=VMEM_LIMIT),
        name="prompt_layer",
    )(x, mod, cos, sin, *weights)


def _sample_body(x_ref, mod_ref, cos_ref, sin_ref, ng_ref, wq_ref, wza_ref, wt_ref, qng_ref, kng_ref, sink_ref,
                 wgate_ref, bgate_ref, glang_ref, wba_ref, wbg_ref, wout_ref,
                 ck_ref, cv_ref, s0_ref,
                 y_ref, ok_ref, ov_ref, s1_ref,
                 qrows_scr, snew_scr, oc_scr, pnew_scr, rden_scr, qhat_scr, kdec_scr, vg_scr, la3_scr, oi_scr):
    G, L, T = SEQ_GROUP, DEC_SEQ, T_SAMPLE
    x = x_ref[...]

    def per_token(a, b):
        return jnp.concatenate([jnp.broadcast_to(mod_ref[n:n + 1, a:b], (L, b - a)) for n in range(G)], axis=0)

    shift = per_token(0, D_MODEL)
    scale = per_token(D_MODEL, 2 * D_MODEL)
    gate = per_token(2 * D_MODEL, 3 * D_MODEL)
    ms = jnp.mean(x * x, axis=-1, keepdims=True)
    h = (x * lax.rsqrt(ms + EPS) * ng_ref[...] * (1.0 + scale) + shift).astype(BF16)
    p = _Proj(h, wq_ref, wza_ref, wt_ref)

    cos, sin = cos_ref[...], sin_ref[...]
    lane = lax.broadcasted_iota(jnp.int32, (T, LANES), 1)
    lo = lane < HALF
    q = p.q_att()
    for i in range(ATT_GROUP):
        s = _norm_rope_slab(q[:, i * LANES:(i + 1) * LANES], qng_ref[...], cos, sin) * (ATT_HEAD_DIM ** -0.5)
        qrows_scr[:, i * L:(i + 1) * L, :] = jnp.where(lo, s, 0.0).reshape(G, L, LANES)
        qrows_scr[:, (i + ATT_GROUP) * L:(i + ATT_GROUP + 1) * L, :] = jnp.where(lo, 0.0, s).reshape(G, L, LANES)
    k_new = _norm_rope_slab(p.rows(R_K, R_V), kng_ref[...], cos, sin)
    v_new = p.rows(R_V, R_ZA)
    k_new_t, v_new_t = k_new.T, v_new.T
    nrow = ATT_HEADS * L
    snew_scr[...] = _dot(qrows_scr[...].reshape(G * nrow, LANES).astype(BF16),
                         k_new_t.astype(BF16)).reshape(G, nrow, T)

    qg = p.rows(R_QG, R_KG) * (GLA_DK ** -0.5)
    kg = p.rows(R_KG, R_VG)
    vg = p.rows(R_VG, R_LR)
    la = _log_decay(p.rows(R_LR, R_LR + LANES), wgate_ref, bgate_ref)
    tr = lax.broadcasted_iota(jnp.int32, (T, T), 0)
    tc = lax.broadcasted_iota(jnp.int32, (T, T), 1)
    same_seq = (tr // L) == (tc // L)
    causal = same_seq & (tc <= tr)
    hi, mid, lo3 = _split3(la)
    tri = causal.astype(BF16)
    blk = same_seq.astype(BF16)
    b = _dot(tri, hi) + _dot(tri, mid) + _dot(tri, lo3)
    b_last = _dot(blk, hi) + _dot(blk, mid) + _dot(blk, lo3)
    q_hat = qg * jnp.exp(b)
    k_til = (kg * jnp.exp(-b)).astype(BF16)
    k_dec = kg * jnp.exp(b_last - b)
    hmasks = _head_masks(T)
    zeros_l = jnp.zeros((G, L, GLA_KEY_WIDTH), F32)
    for hh in range(GLA_HEADS):
        qhat_scr[:, hh * L:(hh + 1) * L, :] = jnp.where(hmasks[hh], q_hat, 0.0).reshape(G, L, GLA_KEY_WIDTH)
    kdec_scr[:, 0:L, :] = k_dec.reshape(G, L, GLA_KEY_WIDTH)
    kdec_scr[:, L:2 * L, :] = zeros_l
    vg_scr[:, 0:L, :] = vg.reshape(G, L, GLA_WIDTH)
    vg_scr[:, L:2 * L, :] = jnp.zeros((G, L, GLA_WIDTH), F32)
    la3_scr[:, 0:L, :] = hi.astype(F32).reshape(G, L, GLA_KEY_WIDTH)
    la3_scr[:, L:2 * L, :] = mid.astype(F32).reshape(G, L, GLA_KEY_WIDTH)
    la3_scr[:, 2 * L:3 * L, :] = lo3.astype(F32).reshape(G, L, GLA_KEY_WIDTH)
    la3_scr[:, 3 * L:4 * L, :] = zeros_l

    rr = lax.broadcasted_iota(jnp.int32, (nrow, LANES), 0) & (L - 1)
    cc = lax.broadcasted_iota(jnp.int32, (nrow, LANES), 1)
    cache_mask = cc >= rr
    sink_col = jnp.concatenate([jnp.full((L, 1), sink_ref[hh], F32) for hh in range(ATT_HEADS)], axis=0)
    keep_old = lax.broadcasted_iota(jnp.int32, (KV_WIDTH, WINDOW), 1) < WINDOW - L
    ones16 = jnp.ones((4 * L, GLA_DV), BF16)

    def seq_body(it, carry):
        ns = [it * SEQ_UNROLL + u for u in range(SEQ_UNROLL)]
        kc = [ck_ref[n] for n in ns]
        vc = [cv_ref[n] for n in ns]
        s_c = [jnp.where(cache_mask, _dot(qrows_scr[n].astype(BF16), k.astype(BF16)), NEG)
               for n, k in zip(ns, kc)]
        st0 = [s0_ref[n] for n in ns]
        for n, s0 in zip(ns, st0):
            oi_scr[n] = _dot(qhat_scr[n].astype(BF16), s0.astype(BF16))
        kv = [_dot_tn(kdec_scr[n].astype(BF16), vg_scr[n].astype(BF16)) for n in ns]
        b_col = [_dot_tn(la3_scr[n].astype(BF16), ones16) for n in ns]
        s_n = [jnp.where((cc >= n * L) & (cc <= n * L + rr), snew_scr[n], NEG) for n in ns]
        m = [jnp.maximum(jnp.maximum(jnp.max(a, axis=-1, keepdims=True), jnp.max(b_, axis=-1, keepdims=True)),
                         sink_col) for a, b_ in zip(s_c, s_n)]
        e_c = [jnp.exp(a - mm) for a, mm in zip(s_c, m)]
        e_n = [jnp.exp(a - mm) for a, mm in zip(s_n, m)]
        for n, k, v in zip(ns, kc, vc):
            new_shift = WINDOW - L - n * L
            ok_ref[n] = jnp.where(keep_old, pltpu.roll(k, WINDOW - L, 1), pltpu.roll(k_new_t, new_shift, 1))
            ov_ref[n] = jnp.where(keep_old, pltpu.roll(v, WINDOW - L, 1), pltpu.roll(v_new_t, new_shift, 1))
        for n, ec, en, mm, v in zip(ns, e_c, e_n, m, vc):
            denom = (jnp.sum(ec, axis=-1, keepdims=True) + jnp.sum(en, axis=-1, keepdims=True)
                     + jnp.exp(sink_col - mm))
            oc_scr[n] = _dot_nt(ec.astype(BF16), v.astype(BF16))
            pnew_scr[n] = en
            rden_scr[n] = jnp.broadcast_to(1.0 / denom, (nrow, LANES))
        for n, s0, kv_n, bc in zip(ns, st0, kv, b_col):
            upd = jnp.concatenate([kv_n[hh * GLA_DK:(hh + 1) * GLA_DK, hh * GLA_DV:(hh + 1) * GLA_DV]
                                   for hh in range(GLA_HEADS)], axis=0)
            s1_ref[n] = jnp.exp(bc) * s0 + upd
        return carry

    lax.fori_loop(0, G // SEQ_UNROLL, seq_body, 0)

    o_new = _dot(pnew_scr[...].reshape(G * nrow, T).astype(BF16), v_new.astype(BF16))
    o_att = ((oc_scr[...].reshape(G * nrow, KV_WIDTH) + o_new)
             * rden_scr[...].reshape(G * nrow, LANES)).reshape(G, nrow, KV_WIDTH)
    o_a = jnp.concatenate(
        [jnp.where(lo, o_att[:, i * L:(i + 1) * L, :].reshape(T, LANES),
                   o_att[:, (i + ATT_GROUP) * L:(i + ATT_GROUP + 1) * L, :].reshape(T, LANES))
         for i in range(ATT_GROUP)], axis=1)
    oi = oi_scr[...]
    o_inter = jnp.concatenate([oi[:, hh * L:(hh + 1) * L, :].reshape(T, GLA_DV) for hh in range(GLA_HEADS)],
                              axis=1)
    intra = []
    for hh in range(GLA_HEADS):
        a = _dot_nt(jnp.where(hmasks[hh], q_hat, 0.0).astype(BF16), k_til)
        a = jnp.where(causal, a, 0.0).astype(BF16)
        intra.append(_dot(a, vg[:, hh * GLA_DV:(hh + 1) * GLA_DV].astype(BF16)))
    o_g = o_inter + jnp.concatenate(intra, axis=1)

    y_ref[...] = _layer_tail(x, gate, o_a, o_g, _gates(p), glang_ref, wba_ref, wbg_ref, wout_ref)


def _sample_layer(x, mod, cos, sin, weights, ck, cv, s0):
    G, L, T = SEQ_GROUP, DEC_SEQ, T_SAMPLE
    nrow = ATT_HEADS * L
    row_spec = lambda w: pl.BlockSpec((T, w), lambda i: (i, 0))
    seq_spec = lambda a, b: pl.BlockSpec((G, a, b), lambda i: (i, 0, 0))
    return pl.pallas_call(
        _sample_body,
        grid=(DEC_BATCH // G,),
        in_specs=[row_spec(D_MODEL), pl.BlockSpec((G, 3 * D_MODEL), lambda i: (i, 0)),
                  _const_spec((T, LANES), 1), _const_spec((T, LANES), 1)]
                 + _weight_specs(1)
                 + [seq_spec(KV_WIDTH, WINDOW), seq_spec(KV_WIDTH, WINDOW), seq_spec(GLA_KEY_WIDTH, GLA_DV)],
        out_specs=[row_spec(D_MODEL), seq_spec(KV_WIDTH, WINDOW), seq_spec(KV_WIDTH, WINDOW),
                   seq_spec(GLA_KEY_WIDTH, GLA_DV)],
        out_shape=[
            jax.ShapeDtypeStruct((DEC_BATCH * L, D_MODEL), F32),
            jax.ShapeDtypeStruct((DEC_BATCH, KV_WIDTH, WINDOW), F32),
            jax.ShapeDtypeStruct((DEC_BATCH, KV_WIDTH, WINDOW), F32),
            jax.ShapeDtypeStruct((DEC_BATCH, GLA_KEY_WIDTH, GLA_DV), F32),
        ],
        scratch_shapes=[
            pltpu.VMEM((G, nrow, LANES), F32),
            pltpu.VMEM((G, nrow, T), F32),
            pltpu.VMEM((G, nrow, KV_WIDTH), F32),
            pltpu.VMEM((G, nrow, T), F32),
            pltpu.VMEM((G, nrow, LANES), F32),
            pltpu.VMEM((G, GLA_HEADS * L, GLA_KEY_WIDTH), F32),
            pltpu.VMEM((G, 2 * L, GLA_KEY_WIDTH), F32),
            pltpu.VMEM((G, 2 * L, GLA_WIDTH), F32),
            pltpu.VMEM((G, 4 * L, GLA_KEY_WIDTH), F32),
            pltpu.VMEM((G, GLA_HEADS * L, GLA_DV), F32),
        ],
        compiler_params=pltpu.CompilerParams(dimension_semantics=("arbitrary",),
                                             vmem_limit_bytes=VMEM_LIMIT),
        name="sample_layer",
    )(x, mod, cos, sin, *weights, ck, cv, s0)


def _rope_tables(pos):
    half = ATT_HEAD_DIM // 2
    inv = 1.0 / (ROPE_THETA ** (np.arange(half, dtype=np.float64) / half))
    ang = pos.astype(np.float64)[:, None] * inv[None, :]
    c, s = np.cos(ang), np.sin(ang)
    return (np.tile(c, (1, 4)).astype(np.float32),
            np.concatenate([-s, s, -s, s], axis=1).astype(np.float32))


def _pair_heads(w):
    shape = w.shape
    w = w.reshape((ATT_KV_HEADS, ATT_GROUP, ATT_HEAD_DIM) + shape[1:])
    return jnp.swapaxes(w, 0, 1).reshape(shape)


def kernel(x_prompt, x_sample, cache_win_k, cache_win_v, state_gla, c_prompt, c_sample, norm_g, w_ada, b_ada, w_in, q_norm_g, k_norm_g, attn_sinks, w_gla_gate, b_gla_gate, gla_norm_g, w_branch_att, w_branch_gla, w_out):
    assert w_in.shape == (1, D_MODEL, R_END), "single-layer trunk"
    wt = jnp.swapaxes(w_in[0], 0, 1)
    wgate = jnp.concatenate([w_gla_gate[0], jnp.zeros((LANES - GLA_GATE_RANK, GLA_KEY_WIDTH), F32)],
                            axis=0).astype(BF16)
    weights = (
        norm_g[0][None, :],
        _pair_heads(wt[0:R_K]).astype(BF16), _pair_heads(wt[R_ZA:R_QG]).astype(BF16), wt.astype(BF16),
        jnp.tile(q_norm_g[0], 2)[None, :], jnp.tile(k_norm_g[0], 2)[None, :],
        attn_sinks[0], wgate, b_gla_gate[0][None, :], gla_norm_g[0][None, :],
        _pair_heads(w_branch_att[0]).astype(BF16), w_branch_gla[0].astype(BF16), w_out[0].astype(BF16),
    )

    c_all = jnp.concatenate([c_sample, c_prompt, jnp.zeros((MOD_ROWS - DEC_BATCH - BATCH, D_MODEL), F32)], axis=0)
    mod = _modulation(c_all, w_ada[0], b_ada[0][None, :])
    mod_prompt = mod[DEC_BATCH:DEC_BATCH + BATCH].reshape(BATCH, 3, D_MODEL)

    cos_p, sin_p = _rope_tables(np.arange(SEQ))
    cos_s, sin_s = _rope_tables(PAST_LEN + np.arange(DEC_SEQ))
    cos_s, sin_s = np.tile(cos_s, (SEQ_GROUP, 1)), np.tile(sin_s, (SEQ_GROUP, 1))

    def kv_rows(c):
        return jnp.transpose(c, (0, 2, 3, 1)).reshape(c.shape[0], KV_WIDTH, WINDOW)

    def kv_out(c):
        return jnp.transpose(c.reshape(c.shape[0], ATT_KV_HEADS, ATT_HEAD_DIM, WINDOW), (0, 3, 1, 2))[None]

    y_p, wk_p, wv_p, st_p = _prompt_layer(x_prompt, mod_prompt, cos_p, sin_p, weights)
    y_s, wk_s, wv_s, st_s = _sample_layer(
        x_sample.reshape(DEC_BATCH * DEC_SEQ, D_MODEL), mod, cos_s, sin_s, weights,
        kv_rows(cache_win_k[0]), kv_rows(cache_win_v[0]),
        state_gla[0].reshape(DEC_BATCH, GLA_KEY_WIDTH, GLA_DV))

    st_shape = (1, -1, GLA_HEADS, GLA_DK, GLA_DV)
    return (y_p, y_s.reshape(DEC_BATCH, DEC_SEQ, D_MODEL),
            kv_out(wk_p), kv_out(wv_p), st_p.reshape(st_shape),
            kv_out(wk_s), kv_out(wv_s), st_s.reshape(st_shape))
```

```python
import numpy as np

import jax
import jax.numpy as jnp
from jax import lax
from jax.experimental import pallas as pl
from jax.experimental.pallas import tpu as pltpu

F32 = jnp.float32
BF16 = jnp.bfloat16

D_MODEL = 1024
BATCH = 2
SEQ = 8192
DEC_BATCH = 128
DEC_SEQ = 8
PAST_LEN = 16384
ATT_HEADS = 8
ATT_KV_HEADS = 2
ATT_GROUP = ATT_HEADS // ATT_KV_HEADS
ATT_HEAD_DIM = 64
ATT_WIDTH = ATT_HEADS * ATT_HEAD_DIM
KV_WIDTH = ATT_KV_HEADS * ATT_HEAD_DIM
WINDOW = 128
ATT_BLOCK = 128
ROPE_THETA = 10000.0
GLA_HEADS = 4
GLA_WIDTH = D_MODEL // 2
GLA_DV = GLA_WIDTH // GLA_HEADS
GLA_KEY_WIDTH = GLA_WIDTH // 2
GLA_DK = GLA_KEY_WIDTH // GLA_HEADS
GLA_GATE_RANK = 16
GLA_GATE_TAU = 16.0
EPS = 1e-6
NEG = -1e30

LANES = 128
HALF = LANES // 2
MXU_TILE = 256
R_K, R_V, R_ZA = ATT_WIDTH, ATT_WIDTH + KV_WIDTH, ATT_WIDTH + 2 * KV_WIDTH
R_QG = R_ZA + ATT_WIDTH
R_KG = R_QG + GLA_KEY_WIDTH
R_VG = R_KG + GLA_KEY_WIDTH
R_LR = R_VG + GLA_WIDTH
R_ZG = R_LR + GLA_GATE_RANK
R_MA = R_ZG + GLA_WIDTH
R_MG = R_MA + D_MODEL
R_END = R_MG + D_MODEL

T_PROMPT = 512
GLA_BLOCK = 64
SEQ_GROUP = 16
T_SAMPLE = SEQ_GROUP * DEC_SEQ
SEQ_UNROLL = 4
MOD_ROWS = DEC_BATCH + 16
MOD_TILE = 512
VMEM_LIMIT = 56 * 1024 * 1024


def _sigmoid(x):
    return 1.0 / (1.0 + jnp.exp(-x))


def _dot(a, b):
    return jnp.dot(a, b, preferred_element_type=F32)


def _dot_nt(a, b):
    return lax.dot_general(a, b, (((1,), (1,)), ((), ())), preferred_element_type=F32)


def _dot_tn(a, b):
    return lax.dot_general(a, b, (((0,), (0,)), ((), ())), preferred_element_type=F32)


def _split3(x):
    hi = x.astype(BF16)
    r1 = x - hi.astype(F32)
    mid = r1.astype(BF16)
    lo = (r1 - mid.astype(F32)).astype(BF16)
    return hi, mid, lo


def _norm_rope_slab(xs, g, cos, sin):
    lane = lax.broadcasted_iota(jnp.int32, xs.shape, 1)
    lo = lane < HALF
    first = (lane & (HALF // 2)) == 0
    sq = xs * xs
    s_lo = jnp.sum(jnp.where(lo, sq, 0.0), axis=-1, keepdims=True)
    s_hi = jnp.sum(jnp.where(lo, 0.0, sq), axis=-1, keepdims=True)
    inv = jnp.where(lo, lax.rsqrt(s_lo * (1.0 / ATT_HEAD_DIM) + EPS),
                    lax.rsqrt(s_hi * (1.0 / ATT_HEAD_DIM) + EPS))
    xn = xs * inv * g
    swapped = jnp.where(first, pltpu.roll(xn, LANES - HALF // 2, 1), pltpu.roll(xn, HALF // 2, 1))
    return xn * cos + swapped * sin


def _log_decay(lrz, wgate_ref, bgate_ref):
    pre = _dot(lrz.astype(BF16), wgate_ref[...]) + bgate_ref[...]
    log_sig = jnp.minimum(pre, 0.0) - jnp.log(1.0 + jnp.exp(-jnp.abs(pre)))
    return log_sig * (1.0 / GLA_GATE_TAU)


def _head_masks(rows):
    lane = lax.broadcasted_iota(jnp.int32, (rows, GLA_KEY_WIDTH), 1)
    return [(lane >= h * GLA_DK) & (lane < (h + 1) * GLA_DK) for h in range(GLA_HEADS)]


class _Proj:
    def __init__(self, h, wq_ref, wza_ref, wt_ref):
        self.h, self.wq_ref, self.wza_ref, self.wt_ref = h, wq_ref, wza_ref, wt_ref

    def rows(self, a, b):
        return _dot_nt(self.h, self.wt_ref[a:b, :])

    def q_att(self):
        return _dot_nt(self.h, self.wq_ref[...])

    def z_att(self):
        return _dot_nt(self.h, self.wza_ref[...])


def _gates(p):
    z_a = p.z_att()
    z_g = p.rows(R_ZG, R_MA)
    return (z_a * _sigmoid(z_a), z_g * _sigmoid(z_g),
            _sigmoid(p.rows(R_MA, R_MG)), _sigmoid(p.rows(R_MG, R_END)))


def _layer_tail(x, gate, o_a, o_g, gates, glang_ref, wba_ref, wbg_ref, wout_ref):
    silu_a, silu_g, merge_a, merge_g = gates
    o_a = o_a * silu_a
    slabs = []
    for h in range(GLA_HEADS):
        oh = o_g[:, h * GLA_DV:(h + 1) * GLA_DV]
        ms = jnp.mean(oh * oh, axis=-1, keepdims=True)
        slabs.append(oh * lax.rsqrt(ms + EPS) * glang_ref[...])
    o_g = jnp.concatenate(slabs, axis=1) * silu_g
    merged = (merge_a * _dot(o_a.astype(BF16), wba_ref[...])
              + merge_g * _dot(o_g.astype(BF16), wbg_ref[...]))
    return x + gate * _dot(merged.astype(BF16), wout_ref[...])


def _mod_body(c_ref, w_ref, b_ref, o_ref, act_ref):
    @pl.when(pl.program_id(0) == 0)
    def _():
        c = c_ref[...]
        act_ref[...] = (c * _sigmoid(c)).astype(BF16)

    o_ref[...] = _dot(act_ref[...], w_ref[...].astype(BF16)) + b_ref[...]


def _modulation(c_all, w_ada, b_ada):
    rows = c_all.shape[0]
    return pl.pallas_call(
        _mod_body,
        grid=(3 * D_MODEL // MOD_TILE,),
        in_specs=[
            pl.BlockSpec((rows, D_MODEL), lambda i: (0, 0)),
            pl.BlockSpec((D_MODEL, MOD_TILE), lambda i: (0, i)),
            pl.BlockSpec((1, MOD_TILE), lambda i: (0, i)),
        ],
        out_specs=pl.BlockSpec((rows, MOD_TILE), lambda i: (0, i)),
        out_shape=jax.ShapeDtypeStruct((rows, 3 * D_MODEL), F32),
        scratch_shapes=[pltpu.VMEM((rows, D_MODEL), BF16)],
        compiler_params=pltpu.CompilerParams(dimension_semantics=("arbitrary",),
                                             vmem_limit_bytes=VMEM_LIMIT),
        name="adaln_mod",
    )(c_all, w_ada, b_ada)


def _softmax_block(s, mask, sink_ref):
    probs, rden = [], []
    for h in range(ATT_HEADS):
        sh = jnp.where(mask, s[h * ATT_BLOCK:(h + 1) * ATT_BLOCK], NEG)
        sink = sink_ref[h]
        m = jnp.maximum(jnp.max(sh, axis=-1, keepdims=True), sink)
        e = jnp.exp(sh - m)
        rden.append(1.0 / (jnp.sum(e, axis=-1, keepdims=True) + jnp.exp(sink - m)))
        probs.append(e.astype(BF16))
    return jnp.concatenate(probs, axis=0), rden


def _prompt_body(x_ref, mod_ref, cos_ref, sin_ref, ng_ref, wq_ref, wza_ref, wt_ref, qng_ref, kng_ref, sink_ref,
                 wgate_ref, bgate_ref, glang_ref, wba_ref, wbg_ref, wout_ref,
                 y_ref, wk_ref, wv_ref, st_ref,
                 kprev_ref, vprev_ref, st_scr, klast_scr, vlast_scr):
    T, C = T_PROMPT, GLA_BLOCK
    n_att, n_gla = T // ATT_BLOCK, T // C
    j = pl.program_id(1)
    last = pl.num_programs(1) - 1

    @pl.when(j == 0)
    def _():
        kprev_ref[...] = jnp.zeros_like(kprev_ref)
        vprev_ref[...] = jnp.zeros_like(vprev_ref)
        st_scr[...] = jnp.zeros_like(st_scr)

    x = x_ref[...]
    shift, scale, gate = mod_ref[0:1, :], mod_ref[1:2, :], mod_ref[2:3, :]
    ms = jnp.mean(x * x, axis=-1, keepdims=True)
    h = (x * lax.rsqrt(ms + EPS) * ng_ref[...] * (1.0 + scale) + shift).astype(BF16)
    p = _Proj(h, wq_ref, wza_ref, wt_ref)
    cos, sin = cos_ref[...], sin_ref[...]
    lane = lax.broadcasted_iota(jnp.int32, (ATT_BLOCK, LANES), 1)
    lo = lane < HALF
    v_ = {}

    def a1_qkv():
        kv = p.rows(R_K, R_ZA)
        v_["q"], v_["k_raw"], v_["v"] = p.q_att(), kv[:, :KV_WIDTH], kv[:, KV_WIDTH:]

    def a2_norm_rope():
        q = v_["q"]
        v_["q_slabs"] = [_norm_rope_slab(q[:, i * LANES:(i + 1) * LANES], qng_ref[...], cos, sin)
                         * (ATT_HEAD_DIM ** -0.5) for i in range(ATT_GROUP)]
        k = _norm_rope_slab(v_["k_raw"], kng_ref[...], cos, sin)
        v = v_["v"]
        klast_scr[...] = k[T - WINDOW:]
        vlast_scr[...] = v[T - WINDOW:]
        v_["k16"], v_["v16"] = k.astype(BF16), v.astype(BF16)

    def a3_scores():
        k16, v16, q_slabs = v_["k16"], v_["v16"], v_["q_slabs"]
        r = lax.broadcasted_iota(jnp.int32, (ATT_BLOCK, 2 * ATT_BLOCK), 0)
        c = lax.broadcasted_iota(jnp.int32, (ATT_BLOCK, 2 * ATT_BLOCK), 1)
        band = ((c < ATT_BLOCK) & (c >= r)) | ((c >= ATT_BLOCK) & (c - ATT_BLOCK <= r))
        v_["masks"] = [band & ((c >= ATT_BLOCK) | (j > 0))] + [band] * (n_att - 1)
        scores, vcats = [], []
        for i in range(n_att):
            rows = slice(i * ATT_BLOCK, (i + 1) * ATT_BLOCK)
            if i == 0:
                kp, vp = kprev_ref[...], vprev_ref[...]
            else:
                prev = slice((i - 1) * ATT_BLOCK, i * ATT_BLOCK)
                kp, vp = k16[prev], v16[prev]
            pieces = ([jnp.where(lo, s[rows], 0.0).astype(BF16) for s in q_slabs]
                      + [jnp.where(lo, 0.0, s[rows]).astype(BF16) for s in q_slabs])
            scores.append(_dot_nt(jnp.concatenate(pieces, axis=0), jnp.concatenate([kp, k16[rows]], axis=0)))
            vcats.append(jnp.concatenate([vp, v16[rows]], axis=0))
        kprev_ref[...] = k16[T - ATT_BLOCK:]
        vprev_ref[...] = v16[T - ATT_BLOCK:]
        v_["scores"], v_["vcats"] = scores, vcats

    def a4_softmax():
        v_["soft"] = [_softmax_block(v_["scores"][i], v_["masks"][i], sink_ref) for i in range(n_att)]

    def a5_values():
        o_rows = []
        for i in range(n_att):
            probs, rden = v_["soft"][i]
            o = _dot(probs, v_["vcats"][i])
            o_rows.append(jnp.concatenate(
                [jnp.where(lo, o[g * ATT_BLOCK:(g + 1) * ATT_BLOCK] * rden[g],
                           o[(g + ATT_GROUP) * ATT_BLOCK:(g + ATT_GROUP + 1) * ATT_BLOCK] * rden[g + ATT_GROUP])
                 for g in range(ATT_GROUP)], axis=1))
        v_["o_a"] = jnp.concatenate(o_rows, axis=0)

    def g1_decay():
        v_["la"] = _log_decay(p.rows(R_LR, R_LR + LANES), wgate_ref, bgate_ref)

    def g2_qkv():
        v_["qg"] = p.rows(R_QG, R_KG) * (GLA_DK ** -0.5)
        v_["kg"] = p.rows(R_KG, R_VG)
        v_["vg16"] = p.rows(R_VG, R_LR).astype(BF16)

    def g3_cumsum():
        tr = lax.broadcasted_iota(jnp.int32, (T, T), 0)
        tc = lax.broadcasted_iota(jnp.int32, (T, T), 1)
        tri = (((tr // C) == (tc // C)) & (tc <= tr)).astype(BF16)
        hi, mid, lo3 = _split3(v_["la"])
        b = _dot(tri, hi) + _dot(tri, mid) + _dot(tri, lo3)
        v_["b"] = b
        v_["b_last"] = [b[(i + 1) * C - 1:(i + 1) * C] for i in range(n_gla)]

    def g4_decayed():
        b, qg, kg = v_["b"], v_["qg"], v_["kg"]
        b_end = jnp.concatenate([jnp.broadcast_to(bl, (C, GLA_KEY_WIDTH)) for bl in v_["b_last"]], axis=0)
        q_hat = qg * jnp.exp(b)
        v_["k_til"] = (kg * jnp.exp(-b)).astype(BF16)
        k_dec = kg * jnp.exp(b_end - b)
        hmasks = _head_masks(T)
        v_["q_m"] = [jnp.where(hm, q_hat, 0.0).astype(BF16) for hm in hmasks]
        v_["k_m"] = [jnp.where(hm, k_dec, 0.0).astype(BF16) for hm in hmasks]

    def g5_states():
        vg16 = v_["vg16"]
        states = [st_scr[...]]
        for i in range(n_gla):
            rows = slice(i * C, (i + 1) * C)
            v_stack = jnp.concatenate([vg16[rows, hh * GLA_DV:(hh + 1) * GLA_DV] for hh in range(GLA_HEADS)],
                                      axis=0)
            k_stack = jnp.concatenate([km[rows] for km in v_["k_m"]], axis=0)
            states.append(states[-1] * jnp.exp(v_["b_last"][i]) + _dot_tn(v_stack, k_stack))
        st_scr[...] = states[-1]
        v_["states"] = states

    def g6_mixed():
        mixed = []
        for i in range(n_gla):
            rows = slice(i * C, (i + 1) * C)
            q_stack = jnp.concatenate([qm[rows] for qm in v_["q_m"]], axis=0)
            rhs = jnp.concatenate([v_["states"][i].astype(BF16), v_["k_til"][rows]], axis=0)
            mixed.append(_dot_nt(q_stack, rhs))
        v_["mixed"] = mixed

    def g7_outputs():
        sr = lax.broadcasted_iota(jnp.int32, (GLA_HEADS * C, C), 0)
        sc = lax.broadcasted_iota(jnp.int32, (GLA_HEADS * C, C), 1)
        causal = (sr & (C - 1)) >= sc
        vg16 = v_["vg16"]
        og_rows = []
        for i in range(n_gla):
            rows = slice(i * C, (i + 1) * C)
            m_i = v_["mixed"][i]
            a = jnp.where(causal, m_i[:, GLA_DV:], 0.0).astype(BF16)
            og_rows.append(jnp.concatenate(
                [m_i[hh * C:(hh + 1) * C, :GLA_DV]
                 + _dot(a[hh * C:(hh + 1) * C], vg16[rows, hh * GLA_DV:(hh + 1) * GLA_DV])
                 for hh in range(GLA_HEADS)], axis=1))
        v_["o_g"] = jnp.concatenate(og_rows, axis=0)

    gate_parts = {"silu_a": [], "silu_g": [], "merge_a": [], "merge_g": []}

    def gate_chunk(key, w_ref, row, silu):
        def run():
            z = _dot_nt(h, w_ref[row:row + MXU_TILE, :])
            s = _sigmoid(z)
            gate_parts[key].append(z * s if silu else s)
        return run

    t = ([gate_chunk("silu_a", wza_ref, r, True) for r in range(0, ATT_WIDTH, MXU_TILE)]
         + [gate_chunk("silu_g", wt_ref, R_ZG + r, True) for r in range(0, GLA_WIDTH, MXU_TILE)]
         + [gate_chunk("merge_a", wt_ref, R_MA + r, False) for r in range(0, D_MODEL, MXU_TILE)]
         + [gate_chunk("merge_g", wt_ref, R_MG + r, False) for r in range(0, D_MODEL, MXU_TILE)])

    def t_tail():
        gates = tuple(jnp.concatenate(gate_parts[key], axis=1) for key in ("silu_a", "silu_g", "merge_a", "merge_g"))
        y_ref[...] = _layer_tail(x, gate, v_["o_a"], v_["o_g"], gates, glang_ref, wba_ref, wbg_ref, wout_ref)

    for stage in (a1_qkv, g1_decay, t[0], g2_qkv, t[1], a2_norm_rope, t[2], g3_cumsum, t[3], a3_scores, t[4],
                  g4_decayed, t[5], a4_softmax, t[6], t[7], g5_states, t[8], a5_values, t[9], g6_mixed, t[10],
                  g7_outputs, t[11], t_tail):
        stage()

    @pl.when(j == last)
    def _():
        wk_ref[...] = klast_scr[...].T
        wv_ref[...] = vlast_scr[...].T
        st_ref[...] = st_scr[...].T


def _const_spec(shape, nargs):
    zeros = (0,) * len(shape)
    if nargs == 1:
        return pl.BlockSpec(shape, lambda i: zeros)
    return pl.BlockSpec(shape, lambda b, j: zeros)


def _weight_specs(nargs):
    smem = pl.BlockSpec(memory_space=pltpu.SMEM)
    return [
        _const_spec((1, D_MODEL), nargs),
        _const_spec((ATT_WIDTH, D_MODEL), nargs),
        _const_spec((ATT_WIDTH, D_MODEL), nargs),
        _const_spec((R_END, D_MODEL), nargs),
        _const_spec((1, LANES), nargs),
        _const_spec((1, LANES), nargs),
        smem,
        _const_spec((LANES, GLA_KEY_WIDTH), nargs),
        _const_spec((1, GLA_KEY_WIDTH), nargs),
        _const_spec((1, GLA_DV), nargs),
        _const_spec((ATT_WIDTH, D_MODEL), nargs),
        _const_spec((GLA_WIDTH, D_MODEL), nargs),
        _const_spec((D_MODEL, D_MODEL), nargs),
    ]


def _prompt_layer(x, mod, cos, sin, weights):
    nblk = SEQ // T_PROMPT
    return pl.pallas_call(
        _prompt_body,
        grid=(BATCH, nblk),
        in_specs=[
            pl.BlockSpec((None, T_PROMPT, D_MODEL), lambda b, j: (b, j, 0)),
            pl.BlockSpec((None, 3, D_MODEL), lambda b, j: (b, 0, 0)),
            pl.BlockSpec((T_PROMPT, LANES), lambda b, j: (j, 0)),
            pl.BlockSpec((T_PROMPT, LANES), lambda b, j: (j, 0)),
        ] + _weight_specs(2),
        out_specs=[
            pl.BlockSpec((None, T_PROMPT, D_MODEL), lambda b, j: (b, j, 0)),
            pl.BlockSpec((None, KV_WIDTH, WINDOW), lambda b, j: (b, 0, 0)),
            pl.BlockSpec((None, KV_WIDTH, WINDOW), lambda b, j: (b, 0, 0)),
            pl.BlockSpec((None, GLA_KEY_WIDTH, GLA_DV), lambda b, j: (b, 0, 0)),
        ],
        out_shape=[
            jax.ShapeDtypeStruct((BATCH, SEQ, D_MODEL), F32),
            jax.ShapeDtypeStruct((BATCH, KV_WIDTH, WINDOW), F32),
            jax.ShapeDtypeStruct((BATCH, KV_WIDTH, WINDOW), F32),
            jax.ShapeDtypeStruct((BATCH, GLA_KEY_WIDTH, GLA_DV), F32),
        ],
        scratch_shapes=[
            pltpu.VMEM((ATT_BLOCK, KV_WIDTH), BF16),
            pltpu.VMEM((ATT_BLOCK, KV_WIDTH), BF16),
            pltpu.VMEM((GLA_DV, GLA_KEY_WIDTH), F32),
            pltpu.VMEM((WINDOW, KV_WIDTH), F32),
            pltpu.VMEM((WINDOW, KV_WIDTH), F32),
        ],
        compiler_params=pltpu.CompilerParams(dimension_semantics=("arbitrary", "arbitrary"),
                                             vmem_limit_bytes=VMEM_LIMIT),
        name="prompt_layer",
    )(x, mod, cos, sin, *weights)


def _sample_body(x_ref, mod_ref, cos_ref, sin_ref, ng_ref, wq_ref, wza_ref, wt_ref, qng_ref, kng_ref, sink_ref,
                 wgate_ref, bgate_ref, glang_ref, wba_ref, wbg_ref, wout_ref,
                 ck_ref, cv_ref, s0_ref,
                 y_ref, ok_ref, ov_ref, s1_ref,
                 qrows_scr, snew_scr, oc_scr, pnew_scr, rden_scr, qhat_scr, kdec_scr, vg_scr, la3_scr, oi_scr):
    G, L, T = SEQ_GROUP, DEC_SEQ, T_SAMPLE
    x = x_ref[...]

    def per_token(a, b):
        return jnp.concatenate([jnp.broadcast_to(mod_ref[n:n + 1, a:b], (L, b - a)) for n in range(G)], axis=0)

    shift = per_token(0, D_MODEL)
    scale = per_token(D_MODEL, 2 * D_MODEL)
    gate = per_token(2 * D_MODEL, 3 * D_MODEL)
    ms = jnp.mean(x * x, axis=-1, keepdims=True)
    h = (x * lax.rsqrt(ms + EPS) * ng_ref[...] * (1.0 + scale) + shift).astype(BF16)
    p = _Proj(h, wq_ref, wza_ref, wt_ref)

    cos, sin = cos_ref[...], sin_ref[...]
    lane = lax.broadcasted_iota(jnp.int32, (T, LANES), 1)
    lo = lane < HALF
    q = p.q_att()
    for i in range(ATT_GROUP):
        s = _norm_rope_slab(q[:, i * LANES:(i + 1) * LANES], qng_ref[...], cos, sin) * (ATT_HEAD_DIM ** -0.5)
        qrows_scr[:, i * L:(i + 1) * L, :] = jnp.where(lo, s, 0.0).reshape(G, L, LANES)
        qrows_scr[:, (i + ATT_GROUP) * L:(i + ATT_GROUP + 1) * L, :] = jnp.where(lo, 0.0, s).reshape(G, L, LANES)
    kv_new = p.rows(R_K, R_ZA)
    k_new = _norm_rope_slab(kv_new[:, :KV_WIDTH], kng_ref[...], cos, sin)
    v_new = kv_new[:, KV_WIDTH:]
    k_new_t, v_new_t = k_new.T, v_new.T
    nrow = ATT_HEADS * L
    snew_scr[...] = _dot(qrows_scr[...].reshape(G * nrow, LANES).astype(BF16),
                         k_new_t.astype(BF16)).reshape(G, nrow, T)

    qg = p.rows(R_QG, R_KG) * (GLA_DK ** -0.5)
    kg = p.rows(R_KG, R_VG)
    vg = p.rows(R_VG, R_LR)
    la = _log_decay(p.rows(R_LR, R_LR + LANES), wgate_ref, bgate_ref)
    tr = lax.broadcasted_iota(jnp.int32, (T, T), 0)
    tc = lax.broadcasted_iota(jnp.int32, (T, T), 1)
    same_seq = (tr // L) == (tc // L)
    causal = same_seq & (tc <= tr)
    hi, mid, lo3 = _split3(la)
    tri = causal.astype(BF16)
    blk = same_seq.astype(BF16)
    b = _dot(tri, hi) + _dot(tri, mid) + _dot(tri, lo3)
    b_last = _dot(blk, hi) + _dot(blk, mid) + _dot(blk, lo3)
    q_hat = qg * jnp.exp(b)
    k_til = (kg * jnp.exp(-b)).astype(BF16)
    k_dec = kg * jnp.exp(b_last - b)
    hmasks = _head_masks(T)
    zeros_l = jnp.zeros((G, L, GLA_KEY_WIDTH), F32)
    for hh in range(GLA_HEADS):
        qhat_scr[:, hh * L:(hh + 1) * L, :] = jnp.where(hmasks[hh], q_hat, 0.0).reshape(G, L, GLA_KEY_WIDTH)
    kdec_scr[:, 0:L, :] = k_dec.reshape(G, L, GLA_KEY_WIDTH)
    kdec_scr[:, L:2 * L, :] = zeros_l
    vg_scr[:, 0:L, :] = vg.reshape(G, L, GLA_WIDTH)
    vg_scr[:, L:2 * L, :] = jnp.zeros((G, L, GLA_WIDTH), F32)
    la3_scr[:, 0:L, :] = hi.astype(F32).reshape(G, L, GLA_KEY_WIDTH)
    la3_scr[:, L:2 * L, :] = mid.astype(F32).reshape(G, L, GLA_KEY_WIDTH)
    la3_scr[:, 2 * L:3 * L, :] = lo3.astype(F32).reshape(G, L, GLA_KEY_WIDTH)
    la3_scr[:, 3 * L:4 * L, :] = zeros_l

    rr = lax.broadcasted_iota(jnp.int32, (nrow, LANES), 0) & (L - 1)
    cc = lax.broadcasted_iota(jnp.int32, (nrow, LANES), 1)
    cache_mask = cc >= rr
    sink_col = jnp.concatenate([jnp.full((L, 1), sink_ref[hh], F32) for hh in range(ATT_HEADS)], axis=0)
    keep_old = lax.broadcasted_iota(jnp.int32, (KV_WIDTH, WINDOW), 1) < WINDOW - L
    ones16 = jnp.ones((4 * L, GLA_DV), BF16)

    def seq_body(it, carry):
        ns = [it * SEQ_UNROLL + u for u in range(SEQ_UNROLL)]
        kc = [ck_ref[n] for n in ns]
        vc = [cv_ref[n] for n in ns]
        s_c = [jnp.where(cache_mask, _dot(qrows_scr[n].astype(BF16), k.astype(BF16)), NEG)
               for n, k in zip(ns, kc)]
        st0 = [s0_ref[n] for n in ns]
        for n, s0 in zip(ns, st0):
            oi_scr[n] = _dot(qhat_scr[n].astype(BF16), s0.astype(BF16))
        kv = [_dot_tn(kdec_scr[n].astype(BF16), vg_scr[n].astype(BF16)) for n in ns]
        b_col = [_dot_tn(la3_scr[n].astype(BF16), ones16) for n in ns]
        s_n = [jnp.where((cc >= n * L) & (cc <= n * L + rr), snew_scr[n], NEG) for n in ns]
        m = [jnp.maximum(jnp.maximum(jnp.max(a, axis=-1, keepdims=True), jnp.max(b_, axis=-1, keepdims=True)),
                         sink_col) for a, b_ in zip(s_c, s_n)]
        e_c = [jnp.exp(a - mm) for a, mm in zip(s_c, m)]
        e_n = [jnp.exp(a - mm) for a, mm in zip(s_n, m)]
        for n, k, v in zip(ns, kc, vc):
            new_shift = WINDOW - L - n * L
            ok_ref[n] = jnp.where(keep_old, pltpu.roll(k, WINDOW - L, 1), pltpu.roll(k_new_t, new_shift, 1))
            ov_ref[n] = jnp.where(keep_old, pltpu.roll(v, WINDOW - L, 1), pltpu.roll(v_new_t, new_shift, 1))
        for n, ec, en, mm, v in zip(ns, e_c, e_n, m, vc):
            denom = (jnp.sum(ec, axis=-1, keepdims=True) + jnp.sum(en, axis=-1, keepdims=True)
                     + jnp.exp(sink_col - mm))
            oc_scr[n] = _dot_nt(ec.astype(BF16), v.astype(BF16))
            pnew_scr[n] = en
            rden_scr[n] = jnp.broadcast_to(1.0 / denom, (nrow, LANES))
        for n, s0, kv_n, bc in zip(ns, st0, kv, b_col):
            upd = jnp.concatenate([kv_n[hh * GLA_DK:(hh + 1) * GLA_DK, hh * GLA_DV:(hh + 1) * GLA_DV]
                                   for hh in range(GLA_HEADS)], axis=0)
            s1_ref[n] = jnp.exp(bc) * s0 + upd
        return carry

    lax.fori_loop(0, G // SEQ_UNROLL, seq_body, 0)

    o_new = _dot(pnew_scr[...].reshape(G * nrow, T).astype(BF16), v_new.astype(BF16))
    o_att = ((oc_scr[...].reshape(G * nrow, KV_WIDTH) + o_new)
             * rden_scr[...].reshape(G * nrow, LANES)).reshape(G, nrow, KV_WIDTH)
    o_a = jnp.concatenate(
        [jnp.where(lo, o_att[:, i * L:(i + 1) * L, :].reshape(T, LANES),
                   o_att[:, (i + ATT_GROUP) * L:(i + ATT_GROUP + 1) * L, :].reshape(T, LANES))
         for i in range(ATT_GROUP)], axis=1)
    oi = oi_scr[...]
    o_inter = jnp.concatenate([oi[:, hh * L:(hh + 1) * L, :].reshape(T, GLA_DV) for hh in range(GLA_HEADS)],
                              axis=1)
    intra = []
    for hh in range(GLA_HEADS):
        a = _dot_nt(jnp.where(hmasks[hh], q_hat, 0.0).astype(BF16), k_til)
        a = jnp.where(causal, a, 0.0).astype(BF16)
        intra.append(_dot(a, vg[:, hh * GLA_DV:(hh + 1) * GLA_DV].astype(BF16)))
    o_g = o_inter + jnp.concatenate(intra, axis=1)

    y_ref[...] = _layer_tail(x, gate, o_a, o_g, _gates(p), glang_ref, wba_ref, wbg_ref, wout_ref)


def _sample_layer(x, mod, cos, sin, weights, ck, cv, s0):
    G, L, T = SEQ_GROUP, DEC_SEQ, T_SAMPLE
    nrow = ATT_HEADS * L
    row_spec = lambda w: pl.BlockSpec((T, w), lambda i: (i, 0))
    seq_spec = lambda a, b: pl.BlockSpec((G, a, b), lambda i: (i, 0, 0))
    return pl.pallas_call(
        _sample_body,
        grid=(DEC_BATCH // G,),
        in_specs=[row_spec(D_MODEL), pl.BlockSpec((G, 3 * D_MODEL), lambda i: (i, 0)),
                  _const_spec((T, LANES), 1), _const_spec((T, LANES), 1)]
                 + _weight_specs(1)
                 + [seq_spec(KV_WIDTH, WINDOW), seq_spec(KV_WIDTH, WINDOW), seq_spec(GLA_KEY_WIDTH, GLA_DV)],
        out_specs=[row_spec(D_MODEL), seq_spec(KV_WIDTH, WINDOW), seq_spec(KV_WIDTH, WINDOW),
                   seq_spec(GLA_KEY_WIDTH, GLA_DV)],
        out_shape=[
            jax.ShapeDtypeStruct((DEC_BATCH * L, D_MODEL), F32),
            jax.ShapeDtypeStruct((DEC_BATCH, KV_WIDTH, WINDOW), F32),
            jax.ShapeDtypeStruct((DEC_BATCH, KV_WIDTH, WINDOW), F32),
            jax.ShapeDtypeStruct((DEC_BATCH, GLA_KEY_WIDTH, GLA_DV), F32),
        ],
        scratch_shapes=[
            pltpu.VMEM((G, nrow, LANES), F32),
            pltpu.VMEM((G, nrow, T), F32),
            pltpu.VMEM((G, nrow, KV_WIDTH), F32),
            pltpu.VMEM((G, nrow, T), F32),
            pltpu.VMEM((G, nrow, LANES), F32),
            pltpu.VMEM((G, GLA_HEADS * L, GLA_KEY_WIDTH), F32),
            pltpu.VMEM((G, 2 * L, GLA_KEY_WIDTH), F32),
            pltpu.VMEM((G, 2 * L, GLA_WIDTH), F32),
            pltpu.VMEM((G, 4 * L, GLA_KEY_WIDTH), F32),
            pltpu.VMEM((G, GLA_HEADS * L, GLA_DV), F32),
        ],
        compiler_params=pltpu.CompilerParams(dimension_semantics=("arbitrary",),
                                             vmem_limit_bytes=VMEM_LIMIT),
        name="sample_layer",
    )(x, mod, cos, sin, *weights, ck, cv, s0)


def _rope_tables(pos):
    half = ATT_HEAD_DIM // 2
    inv = 1.0 / (ROPE_THETA ** (np.arange(half, dtype=np.float64) / half))
    ang = pos.astype(np.float64)[:, None] * inv[None, :]
    c, s = np.cos(ang), np.sin(ang)
    return (np.tile(c, (1, 4)).astype(np.float32),
            np.concatenate([-s, s, -s, s], axis=1).astype(np.float32))


def _pair_heads(w):
    shape = w.shape
    w = w.reshape((ATT_KV_HEADS, ATT_GROUP, ATT_HEAD_DIM) + shape[1:])
    return jnp.swapaxes(w, 0, 1).reshape(shape)


def kernel(x_prompt, x_sample, cache_win_k, cache_win_v, state_gla, c_prompt, c_sample, norm_g, w_ada, b_ada, w_in, q_norm_g, k_norm_g, attn_sinks, w_gla_gate, b_gla_gate, gla_norm_g, w_branch_att, w_branch_gla, w_out):
    assert w_in.shape == (1, D_MODEL, R_END), "single-layer trunk"
    wt = jnp.swapaxes(w_in[0], 0, 1)
    wgate = jnp.concatenate([w_gla_gate[0], jnp.zeros((LANES - GLA_GATE_RANK, GLA_KEY_WIDTH), F32)],
                            axis=0).astype(BF16)
    weights = (
        norm_g[0][None, :],
        _pair_heads(wt[0:R_K]).astype(BF16), _pair_heads(wt[R_ZA:R_QG]).astype(BF16), wt.astype(BF16),
        jnp.tile(q_norm_g[0], 2)[None, :], jnp.tile(k_norm_g[0], 2)[None, :],
        attn_sinks[0], wgate, b_gla_gate[0][None, :], gla_norm_g[0][None, :],
        _pair_heads(w_branch_att[0]).astype(BF16), w_branch_gla[0].astype(BF16), w_out[0].astype(BF16),
    )

    c_all = jnp.concatenate([c_sample, c_prompt, jnp.zeros((MOD_ROWS - DEC_BATCH - BATCH, D_MODEL), F32)], axis=0)
    mod = _modulation(c_all, w_ada[0], b_ada[0][None, :])
    mod_prompt = mod[DEC_BATCH:DEC_BATCH + BATCH].reshape(BATCH, 3, D_MODEL)

    cos_p, sin_p = _rope_tables(np.arange(SEQ))
    cos_s, sin_s = _rope_tables(PAST_LEN + np.arange(DEC_SEQ))
    cos_s, sin_s = np.tile(cos_s, (SEQ_GROUP, 1)), np.tile(sin_s, (SEQ_GROUP, 1))

    def kv_rows(c):
        return jnp.transpose(c, (0, 2, 3, 1)).reshape(c.shape[0], KV_WIDTH, WINDOW)

    def kv_out(c):
        return jnp.transpose(c.reshape(c.shape[0], ATT_KV_HEADS, ATT_HEAD_DIM, WINDOW), (0, 3, 1, 2))[None]

    y_p, wk_p, wv_p, st_p = _prompt_layer(x_prompt, mod_prompt, cos_p, sin_p, weights)
    y_s, wk_s, wv_s, st_s = _sample_layer(
        x_sample.reshape(DEC_BATCH * DEC_SEQ, D_MODEL), mod, cos_s, sin_s, weights,
        kv_rows(cache_win_k[0]), kv_rows(cache_win_v[0]),
        state_gla[0].reshape(DEC_BATCH, GLA_KEY_WIDTH, GLA_DV))

    st_shape = (1, -1, GLA_HEADS, GLA_DK, GLA_DV)
    return (y_p, y_s.reshape(DEC_BATCH, DEC_SEQ, D_MODEL),
            kv_out(wk_p), kv_out(wv_p), st_p.reshape(st_shape),
            kv_out(wk_s), kv_out(wv_s), st_s.reshape(st_shape))
```

```python
import numpy as np

import jax
import jax.numpy as jnp
from jax import lax
from jax.experimental import pallas as pl
from jax.experimental.pallas import tpu as pltpu

F32 = jnp.float32
BF16 = jnp.bfloat16

D_MODEL = 1024
BATCH = 2
SEQ = 8192
DEC_BATCH = 128
DEC_SEQ = 8
PAST_LEN = 16384
ATT_HEADS = 8
ATT_KV_HEADS = 2
ATT_GROUP = ATT_HEADS // ATT_KV_HEADS
ATT_HEAD_DIM = 64
ATT_WIDTH = ATT_HEADS * ATT_HEAD_DIM
KV_WIDTH = ATT_KV_HEADS * ATT_HEAD_DIM
WINDOW = 128
ATT_BLOCK = 128
ROPE_THETA = 10000.0
GLA_HEADS = 4
GLA_WIDTH = D_MODEL // 2
GLA_DV = GLA_WIDTH // GLA_HEADS
GLA_KEY_WIDTH = GLA_WIDTH // 2
GLA_DK = GLA_KEY_WIDTH // GLA_HEADS
GLA_GATE_RANK = 16
GLA_GATE_TAU = 16.0
EPS = 1e-6
NEG = -1e30
LOG2E = 1.4426950408889634
ATT_SCALE = ATT_HEAD_DIM ** -0.5 * LOG2E

LANES = 128
HALF = LANES // 2
MXU_TILE = 256
R_K, R_V, R_ZA = ATT_WIDTH, ATT_WIDTH + KV_WIDTH, ATT_WIDTH + 2 * KV_WIDTH
R_QG = R_ZA + ATT_WIDTH
R_KG = R_QG + GLA_KEY_WIDTH
R_VG = R_KG + GLA_KEY_WIDTH
R_LR = R_VG + GLA_WIDTH
R_ZG = R_LR + GLA_GATE_RANK
R_MA = R_ZG + GLA_WIDTH
R_MG = R_MA + D_MODEL
R_END = R_MG + D_MODEL

T_PROMPT = 512
GLA_BLOCK = 64
SEQ_GROUP = 16
T_SAMPLE = SEQ_GROUP * DEC_SEQ
SEQ_UNROLL = 4
MOD_ROWS = DEC_BATCH + 16
MOD_TILE = 512
VMEM_LIMIT = 56 * 1024 * 1024


def _sigmoid(x):
    return 1.0 / (1.0 + jnp.exp2(x * -LOG2E))


def _dot(a, b):
    return jnp.dot(a, b, preferred_element_type=F32)


def _dot_nt(a, b):
    return lax.dot_general(a, b, (((1,), (1,)), ((), ())), preferred_element_type=F32)


def _dot_tn(a, b):
    return lax.dot_general(a, b, (((0,), (0,)), ((), ())), preferred_element_type=F32)


def _split3(x):
    hi = x.astype(BF16)
    r1 = x - hi.astype(F32)
    mid = r1.astype(BF16)
    lo = (r1 - mid.astype(F32)).astype(BF16)
    return hi, mid, lo


def _norm_rope_slab(xs, g, cos, sin):
    lane = lax.broadcasted_iota(jnp.int32, xs.shape, 1)
    lo = lane < HALF
    first = (lane & (HALF // 2)) == 0
    sq = xs * xs
    s_lo = jnp.sum(jnp.where(lo, sq, 0.0), axis=-1, keepdims=True)
    s_hi = jnp.sum(jnp.where(lo, 0.0, sq), axis=-1, keepdims=True)
    inv = jnp.where(lo, lax.rsqrt(s_lo * (1.0 / ATT_HEAD_DIM) + EPS),
                    lax.rsqrt(s_hi * (1.0 / ATT_HEAD_DIM) + EPS))
    xn = xs * inv * g
    swapped = jnp.where(first, pltpu.roll(xn, LANES - HALF // 2, 1), pltpu.roll(xn, HALF // 2, 1))
    return xn * cos + swapped * sin


def _log_decay(lrz, wgate_ref, bgate_ref):
    pre = _dot(lrz.astype(BF16), wgate_ref[...]) + bgate_ref[...]
    log_sig = jnp.minimum(pre, 0.0) - jnp.log(1.0 + jnp.exp(-jnp.abs(pre)))
    return log_sig * (1.0 / GLA_GATE_TAU)


def _head_masks(rows):
    lane = lax.broadcasted_iota(jnp.int32, (rows, GLA_KEY_WIDTH), 1)
    return [(lane >= h * GLA_DK) & (lane < (h + 1) * GLA_DK) for h in range(GLA_HEADS)]


class _Proj:
    def __init__(self, h, wq_ref, wza_ref, wt_ref):
        self.h, self.wq_ref, self.wza_ref, self.wt_ref = h, wq_ref, wza_ref, wt_ref

    def rows(self, a, b):
        return _dot_nt(self.h, self.wt_ref[a:b, :])

    def q_att(self):
        return _dot_nt(self.h, self.wq_ref[...])

    def z_att(self):
        return _dot_nt(self.h, self.wza_ref[...])


def _gates(p):
    z_a = p.z_att()
    z_g = p.rows(R_ZG, R_MA)
    return (z_a * _sigmoid(z_a), z_g * _sigmoid(z_g),
            _sigmoid(p.rows(R_MA, R_MG)), _sigmoid(p.rows(R_MG, R_END)))


def _layer_tail(x, gate, o_a, o_g, gates, glang_ref, wba_ref, wbg_ref, wout_ref):
    silu_a, silu_g, merge_a, merge_g = gates
    o_a = o_a * silu_a
    slabs = []
    for h in range(GLA_HEADS):
        oh = o_g[:, h * GLA_DV:(h + 1) * GLA_DV]
        ms = jnp.mean(oh * oh, axis=-1, keepdims=True)
        slabs.append(oh * lax.rsqrt(ms + EPS) * glang_ref[...])
    o_g = jnp.concatenate(slabs, axis=1) * silu_g
    merged = (merge_a * _dot(o_a.astype(BF16), wba_ref[...])
              + merge_g * _dot(o_g.astype(BF16), wbg_ref[...]))
    return x + gate * _dot(merged.astype(BF16), wout_ref[...])


def _mod_body(c_ref, w_ref, b_ref, o_ref, act_ref):
    @pl.when(pl.program_id(0) == 0)
    def _():
        c = c_ref[...]
        act_ref[...] = (c * _sigmoid(c)).astype(BF16)

    o_ref[...] = _dot(act_ref[...], w_ref[...].astype(BF16)) + b_ref[...]


def _modulation(c_all, w_ada, b_ada):
    rows = c_all.shape[0]
    return pl.pallas_call(
        _mod_body,
        grid=(3 * D_MODEL // MOD_TILE,),
        in_specs=[
            pl.BlockSpec((rows, D_MODEL), lambda i: (0, 0)),
            pl.BlockSpec((D_MODEL, MOD_TILE), lambda i: (0, i)),
            pl.BlockSpec((1, MOD_TILE), lambda i: (0, i)),
        ],
        out_specs=pl.BlockSpec((rows, MOD_TILE), lambda i: (0, i)),
        out_shape=jax.ShapeDtypeStruct((rows, 3 * D_MODEL), F32),
        scratch_shapes=[pltpu.VMEM((rows, D_MODEL), BF16)],
        compiler_params=pltpu.CompilerParams(dimension_semantics=("arbitrary",),
                                             vmem_limit_bytes=VMEM_LIMIT),
        name="adaln_mod",
    )(c_all, w_ada, b_ada)


def _softmax_block(s, mask, sink_ref):
    probs, rden = [], []
    for h in range(ATT_HEADS):
        sh = jnp.where(mask, s[h * ATT_BLOCK:(h + 1) * ATT_BLOCK], NEG)
        sink = sink_ref[h] * LOG2E
        m = jnp.maximum(jnp.max(sh, axis=-1, keepdims=True), sink)
        e = jnp.exp2(sh - m)
        rden.append(1.0 / (jnp.sum(e, axis=-1, keepdims=True) + jnp.exp2(sink - m)))
        probs.append(e.astype(BF16))
    return jnp.concatenate(probs, axis=0), rden


def _prompt_body(x_ref, mod_ref, cos_ref, sin_ref, ng_ref, wq_ref, wza_ref, wt_ref, qng_ref, kng_ref, sink_ref,
                 wgate_ref, bgate_ref, glang_ref, wba_ref, wbg_ref, wout_ref,
                 y_ref, wk_ref, wv_ref, st_ref,
                 kprev_ref, vprev_ref, st_scr, klast_scr, vlast_scr):
    T, C = T_PROMPT, GLA_BLOCK
    n_att, n_gla = T // ATT_BLOCK, T // C
    j = pl.program_id(1)
    last = pl.num_programs(1) - 1

    @pl.when(j == 0)
    def _():
        kprev_ref[...] = jnp.zeros_like(kprev_ref)
        vprev_ref[...] = jnp.zeros_like(vprev_ref)
        st_scr[...] = jnp.zeros_like(st_scr)

    x = x_ref[...]
    shift, scale, gate = mod_ref[0:1, :], mod_ref[1:2, :], mod_ref[2:3, :]
    ms = jnp.mean(x * x, axis=-1, keepdims=True)
    h = (x * lax.rsqrt(ms + EPS) * ng_ref[...] * (1.0 + scale) + shift).astype(BF16)
    p = _Proj(h, wq_ref, wza_ref, wt_ref)
    cos, sin = cos_ref[...], sin_ref[...]
    lane = lax.broadcasted_iota(jnp.int32, (ATT_BLOCK, LANES), 1)
    lo = lane < HALF
    v_ = {}

    def a1_qkv():
        kv = p.rows(R_K, R_ZA)
        v_["q"], v_["k_raw"], v_["v"] = p.q_att(), kv[:, :KV_WIDTH], kv[:, KV_WIDTH:]

    def a2_norm_rope():
        q = v_["q"]
        v_["q_slabs"] = [_norm_rope_slab(q[:, i * LANES:(i + 1) * LANES], qng_ref[...], cos, sin)
                         * ATT_SCALE for i in range(ATT_GROUP)]
        k = _norm_rope_slab(v_["k_raw"], kng_ref[...], cos, sin)
        v = v_["v"]
        klast_scr[...] = k[T - WINDOW:]
        vlast_scr[...] = v[T - WINDOW:]
        v_["k16"], v_["v16"] = k.astype(BF16), v.astype(BF16)

    def a3_scores():
        k16, v16, q_slabs = v_["k16"], v_["v16"], v_["q_slabs"]
        r = lax.broadcasted_iota(jnp.int32, (ATT_BLOCK, 2 * ATT_BLOCK), 0)
        c = lax.broadcasted_iota(jnp.int32, (ATT_BLOCK, 2 * ATT_BLOCK), 1)
        band = ((c < ATT_BLOCK) & (c >= r)) | ((c >= ATT_BLOCK) & (c - ATT_BLOCK <= r))
        v_["masks"] = [band & ((c >= ATT_BLOCK) | (j > 0))] + [band] * (n_att - 1)
        scores, vcats = [], []
        for i in range(n_att):
            rows = slice(i * ATT_BLOCK, (i + 1) * ATT_BLOCK)
            if i == 0:
                kp, vp = kprev_ref[...], vprev_ref[...]
            else:
                prev = slice((i - 1) * ATT_BLOCK, i * ATT_BLOCK)
                kp, vp = k16[prev], v16[prev]
            pieces = ([jnp.where(lo, s[rows], 0.0).astype(BF16) for s in q_slabs]
                      + [jnp.where(lo, 0.0, s[rows]).astype(BF16) for s in q_slabs])
            scores.append(_dot_nt(jnp.concatenate(pieces, axis=0), jnp.concatenate([kp, k16[rows]], axis=0)))
            vcats.append(jnp.concatenate([vp, v16[rows]], axis=0))
        kprev_ref[...] = k16[T - ATT_BLOCK:]
        vprev_ref[...] = v16[T - ATT_BLOCK:]
        v_["scores"], v_["vcats"] = scores, vcats

    def a4_softmax():
        v_["soft"] = [_softmax_block(v_["scores"][i], v_["masks"][i], sink_ref) for i in range(n_att)]

    def a5_values():
        o_rows = []
        for i in range(n_att):
            probs, rden = v_["soft"][i]
            o = _dot(probs, v_["vcats"][i])
            o_rows.append(jnp.concatenate(
                [jnp.where(lo, o[g * ATT_BLOCK:(g + 1) * ATT_BLOCK] * rden[g],
                           o[(g + ATT_GROUP) * ATT_BLOCK:(g + ATT_GROUP + 1) * ATT_BLOCK] * rden[g + ATT_GROUP])
                 for g in range(ATT_GROUP)], axis=1))
        v_["o_a"] = jnp.concatenate(o_rows, axis=0)

    def g1_decay():
        v_["la"] = _log_decay(p.rows(R_LR, R_LR + LANES), wgate_ref, bgate_ref)

    def g2_qkv():
        v_["qg"] = p.rows(R_QG, R_KG) * (GLA_DK ** -0.5)
        v_["kg"] = p.rows(R_KG, R_VG)
        v_["vg16"] = p.rows(R_VG, R_LR).astype(BF16)

    def g3_cumsum():
        tr = lax.broadcasted_iota(jnp.int32, (T, T), 0)
        tc = lax.broadcasted_iota(jnp.int32, (T, T), 1)
        tri = (((tr // C) == (tc // C)) & (tc <= tr)).astype(BF16)
        hi, mid, lo3 = _split3(v_["la"])
        b = _dot(tri, hi) + _dot(tri, mid) + _dot(tri, lo3)
        v_["b"] = b
        v_["b_last"] = [b[(i + 1) * C - 1:(i + 1) * C] for i in range(n_gla)]
        v_["b_mid"] = [b[i * C + C // 2 - 1:i * C + C // 2] for i in range(n_gla)]

    def g4_decayed():
        b, qg, kg = v_["b"], v_["qg"], v_["kg"]
        b_end = jnp.concatenate([jnp.broadcast_to(bl, (C, GLA_KEY_WIDTH)) for bl in v_["b_last"]], axis=0)
        b_ref = jnp.concatenate([jnp.broadcast_to(bm, (C, GLA_KEY_WIDTH)) for bm in v_["b_mid"]], axis=0)
        q_hat = qg * jnp.exp(b - b_ref)
        v_["k_til"] = (kg * jnp.exp(b_ref - b)).astype(BF16)
        k_dec = kg * jnp.exp(b_end - b)
        hmasks = _head_masks(T)
        v_["q_m"] = [jnp.where(hm, q_hat, 0.0).astype(BF16) for hm in hmasks]
        v_["k_m"] = [jnp.where(hm, k_dec, 0.0).astype(BF16) for hm in hmasks]

    def g5_states():
        vg16 = v_["vg16"]
        states = [st_scr[...]]
        for i in range(n_gla):
            rows = slice(i * C, (i + 1) * C)
            v_stack = jnp.concatenate([vg16[rows, hh * GLA_DV:(hh + 1) * GLA_DV] for hh in range(GLA_HEADS)],
                                      axis=0)
            k_stack = jnp.concatenate([km[rows] for km in v_["k_m"]], axis=0)
            states.append(states[-1] * jnp.exp(v_["b_last"][i]) + _dot_tn(v_stack, k_stack))
        st_scr[...] = states[-1]
        v_["states"] = states

    def g6_mixed():
        mixed = []
        for i in range(n_gla):
            rows = slice(i * C, (i + 1) * C)
            q_stack = jnp.concatenate([qm[rows] for qm in v_["q_m"]], axis=0)
            s_ref = v_["states"][i] * jnp.exp(v_["b_mid"][i])
            rhs = jnp.concatenate([s_ref.astype(BF16), v_["k_til"][rows]], axis=0)
            mixed.append(_dot_nt(q_stack, rhs))
        v_["mixed"] = mixed

    def g7_outputs():
        sr = lax.broadcasted_iota(jnp.int32, (GLA_HEADS * C, C), 0)
        sc = lax.broadcasted_iota(jnp.int32, (GLA_HEADS * C, C), 1)
        causal = (sr & (C - 1)) >= sc
        vg16 = v_["vg16"]
        og_rows = []
        for i in range(n_gla):
            rows = slice(i * C, (i + 1) * C)
            m_i = v_["mixed"][i]
            a = jnp.where(causal, m_i[:, GLA_DV:], 0.0).astype(BF16)
            og_rows.append(jnp.concatenate(
                [m_i[hh * C:(hh + 1) * C, :GLA_DV]
                 + _dot(a[hh * C:(hh + 1) * C], vg16[rows, hh * GLA_DV:(hh + 1) * GLA_DV])
                 for hh in range(GLA_HEADS)], axis=1))
        v_["o_g"] = jnp.concatenate(og_rows, axis=0)

    gate_parts = {"silu_a": [], "silu_g": [], "merge_a": [], "merge_g": []}

    def gate_chunk(key, w_ref, row, silu):
        def run():
            z = _dot_nt(h, w_ref[row:row + MXU_TILE, :])
            s = _sigmoid(z)
            gate_parts[key].append(z * s if silu else s)
        return run

    t = ([gate_chunk("silu_a", wza_ref, r, True) for r in range(0, ATT_WIDTH, MXU_TILE)]
         + [gate_chunk("silu_g", wt_ref, R_ZG + r, True) for r in range(0, GLA_WIDTH, MXU_TILE)]
         + [gate_chunk("merge_a", wt_ref, R_MA + r, False) for r in range(0, D_MODEL, MXU_TILE)]
         + [gate_chunk("merge_g", wt_ref, R_MG + r, False) for r in range(0, D_MODEL, MXU_TILE)])

    def t_tail():
        gates = tuple(jnp.concatenate(gate_parts[key], axis=1) for key in ("silu_a", "silu_g", "merge_a", "merge_g"))
        y_ref[...] = _layer_tail(x, gate, v_["o_a"], v_["o_g"], gates, glang_ref, wba_ref, wbg_ref, wout_ref)

    for stage in (a1_qkv, g1_decay, t[0], g2_qkv, t[1], a2_norm_rope, t[2], g3_cumsum, t[3], a3_scores, t[4],
                  g4_decayed, t[5], a4_softmax, t[6], t[7], g5_states, t[8], a5_values, t[9], g6_mixed, t[10],
                  g7_outputs, t[11], t_tail):
        stage()

    @pl.when(j == last)
    def _():
        wk_ref[...] = klast_scr[...].T
        wv_ref[...] = vlast_scr[...].T
        st_ref[...] = st_scr[...].T


def _const_spec(shape, nargs):
    zeros = (0,) * len(shape)
    if nargs == 1:
        return pl.BlockSpec(shape, lambda i: zeros)
    return pl.BlockSpec(shape, lambda b, j: zeros)


def _weight_specs(nargs):
    smem = pl.BlockSpec(memory_space=pltpu.SMEM)
    return [
        _const_spec((1, D_MODEL), nargs),
        _const_spec((ATT_WIDTH, D_MODEL), nargs),
        _const_spec((ATT_WIDTH, D_MODEL), nargs),
        _const_spec((R_END, D_MODEL), nargs),
        _const_spec((1, LANES), nargs),
        _const_spec((1, LANES), nargs),
        smem,
        _const_spec((LANES, GLA_KEY_WIDTH), nargs),
        _const_spec((1, GLA_KEY_WIDTH), nargs),
        _const_spec((1, GLA_DV), nargs),
        _const_spec((ATT_WIDTH, D_MODEL), nargs),
        _const_spec((GLA_WIDTH, D_MODEL), nargs),
        _const_spec((D_MODEL, D_MODEL), nargs),
    ]


def _prompt_layer(x, mod, cos, sin, weights):
    nblk = SEQ // T_PROMPT
    return pl.pallas_call(
        _prompt_body,
        grid=(BATCH, nblk),
        in_specs=[
            pl.BlockSpec((None, T_PROMPT, D_MODEL), lambda b, j: (b, j, 0)),
            pl.BlockSpec((None, 3, D_MODEL), lambda b, j: (b, 0, 0)),
            pl.BlockSpec((T_PROMPT, LANES), lambda b, j: (j, 0)),
            pl.BlockSpec((T_PROMPT, LANES), lambda b, j: (j, 0)),
        ] + _weight_specs(2),
        out_specs=[
            pl.BlockSpec((None, T_PROMPT, D_MODEL), lambda b, j: (b, j, 0)),
            pl.BlockSpec((None, KV_WIDTH, WINDOW), lambda b, j: (b, 0, 0)),
            pl.BlockSpec((None, KV_WIDTH, WINDOW), lambda b, j: (b, 0, 0)),
            pl.BlockSpec((None, GLA_KEY_WIDTH, GLA_DV), lambda b, j: (b, 0, 0)),
        ],
        out_shape=[
            jax.ShapeDtypeStruct((BATCH, SEQ, D_MODEL), F32),
            jax.ShapeDtypeStruct((BATCH, KV_WIDTH, WINDOW), F32),
            jax.ShapeDtypeStruct((BATCH, KV_WIDTH, WINDOW), F32),
            jax.ShapeDtypeStruct((BATCH, GLA_KEY_WIDTH, GLA_DV), F32),
        ],
        scratch_shapes=[
            pltpu.VMEM((ATT_BLOCK, KV_WIDTH), BF16),
            pltpu.VMEM((ATT_BLOCK, KV_WIDTH), BF16),
            pltpu.VMEM((GLA_DV, GLA_KEY_WIDTH), F32),
            pltpu.VMEM((WINDOW, KV_WIDTH), F32),
            pltpu.VMEM((WINDOW, KV_WIDTH), F32),
        ],
        compiler_params=pltpu.CompilerParams(dimension_semantics=("arbitrary", "arbitrary"),
                                             vmem_limit_bytes=VMEM_LIMIT),
        name="prompt_layer",
    )(x, mod, cos, sin, *weights)


def _sample_body(x_ref, mod_ref, cos_ref, sin_ref, ng_ref, wq_ref, wza_ref, wt_ref, qng_ref, kng_ref, sink_ref,
                 wgate_ref, bgate_ref, glang_ref, wba_ref, wbg_ref, wout_ref,
                 ck_ref, cv_ref, s0_ref,
                 y_ref, ok_ref, ov_ref, s1_ref,
                 qrows_scr, snew_scr, oc_scr, pnew_scr, rden_scr, qhat_scr, kdec_scr, vg_scr, la3_scr, oi_scr):
    G, L, T = SEQ_GROUP, DEC_SEQ, T_SAMPLE
    x = x_ref[...]

    def per_token(a, b):
        return jnp.concatenate([jnp.broadcast_to(mod_ref[n:n + 1, a:b], (L, b - a)) for n in range(G)], axis=0)

    shift = per_token(0, D_MODEL)
    scale = per_token(D_MODEL, 2 * D_MODEL)
    gate = per_token(2 * D_MODEL, 3 * D_MODEL)
    ms = jnp.mean(x * x, axis=-1, keepdims=True)
    h = (x * lax.rsqrt(ms + EPS) * ng_ref[...] * (1.0 + scale) + shift).astype(BF16)
    p = _Proj(h, wq_ref, wza_ref, wt_ref)

    cos, sin = cos_ref[...], sin_ref[...]
    lane = lax.broadcasted_iota(jnp.int32, (T, LANES), 1)
    lo = lane < HALF
    q = p.q_att()
    for i in range(ATT_GROUP):
        s = _norm_rope_slab(q[:, i * LANES:(i + 1) * LANES], qng_ref[...], cos, sin) * ATT_SCALE
        qrows_scr[:, i * L:(i + 1) * L, :] = jnp.where(lo, s, 0.0).reshape(G, L, LANES)
        qrows_scr[:, (i + ATT_GROUP) * L:(i + ATT_GROUP + 1) * L, :] = jnp.where(lo, 0.0, s).reshape(G, L, LANES)
    kv_new = p.rows(R_K, R_ZA)
    k_new = _norm_rope_slab(kv_new[:, :KV_WIDTH], kng_ref[...], cos, sin)
    v_new = kv_new[:, KV_WIDTH:]
    k_new_t, v_new_t = k_new.T, v_new.T
    nrow = ATT_HEADS * L
    snew_scr[...] = _dot(qrows_scr[...].reshape(G * nrow, LANES).astype(BF16),
                         k_new_t.astype(BF16)).reshape(G, nrow, T)

    qg = p.rows(R_QG, R_KG) * (GLA_DK ** -0.5)
    kg = p.rows(R_KG, R_VG)
    vg = p.rows(R_VG, R_LR)
    la = _log_decay(p.rows(R_LR, R_LR + LANES), wgate_ref, bgate_ref)
    tr = lax.broadcasted_iota(jnp.int32, (T, T), 0)
    tc = lax.broadcasted_iota(jnp.int32, (T, T), 1)
    same_seq = (tr // L) == (tc // L)
    causal = same_seq & (tc <= tr)
    hi, mid, lo3 = _split3(la)
    tri = causal.astype(BF16)
    blk = same_seq.astype(BF16)
    b = _dot(tri, hi) + _dot(tri, mid) + _dot(tri, lo3)
    b_last = _dot(blk, hi) + _dot(blk, mid) + _dot(blk, lo3)
    q_hat = qg * jnp.exp(b)
    k_til = (kg * jnp.exp(-b)).astype(BF16)
    k_dec = kg * jnp.exp(b_last - b)
    hmasks = _head_masks(T)
    zeros_l = jnp.zeros((G, L, GLA_KEY_WIDTH), F32)
    for hh in range(GLA_HEADS):
        qhat_scr[:, hh * L:(hh + 1) * L, :] = jnp.where(hmasks[hh], q_hat, 0.0).reshape(G, L, GLA_KEY_WIDTH)
    kdec_scr[:, 0:L, :] = k_dec.reshape(G, L, GLA_KEY_WIDTH)
    kdec_scr[:, L:2 * L, :] = zeros_l
    vg_scr[:, 0:L, :] = vg.reshape(G, L, GLA_WIDTH)
    vg_scr[:, L:2 * L, :] = jnp.zeros((G, L, GLA_WIDTH), F32)
    la3_scr[:, 0:L, :] = hi.astype(F32).reshape(G, L, GLA_KEY_WIDTH)
    la3_scr[:, L:2 * L, :] = mid.astype(F32).reshape(G, L, GLA_KEY_WIDTH)
    la3_scr[:, 2 * L:3 * L, :] = lo3.astype(F32).reshape(G, L, GLA_KEY_WIDTH)
    la3_scr[:, 3 * L:4 * L, :] = zeros_l

    rr = lax.broadcasted_iota(jnp.int32, (nrow, LANES), 0) & (L - 1)
    cc = lax.broadcasted_iota(jnp.int32, (nrow, LANES), 1)
    cache_mask = cc >= rr
    sink_col = jnp.concatenate([jnp.full((L, 1), sink_ref[hh] * LOG2E, F32) for hh in range(ATT_HEADS)], axis=0)
    keep_old = lax.broadcasted_iota(jnp.int32, (KV_WIDTH, WINDOW), 1) < WINDOW - L
    ones16 = jnp.ones((4 * L, GLA_DV), BF16)

    def seq_body(it, carry):
        ns = [it * SEQ_UNROLL + u for u in range(SEQ_UNROLL)]
        kc = [ck_ref[n] for n in ns]
        vc = [cv_ref[n] for n in ns]
        s_c = [jnp.where(cache_mask, _dot(qrows_scr[n].astype(BF16), k.astype(BF16)), NEG)
               for n, k in zip(ns, kc)]
        st0 = [s0_ref[n] for n in ns]
        for n, s0 in zip(ns, st0):
            oi_scr[n] = _dot(qhat_scr[n].astype(BF16), s0.astype(BF16))
        kv = [_dot_tn(kdec_scr[n].astype(BF16), vg_scr[n].astype(BF16)) for n in ns]
        b_col = [_dot_tn(la3_scr[n].astype(BF16), ones16) for n in ns]
        s_n = [jnp.where((cc >= n * L) & (cc <= n * L + rr), snew_scr[n], NEG) for n in ns]
        m = [jnp.maximum(jnp.maximum(jnp.max(a, axis=-1, keepdims=True), jnp.max(b_, axis=-1, keepdims=True)),
                         sink_col) for a, b_ in zip(s_c, s_n)]
        e_c = [jnp.exp2(a - mm) for a, mm in zip(s_c, m)]
        e_n = [jnp.exp2(a - mm) for a, mm in zip(s_n, m)]
        for n, k, v in zip(ns, kc, vc):
            new_shift = WINDOW - L - n * L
            ok_ref[n] = jnp.where(keep_old, pltpu.roll(k, WINDOW - L, 1), pltpu.roll(k_new_t, new_shift, 1))
            ov_ref[n] = jnp.where(keep_old, pltpu.roll(v, WINDOW - L, 1), pltpu.roll(v_new_t, new_shift, 1))
        for n, ec, en, mm, v in zip(ns, e_c, e_n, m, vc):
            denom = (jnp.sum(ec, axis=-1, keepdims=True) + jnp.sum(en, axis=-1, keepdims=True)
                     + jnp.exp2(sink_col - mm))
            oc_scr[n] = _dot_nt(ec.astype(BF16), v.astype(BF16))
            pnew_scr[n] = en
            rden_scr[n] = jnp.broadcast_to(1.0 / denom, (nrow, LANES))
        for n, s0, kv_n, bc in zip(ns, st0, kv, b_col):
            upd = jnp.concatenate([kv_n[hh * GLA_DK:(hh + 1) * GLA_DK, hh * GLA_DV:(hh + 1) * GLA_DV]
                                   for hh in range(GLA_HEADS)], axis=0)
            s1_ref[n] = jnp.exp(bc) * s0 + upd
        return carry

    lax.fori_loop(0, G // SEQ_UNROLL, seq_body, 0)

    o_new = _dot(pnew_scr[...].reshape(G * nrow, T).astype(BF16), v_new.astype(BF16))
    o_att = ((oc_scr[...].reshape(G * nrow, KV_WIDTH) + o_new)
             * rden_scr[...].reshape(G * nrow, LANES)).reshape(G, nrow, KV_WIDTH)
    o_a = jnp.concatenate(
        [jnp.where(lo, o_att[:, i * L:(i + 1) * L, :].reshape(T, LANES),
                   o_att[:, (i + ATT_GROUP) * L:(i + ATT_GROUP + 1) * L, :].reshape(T, LANES))
         for i in range(ATT_GROUP)], axis=1)
    oi = oi_scr[...]
    o_inter = jnp.concatenate([oi[:, hh * L:(hh + 1) * L, :].reshape(T, GLA_DV) for hh in range(GLA_HEADS)],
                              axis=1)
    intra = []
    for hh in range(GLA_HEADS):
        a = _dot_nt(jnp.where(hmasks[hh], q_hat, 0.0).astype(BF16), k_til)
        a = jnp.where(causal, a, 0.0).astype(BF16)
        intra.append(_dot(a, vg[:, hh * GLA_DV:(hh + 1) * GLA_DV].astype(BF16)))
    o_g = o_inter + jnp.concatenate(intra, axis=1)

    y_ref[...] = _layer_tail(x, gate, o_a, o_g, _gates(p), glang_ref, wba_ref, wbg_ref, wout_ref)


def _sample_layer(x, mod, cos, sin, weights, ck, cv, s0):
    G, L, T = SEQ_GROUP, DEC_SEQ, T_SAMPLE
    nrow = ATT_HEADS * L
    row_spec = lambda w: pl.BlockSpec((T, w), lambda i: (i, 0))
    seq_spec = lambda a, b: pl.BlockSpec((G, a, b), lambda i: (i, 0, 0))
    return pl.pallas_call(
        _sample_body,
        grid=(DEC_BATCH // G,),
        in_specs=[row_spec(D_MODEL), pl.BlockSpec((G, 3 * D_MODEL), lambda i: (i, 0)),
                  _const_spec((T, LANES), 1), _const_spec((T, LANES), 1)]
                 + _weight_specs(1)
                 + [seq_spec(KV_WIDTH, WINDOW), seq_spec(KV_WIDTH, WINDOW), seq_spec(GLA_KEY_WIDTH, GLA_DV)],
        out_specs=[row_spec(D_MODEL), seq_spec(KV_WIDTH, WINDOW), seq_spec(KV_WIDTH, WINDOW),
                   seq_spec(GLA_KEY_WIDTH, GLA_DV)],
        out_shape=[
            jax.ShapeDtypeStruct((DEC_BATCH * L, D_MODEL), F32),
            jax.ShapeDtypeStruct((DEC_BATCH, KV_WIDTH, WINDOW), F32),
            jax.ShapeDtypeStruct((DEC_BATCH, KV_WIDTH, WINDOW), F32),
            jax.ShapeDtypeStruct((DEC_BATCH, GLA_KEY_WIDTH, GLA_DV), F32),
        ],
        scratch_shapes=[
            pltpu.VMEM((G, nrow, LANES), F32),
            pltpu.VMEM((G, nrow, T), F32),
            pltpu.VMEM((G, nrow, KV_WIDTH), F32),
            pltpu.VMEM((G, nrow, T), F32),
            pltpu.VMEM((G, nrow, LANES), F32),
            pltpu.VMEM((G, GLA_HEADS * L, GLA_KEY_WIDTH), F32),
            pltpu.VMEM((G, 2 * L, GLA_KEY_WIDTH), F32),
            pltpu.VMEM((G, 2 * L, GLA_WIDTH), F32),
            pltpu.VMEM((G, 4 * L, GLA_KEY_WIDTH), F32),
            pltpu.VMEM((G, GLA_HEADS * L, GLA_DV), F32),
        ],
        compiler_params=pltpu.CompilerParams(dimension_semantics=("arbitrary",),
                                             vmem_limit_bytes=VMEM_LIMIT),
        name="sample_layer",
    )(x, mod, cos, sin, *weights, ck, cv, s0)


def _rope_tables(pos):
    half = ATT_HEAD_DIM // 2
    inv = 1.0 / (ROPE_THETA ** (np.arange(half, dtype=np.float64) / half))
    ang = pos.astype(np.float64)[:, None] * inv[None, :]
    c, s = np.cos(ang), np.sin(ang)
    return (np.tile(c, (1, 4)).astype(np.float32),
            np.concatenate([-s, s, -s, s], axis=1).astype(np.float32))


def _pair_heads(w):
    shape = w.shape
    w = w.reshape((ATT_KV_HEADS, ATT_GROUP, ATT_HEAD_DIM) + shape[1:])
    return jnp.swapaxes(w, 0, 1).reshape(shape)


def kernel(x_prompt, x_sample, cache_win_k, cache_win_v, state_gla, c_prompt, c_sample, norm_g, w_ada, b_ada, w_in, q_norm_g, k_norm_g, attn_sinks, w_gla_gate, b_gla_gate, gla_norm_g, w_branch_att, w_branch_gla, w_out):
    assert w_in.shape == (1, D_MODEL, R_END), "single-layer trunk"
    wt = jnp.swapaxes(w_in[0], 0, 1)
    wgate = jnp.concatenate([w_gla_gate[0], jnp.zeros((LANES - GLA_GATE_RANK, GLA_KEY_WIDTH), F32)],
                            axis=0).astype(BF16)
    weights = (
        norm_g[0][None, :],
        _pair_heads(wt[0:R_K]).astype(BF16), _pair_heads(wt[R_ZA:R_QG]).astype(BF16), wt.astype(BF16),
        jnp.tile(q_norm_g[0], 2)[None, :], jnp.tile(k_norm_g[0], 2)[None, :],
        attn_sinks[0], wgate, b_gla_gate[0][None, :], gla_norm_g[0][None, :],
        _pair_heads(w_branch_att[0]).astype(BF16), w_branch_gla[0].astype(BF16), w_out[0].astype(BF16),
    )

    c_all = jnp.concatenate([c_sample, c_prompt, jnp.zeros((MOD_ROWS - DEC_BATCH - BATCH, D_MODEL), F32)], axis=0)
    mod = _modulation(c_all, w_ada[0], b_ada[0][None, :])
    mod_prompt = mod[DEC_BATCH:DEC_BATCH + BATCH].reshape(BATCH, 3, D_MODEL)

    cos_p, sin_p = _rope_tables(np.arange(SEQ))
    cos_s, sin_s = _rope_tables(PAST_LEN + np.arange(DEC_SEQ))
    cos_s, sin_s = np.tile(cos_s, (SEQ_GROUP, 1)), np.tile(sin_s, (SEQ_GROUP, 1))

    def kv_rows(c):
        return jnp.transpose(c, (0, 2, 3, 1)).reshape(c.shape[0], KV_WIDTH, WINDOW)

    def kv_out(c):
        return jnp.transpose(c.reshape(c.shape[0], ATT_KV_HEADS, ATT_HEAD_DIM, WINDOW), (0, 3, 1, 2))[None]

    y_p, wk_p, wv_p, st_p = _prompt_layer(x_prompt, mod_prompt, cos_p, sin_p, weights)
    y_s, wk_s, wv_s, st_s = _sample_layer(
        x_sample.reshape(DEC_BATCH * DEC_SEQ, D_MODEL), mod, cos_s, sin_s, weights,
        kv_rows(cache_win_k[0]), kv_rows(cache_win_v[0]),
        state_gla[0].reshape(DEC_BATCH, GLA_KEY_WIDTH, GLA_DV))

    st_shape = (1, -1, GLA_HEADS, GLA_DK, GLA_DV)
    return (y_p, y_s.reshape(DEC_BATCH, DEC_SEQ, D_MODEL),
            kv_out(wk_p), kv_out(wv_p), st_p.reshape(st_shape),
            kv_out(wk_s), kv_out(wv_s), st_s.reshape(st_shape))
```

```python
import numpy as np

import jax
import jax.numpy as jnp
from jax import lax
from jax.experimental import pallas as pl
from jax.experimental.pallas import tpu as pltpu

F32 = jnp.float32
BF16 = jnp.bfloat16

D_MODEL = 1024
BATCH = 2
SEQ = 8192
DEC_BATCH = 128
DEC_SEQ = 8
PAST_LEN = 16384
ATT_HEADS = 8
ATT_KV_HEADS = 2
ATT_GROUP = ATT_HEADS // ATT_KV_HEADS
ATT_HEAD_DIM = 64
ATT_WIDTH = ATT_HEADS * ATT_HEAD_DIM
KV_WIDTH = ATT_KV_HEADS * ATT_HEAD_DIM
WINDOW = 128
ATT_BLOCK = 128
ROPE_THETA = 10000.0
GLA_HEADS = 4
GLA_WIDTH = D_MODEL // 2
GLA_DV = GLA_WIDTH // GLA_HEADS
GLA_KEY_WIDTH = GLA_WIDTH // 2
GLA_DK = GLA_KEY_WIDTH // GLA_HEADS
GLA_GATE_RANK = 16
GLA_GATE_TAU = 16.0
EPS = 1e-6
NEG = -1e30
LOG2E = 1.4426950408889634
ATT_SCALE = ATT_HEAD_DIM ** -0.5 * LOG2E

LANES = 128
HALF = LANES // 2
MXU_TILE = 256
R_K, R_V, R_ZA = ATT_WIDTH, ATT_WIDTH + KV_WIDTH, ATT_WIDTH + 2 * KV_WIDTH
R_QG = R_ZA + ATT_WIDTH
R_KG = R_QG + GLA_KEY_WIDTH
R_VG = R_KG + GLA_KEY_WIDTH
R_LR = R_VG + GLA_WIDTH
R_ZG = R_LR + GLA_GATE_RANK
R_MA = R_ZG + GLA_WIDTH
R_MG = R_MA + D_MODEL
R_END = R_MG + D_MODEL

T_PROMPT = 512
N_BLOCKS = SEQ // T_PROMPT
GLA_BLOCK = 64
SEQ_GROUP = DEC_BATCH // (BATCH * N_BLOCKS)
T_SAMPLE = SEQ_GROUP * DEC_SEQ
MOD_ROWS = DEC_BATCH + 16
MOD_TILE = 512
VMEM_LIMIT = 56 * 1024 * 1024


def _sigmoid(x):
    return 1.0 / (1.0 + jnp.exp2(x * -LOG2E))


def _dot(a, b):
    return jnp.dot(a, b, preferred_element_type=F32)


def _dot_nt(a, b):
    return lax.dot_general(a, b, (((1,), (1,)), ((), ())), preferred_element_type=F32)


def _dot_tn(a, b):
    return lax.dot_general(a, b, (((0,), (0,)), ((), ())), preferred_element_type=F32)


def _split3(x):
    hi = x.astype(BF16)
    r1 = x - hi.astype(F32)
    mid = r1.astype(BF16)
    lo = (r1 - mid.astype(F32)).astype(BF16)
    return hi, mid, lo


def _norm_rope_slab(xs, g, cos, sin):
    lane = lax.broadcasted_iota(jnp.int32, xs.shape, 1)
    lo = lane < HALF
    first = (lane & (HALF // 2)) == 0
    sq = xs * xs
    s_lo = jnp.sum(jnp.where(lo, sq, 0.0), axis=-1, keepdims=True)
    s_hi = jnp.sum(jnp.where(lo, 0.0, sq), axis=-1, keepdims=True)
    inv = jnp.where(lo, lax.rsqrt(s_lo * (1.0 / ATT_HEAD_DIM) + EPS),
                    lax.rsqrt(s_hi * (1.0 / ATT_HEAD_DIM) + EPS))
    xn = xs * inv * g
    swapped = jnp.where(first, pltpu.roll(xn, LANES - HALF // 2, 1), pltpu.roll(xn, HALF // 2, 1))
    return xn * cos + swapped * sin


def _log_decay(lrz, wgate_ref, bgate_ref):
    pre = _dot(lrz.astype(BF16), wgate_ref[...]) + bgate_ref[...]
    log_sig = jnp.minimum(pre, 0.0) - jnp.log(1.0 + jnp.exp(-jnp.abs(pre)))
    return log_sig * (1.0 / GLA_GATE_TAU)


def _head_masks(rows):
    lane = lax.broadcasted_iota(jnp.int32, (rows, GLA_KEY_WIDTH), 1)
    return [(lane >= h * GLA_DK) & (lane < (h + 1) * GLA_DK) for h in range(GLA_HEADS)]


def _softmax_block(s, mask, sink_ref):
    probs, rden = [], []
    for h in range(ATT_HEADS):
        sh = jnp.where(mask, s[h * ATT_BLOCK:(h + 1) * ATT_BLOCK], NEG)
        sink = sink_ref[h] * LOG2E
        m = jnp.maximum(jnp.max(sh, axis=-1, keepdims=True), sink)
        e = jnp.exp2(sh - m)
        rden.append(1.0 / (jnp.sum(e, axis=-1, keepdims=True) + jnp.exp2(sink - m)))
        probs.append(e.astype(BF16))
    return jnp.concatenate(probs, axis=0), rden


def _mod_body(c_ref, w_ref, b_ref, o_ref, act_ref):
    @pl.when(pl.program_id(0) == 0)
    def _():
        c = c_ref[...]
        act_ref[...] = (c * _sigmoid(c)).astype(BF16)

    o_ref[...] = _dot(act_ref[...], w_ref[...].astype(BF16)) + b_ref[...]


def _modulation(c_all, w_ada, b_ada):
    rows = c_all.shape[0]
    return pl.pallas_call(
        _mod_body,
        grid=(3 * D_MODEL // MOD_TILE,),
        in_specs=[
            pl.BlockSpec((rows, D_MODEL), lambda i: (0, 0)),
            pl.BlockSpec((D_MODEL, MOD_TILE), lambda i: (0, i)),
            pl.BlockSpec((1, MOD_TILE), lambda i: (0, i)),
        ],
        out_specs=pl.BlockSpec((rows, MOD_TILE), lambda i: (0, i)),
        out_shape=jax.ShapeDtypeStruct((rows, 3 * D_MODEL), F32),
        scratch_shapes=[pltpu.VMEM((rows, D_MODEL), BF16)],
        compiler_params=pltpu.CompilerParams(dimension_semantics=("arbitrary",),
                                             vmem_limit_bytes=VMEM_LIMIT),
        name="adaln_mod",
    )(c_all, w_ada, b_ada)


def _layer_body(xp_ref, xs_ref, modp_ref, mods_ref, cosp_ref, sinp_ref, coss_ref, sins_ref,
                ng_ref, wq_ref, wza_ref, wt_ref, qng_ref, kng_ref, sink_ref,
                wgate_ref, bgate_ref, glang_ref, wba_ref, wbg_ref, wout_ref,
                ck_ref, cv_ref, s0_ref,
                yp_ref, ys_ref, wk_ref, wv_ref, st_ref, ok_ref, ov_ref, s1_ref,
                kprev_ref, vprev_ref, st_scr, klast_scr, vlast_scr):
    TP, TS, C, L, GS = T_PROMPT, T_SAMPLE, GLA_BLOCK, DEC_SEQ, SEQ_GROUP
    n_att, n_gla = TP // ATT_BLOCK, TP // C
    nrow = ATT_HEADS * L
    j = pl.program_id(1)
    last = pl.num_programs(1) - 1

    @pl.when(j == 0)
    def _():
        kprev_ref[...] = jnp.zeros_like(kprev_ref)
        vprev_ref[...] = jnp.zeros_like(vprev_ref)
        st_scr[...] = jnp.zeros_like(st_scr)

    def per_token(a, b):
        return jnp.concatenate([jnp.broadcast_to(mods_ref[u:u + 1, a:b], (L, b - a)) for u in range(GS)], axis=0)

    xp, xs = xp_ref[...], xs_ref[...]
    hp = (xp * lax.rsqrt(jnp.mean(xp * xp, axis=-1, keepdims=True) + EPS) * ng_ref[...]
          * (1.0 + modp_ref[1:2, :]) + modp_ref[0:1, :])
    hs = (xs * lax.rsqrt(jnp.mean(xs * xs, axis=-1, keepdims=True) + EPS) * ng_ref[...]
          * (1.0 + per_token(D_MODEL, 2 * D_MODEL)) + per_token(0, D_MODEL))
    h = jnp.concatenate([hp, hs], axis=0).astype(BF16)

    def proj(a, b):
        return _dot_nt(h, wt_ref[a:b, :])

    cos = jnp.concatenate([cosp_ref[...], coss_ref[...]], axis=0)
    sin = jnp.concatenate([sinp_ref[...], sins_ref[...]], axis=0)
    lane = lax.broadcasted_iota(jnp.int32, (ATT_BLOCK, LANES), 1)
    lo = lane < HALF
    v_ = {}

    def a1_qkv():
        kv = proj(R_K, R_ZA)
        v_["q"], v_["k_raw"], v_["v"] = _dot_nt(h, wq_ref[...]), kv[:, :KV_WIDTH], kv[:, KV_WIDTH:]

    def a2_norm_rope():
        q = v_["q"]
        v_["q_slabs"] = [_norm_rope_slab(q[:, i * LANES:(i + 1) * LANES], qng_ref[...], cos, sin)
                         * ATT_SCALE for i in range(ATT_GROUP)]
        k = _norm_rope_slab(v_["k_raw"], kng_ref[...], cos, sin)
        v = v_["v"]
        klast_scr[...] = k[TP - WINDOW:TP]
        vlast_scr[...] = v[TP - WINDOW:TP]
        v_["k16"], v_["v16"] = k[:TP].astype(BF16), v[:TP].astype(BF16)
        v_["k_s"], v_["v_s"] = k[TP:], v[TP:]

    def g1_decay():
        la = _log_decay(proj(R_LR, R_LR + LANES), wgate_ref, bgate_ref)
        v_["la"], v_["la_s"] = la[:TP], la[TP:]

    def g2_qkv():
        qg = proj(R_QG, R_KG) * (GLA_DK ** -0.5)
        kg = proj(R_KG, R_VG)
        vg = proj(R_VG, R_LR)
        v_["qg"], v_["kg"], v_["vg16"] = qg[:TP], kg[:TP], vg[:TP].astype(BF16)
        v_["qg_s"], v_["kg_s"], v_["vg_s"] = qg[TP:], kg[TP:], vg[TP:]

    def a3_scores():
        k16, v16, q_slabs = v_["k16"], v_["v16"], v_["q_slabs"]
        r = lax.broadcasted_iota(jnp.int32, (ATT_BLOCK, 2 * ATT_BLOCK), 0)
        c = lax.broadcasted_iota(jnp.int32, (ATT_BLOCK, 2 * ATT_BLOCK), 1)
        band = ((c < ATT_BLOCK) & (c >= r)) | ((c >= ATT_BLOCK) & (c - ATT_BLOCK <= r))
        v_["masks"] = [band & ((c >= ATT_BLOCK) | (j > 0))] + [band] * (n_att - 1)
        scores, vcats = [], []
        for i in range(n_att):
            rows = slice(i * ATT_BLOCK, (i + 1) * ATT_BLOCK)
            if i == 0:
                kp, vp = kprev_ref[...], vprev_ref[...]
            else:
                prev = slice((i - 1) * ATT_BLOCK, i * ATT_BLOCK)
                kp, vp = k16[prev], v16[prev]
            pieces = ([jnp.where(lo, s[rows], 0.0).astype(BF16) for s in q_slabs]
                      + [jnp.where(lo, 0.0, s[rows]).astype(BF16) for s in q_slabs])
            scores.append(_dot_nt(jnp.concatenate(pieces, axis=0), jnp.concatenate([kp, k16[rows]], axis=0)))
            vcats.append(jnp.concatenate([vp, v16[rows]], axis=0))
        kprev_ref[...] = k16[TP - ATT_BLOCK:]
        vprev_ref[...] = v16[TP - ATT_BLOCK:]
        v_["scores"], v_["vcats"] = scores, vcats

    v_["soft"] = [None] * n_att

    def softmax_stage(i):
        def run():
            v_["soft"][i] = _softmax_block(v_["scores"][i], v_["masks"][i], sink_ref)
        return run

    a4 = [softmax_stage(i) for i in range(n_att)]

    def a5_values():
        o_rows = []
        for i in range(n_att):
            probs, rden = v_["soft"][i]
            o = _dot(probs, v_["vcats"][i])
            o_rows.append(jnp.concatenate(
                [jnp.where(lo, o[g * ATT_BLOCK:(g + 1) * ATT_BLOCK] * rden[g],
                           o[(g + ATT_GROUP) * ATT_BLOCK:(g + ATT_GROUP + 1) * ATT_BLOCK] * rden[g + ATT_GROUP])
                 for g in range(ATT_GROUP)], axis=1))
        v_["o_a"] = jnp.concatenate(o_rows, axis=0)

    def g3_cumsum():
        tr = lax.broadcasted_iota(jnp.int32, (TP, TP), 0)
        tc = lax.broadcasted_iota(jnp.int32, (TP, TP), 1)
        tri = (((tr // C) == (tc // C)) & (tc <= tr)).astype(BF16)
        hi, mid, lo3 = _split3(v_["la"])
        b = _dot(tri, hi) + _dot(tri, mid) + _dot(tri, lo3)
        v_["b"] = b
        v_["b_last"] = [b[(i + 1) * C - 1:(i + 1) * C] for i in range(n_gla)]
        v_["b_mid"] = [b[i * C + C // 2 - 1:i * C + C // 2] for i in range(n_gla)]

    def g4_decayed():
        b, qg, kg = v_["b"], v_["qg"], v_["kg"]
        b_end = jnp.concatenate([jnp.broadcast_to(bl, (C, GLA_KEY_WIDTH)) for bl in v_["b_last"]], axis=0)
        b_ref = jnp.concatenate([jnp.broadcast_to(bm, (C, GLA_KEY_WIDTH)) for bm in v_["b_mid"]], axis=0)
        q_hat = qg * jnp.exp(b - b_ref)
        v_["k_til"] = (kg * jnp.exp(b_ref - b)).astype(BF16)
        k_dec = kg * jnp.exp(b_end - b)
        hmasks = _head_masks(TP)
        v_["q_m"] = [jnp.where(hm, q_hat, 0.0).astype(BF16) for hm in hmasks]
        v_["k_m"] = [jnp.where(hm, k_dec, 0.0).astype(BF16) for hm in hmasks]

    def g5_states():
        vg16 = v_["vg16"]
        states = [st_scr[...]]
        for i in range(n_gla):
            rows = slice(i * C, (i + 1) * C)
            v_stack = jnp.concatenate([vg16[rows, hh * GLA_DV:(hh + 1) * GLA_DV] for hh in range(GLA_HEADS)],
                                      axis=0)
            k_stack = jnp.concatenate([km[rows] for km in v_["k_m"]], axis=0)
            states.append(states[-1] * jnp.exp(v_["b_last"][i]) + _dot_tn(v_stack, k_stack))
        st_scr[...] = states[-1]
        v_["states"] = states

    def g6_mixed():
        mixed = []
        for i in range(n_gla):
            rows = slice(i * C, (i + 1) * C)
            q_stack = jnp.concatenate([qm[rows] for qm in v_["q_m"]], axis=0)
            s_ref = v_["states"][i] * jnp.exp(v_["b_mid"][i])
            rhs = jnp.concatenate([s_ref.astype(BF16), v_["k_til"][rows]], axis=0)
            mixed.append(_dot_nt(q_stack, rhs))
        v_["mixed"] = mixed

    def g7_outputs():
        sr = lax.broadcasted_iota(jnp.int32, (GLA_HEADS * C, C), 0)
        sc = lax.broadcasted_iota(jnp.int32, (GLA_HEADS * C, C), 1)
        causal = (sr & (C - 1)) >= sc
        vg16 = v_["vg16"]
        og_rows = []
        for i in range(n_gla):
            rows = slice(i * C, (i + 1) * C)
            m_i = v_["mixed"][i]
            a = jnp.where(causal, m_i[:, GLA_DV:], 0.0).astype(BF16)
            og_rows.append(jnp.concatenate(
                [m_i[hh * C:(hh + 1) * C, :GLA_DV]
                 + _dot(a[hh * C:(hh + 1) * C], vg16[rows, hh * GLA_DV:(hh + 1) * GLA_DV])
                 for hh in range(GLA_HEADS)], axis=1))
        v_["o_g"] = jnp.concatenate(og_rows, axis=0)

    seqs = range(GS)

    def s1_inputs():
        lo_s = lax.broadcasted_iota(jnp.int32, (L, LANES), 1) < HALF
        qrows = []
        for u in seqs:
            rows = slice(TP + u * L, TP + (u + 1) * L)
            pieces = ([jnp.where(lo_s, s[rows], 0.0) for s in v_["q_slabs"]]
                      + [jnp.where(lo_s, 0.0, s[rows]) for s in v_["q_slabs"]])
            qrows.append(jnp.concatenate(pieces, axis=0).astype(BF16))
        pad = jnp.zeros((WINDOW - TS, KV_WIDTH), F32)
        k_pad = jnp.concatenate([v_["k_s"], pad], axis=0)
        v_pad = jnp.concatenate([v_["v_s"], pad], axis=0)
        v_["qrows"], v_["k_pad_t"], v_["v_pad_t"], v_["v_pad"] = qrows, k_pad.T, v_pad.T, v_pad
        v_["snew"] = _dot(jnp.concatenate(qrows, axis=0), v_["k_pad_t"].astype(BF16))

    def s2_windows():
        keep_old = lax.broadcasted_iota(jnp.int32, (KV_WIDTH, WINDOW), 1) < WINDOW - L
        for u in seqs:
            shift = WINDOW - L - u * L
            ok_ref[u] = jnp.where(keep_old, pltpu.roll(ck_ref[u], WINDOW - L, 1),
                                  pltpu.roll(v_["k_pad_t"], shift, 1))
            ov_ref[u] = jnp.where(keep_old, pltpu.roll(cv_ref[u], WINDOW - L, 1),
                                  pltpu.roll(v_["v_pad_t"], shift, 1))

    def s3_scores():
        rr = lax.broadcasted_iota(jnp.int32, (nrow, LANES), 0) & (L - 1)
        cc = lax.broadcasted_iota(jnp.int32, (nrow, LANES), 1)
        v_["s_c"] = [jnp.where(cc >= rr, _dot(v_["qrows"][u], ck_ref[u].astype(BF16)), NEG) for u in seqs]
        v_["s_n"] = [jnp.where((cc >= u * L) & (cc <= u * L + rr), v_["snew"][u * nrow:(u + 1) * nrow], NEG)
                     for u in seqs]

    def s4_softmax():
        sink_col = jnp.concatenate([jnp.full((L, 1), sink_ref[hh] * LOG2E, F32) for hh in range(ATT_HEADS)],
                                   axis=0)
        e_c, e_n, rden = [], [], []
        for a, b_ in zip(v_["s_c"], v_["s_n"]):
            m = jnp.maximum(jnp.maximum(jnp.max(a, axis=-1, keepdims=True), jnp.max(b_, axis=-1, keepdims=True)),
                            sink_col)
            ec, en = jnp.exp2(a - m), jnp.exp2(b_ - m)
            rden.append(1.0 / (jnp.sum(ec, axis=-1, keepdims=True) + jnp.sum(en, axis=-1, keepdims=True)
                               + jnp.exp2(sink_col - m)))
            e_c.append(ec.astype(BF16))
            e_n.append(en.astype(BF16))
        v_["e_c"], v_["e_n"], v_["rden_s"] = e_c, e_n, rden

    def s5_values():
        lo_s = lax.broadcasted_iota(jnp.int32, (L, LANES), 1) < HALF
        o_new = _dot(jnp.concatenate(v_["e_n"], axis=0), v_["v_pad"].astype(BF16))
        rows_out = []
        for u in seqs:
            o = ((_dot_nt(v_["e_c"][u], cv_ref[u].astype(BF16)) + o_new[u * nrow:(u + 1) * nrow])
                 * v_["rden_s"][u])
            rows_out.append(jnp.concatenate(
                [jnp.where(lo_s, o[i * L:(i + 1) * L], o[(i + ATT_GROUP) * L:(i + ATT_GROUP + 1) * L])
                 for i in range(ATT_GROUP)], axis=1))
        v_["o_a_s"] = jnp.concatenate(rows_out, axis=0)

    def s6_gla():
        qg, kg, vg, la = v_["qg_s"], v_["kg_s"], v_["vg_s"], v_["la_s"]
        tr = lax.broadcasted_iota(jnp.int32, (TS, TS), 0)
        tc = lax.broadcasted_iota(jnp.int32, (TS, TS), 1)
        same_seq = (tr // L) == (tc // L)
        causal = same_seq & (tc <= tr)
        parts = _split3(la)
        tri, blk = causal.astype(BF16), same_seq.astype(BF16)
        b = _dot(tri, parts[0]) + _dot(tri, parts[1]) + _dot(tri, parts[2])
        b_last = _dot(blk, parts[0]) + _dot(blk, parts[1]) + _dot(blk, parts[2])
        q_hat = qg * jnp.exp(b)
        k_til = (kg * jnp.exp(-b)).astype(BF16)
        k_dec = kg * jnp.exp(b_last - b)
        hmasks = _head_masks(TS)
        q_m = [jnp.where(hm, q_hat, 0.0) for hm in hmasks]
        decay = jnp.exp(b_last)
        zk = jnp.zeros((L, GLA_DK), F32)
        zv = jnp.zeros((L, GLA_DV), F32)
        st0 = [s0_ref[u] for u in seqs]
        inter, upds, dcols = [], [], []
        for u in seqs:
            rows = slice(u * L, (u + 1) * L)
            q_stack = jnp.concatenate([qm[rows] for qm in q_m], axis=0).astype(BF16)
            inter.append(_dot(q_stack, st0[u].astype(BF16)))
            per_head = []
            for hh in range(GLA_HEADS):
                k_rows = jnp.concatenate([k_dec[rows, hh * GLA_DK:(hh + 1) * GLA_DK], zk], axis=0).astype(BF16)
                v_rows = jnp.concatenate([vg[rows, hh * GLA_DV:(hh + 1) * GLA_DV], zv], axis=0).astype(BF16)
                per_head.append(_dot_tn(k_rows, v_rows))
            upds.append(jnp.concatenate(per_head, axis=0))
            dcols.append(decay[rows].T[:, 0:1])
        intra = []
        for hh in range(GLA_HEADS):
            a = jnp.where(causal, _dot_nt(q_m[hh].astype(BF16), k_til), 0.0).astype(BF16)
            intra.append(_dot(a, vg[:, hh * GLA_DV:(hh + 1) * GLA_DV].astype(BF16)))
        for u in seqs:
            s1_ref[u] = dcols[u] * st0[u] + upds[u]
        o_inter = jnp.concatenate(
            [jnp.concatenate([inter[u][hh * L:(hh + 1) * L] for hh in range(GLA_HEADS)], axis=1) for u in seqs],
            axis=0)
        v_["o_g_s"] = o_inter + jnp.concatenate(intra, axis=1)

    gate_parts = {"silu_a": [], "silu_g": [], "merge_a": [], "merge_g": []}

    def gate_chunk(key, w_ref, row, silu):
        def run():
            z = _dot_nt(h, w_ref[row:row + MXU_TILE, :])
            s = _sigmoid(z)
            gate_parts[key].append(z * s if silu else s)
        return run

    t = ([gate_chunk("silu_a", wza_ref, r, True) for r in range(0, ATT_WIDTH, MXU_TILE)]
         + [gate_chunk("silu_g", wt_ref, R_ZG + r, True) for r in range(0, GLA_WIDTH, MXU_TILE)]
         + [gate_chunk("merge_a", wt_ref, R_MA + r, False) for r in range(0, D_MODEL, MXU_TILE)]
         + [gate_chunk("merge_g", wt_ref, R_MG + r, False) for r in range(0, D_MODEL, MXU_TILE)])

    def t_tail():
        silu_a, silu_g, merge_a, merge_g = (jnp.concatenate(gate_parts[key], axis=1)
                                            for key in ("silu_a", "silu_g", "merge_a", "merge_g"))
        o_a = jnp.concatenate([v_["o_a"], v_["o_a_s"]], axis=0) * silu_a
        o_g = jnp.concatenate([v_["o_g"], v_["o_g_s"]], axis=0)
        slabs = []
        for hh in range(GLA_HEADS):
            oh = o_g[:, hh * GLA_DV:(hh + 1) * GLA_DV]
            slabs.append(oh * lax.rsqrt(jnp.mean(oh * oh, axis=-1, keepdims=True) + EPS) * glang_ref[...])
        o_g = jnp.concatenate(slabs, axis=1) * silu_g
        merged = (merge_a * _dot(o_a.astype(BF16), wba_ref[...])
                  + merge_g * _dot(o_g.astype(BF16), wbg_ref[...]))
        out = _dot(merged.astype(BF16), wout_ref[...])
        yp_ref[...] = xp + modp_ref[2:3, :] * out[:TP]
        ys_ref[...] = xs + per_token(2 * D_MODEL, 3 * D_MODEL) * out[TP:]

    for stage in (a1_qkv, g1_decay, t[0], g2_qkv, t[1], a2_norm_rope, t[2], s1_inputs, g3_cumsum, t[3],
                  a3_scores, s2_windows, t[4], g4_decayed, s3_scores, a4[0], t[5], a4[1], s4_softmax, t[6], a4[2],
                  t[7], a4[3], g5_states, s5_values, t[8], a5_values, s6_gla, t[9], g6_mixed, t[10], g7_outputs,
                  t[11], t_tail):
        stage()

    @pl.when(j == last)
    def _():
        wk_ref[...] = klast_scr[...].T
        wv_ref[...] = vlast_scr[...].T
        st_ref[...] = st_scr[...].T


def _fused_layer(xp, xs, mod_p, mod_s, cos_p, sin_p, cos_s, sin_s, weights, ck, cv, s0):
    GS, TS = SEQ_GROUP, T_SAMPLE

    def const(shape):
        zeros = (0,) * len(shape)
        return pl.BlockSpec(shape, lambda b, j: zeros)

    def per_batch(shape):
        return pl.BlockSpec((None,) + shape, lambda b, j: (b, 0, 0))

    def per_step(shape):
        tail = (0,) * (len(shape) - 1)
        return pl.BlockSpec(shape, lambda b, j: (b * N_BLOCKS + j,) + tail)

    prompt_rows = pl.BlockSpec((None, T_PROMPT, D_MODEL), lambda b, j: (b, j, 0))
    table_rows = pl.BlockSpec((T_PROMPT, LANES), lambda b, j: (j, 0))
    weight_specs = [
        const((1, D_MODEL)),
        const((ATT_WIDTH, D_MODEL)),
        const((ATT_WIDTH, D_MODEL)),
        const((R_END, D_MODEL)),
        const((1, LANES)),
        const((1, LANES)),
        pl.BlockSpec(memory_space=pltpu.SMEM),
        const((LANES, GLA_KEY_WIDTH)),
        const((1, GLA_KEY_WIDTH)),
        const((1, GLA_DV)),
        const((ATT_WIDTH, D_MODEL)),
        const((GLA_WIDTH, D_MODEL)),
        const((D_MODEL, D_MODEL)),
    ]
    return pl.pallas_call(
        _layer_body,
        grid=(BATCH, N_BLOCKS),
        in_specs=[prompt_rows, per_step((TS, D_MODEL)), per_batch((3, D_MODEL)),
                  pl.BlockSpec((None, GS, 3 * D_MODEL), lambda b, j: (b * N_BLOCKS + j, 0, 0)),
                  table_rows, table_rows, const((TS, LANES)), const((TS, LANES))]
                 + weight_specs
                 + [per_step((GS, KV_WIDTH, WINDOW)), per_step((GS, KV_WIDTH, WINDOW)),
                    per_step((GS, GLA_KEY_WIDTH, GLA_DV))],
        out_specs=[prompt_rows, per_step((TS, D_MODEL)),
                   per_batch((KV_WIDTH, WINDOW)), per_batch((KV_WIDTH, WINDOW)), per_batch((GLA_KEY_WIDTH, GLA_DV)),
                   per_step((GS, KV_WIDTH, WINDOW)), per_step((GS, KV_WIDTH, WINDOW)),
                   per_step((GS, GLA_KEY_WIDTH, GLA_DV))],
        out_shape=[
            jax.ShapeDtypeStruct((BATCH, SEQ, D_MODEL), F32),
            jax.ShapeDtypeStruct((DEC_BATCH * DEC_SEQ, D_MODEL), F32),
            jax.ShapeDtypeStruct((BATCH, KV_WIDTH, WINDOW), F32),
            jax.ShapeDtypeStruct((BATCH, KV_WIDTH, WINDOW), F32),
            jax.ShapeDtypeStruct((BATCH, GLA_KEY_WIDTH, GLA_DV), F32),
            jax.ShapeDtypeStruct((DEC_BATCH, KV_WIDTH, WINDOW), F32),
            jax.ShapeDtypeStruct((DEC_BATCH, KV_WIDTH, WINDOW), F32),
            jax.ShapeDtypeStruct((DEC_BATCH, GLA_KEY_WIDTH, GLA_DV), F32),
        ],
        scratch_shapes=[
            pltpu.VMEM((ATT_BLOCK, KV_WIDTH), BF16),
            pltpu.VMEM((ATT_BLOCK, KV_WIDTH), BF16),
            pltpu.VMEM((GLA_DV, GLA_KEY_WIDTH), F32),
            pltpu.VMEM((WINDOW, KV_WIDTH), F32),
            pltpu.VMEM((WINDOW, KV_WIDTH), F32),
        ],
        compiler_params=pltpu.CompilerParams(dimension_semantics=("arbitrary", "arbitrary"),
                                             vmem_limit_bytes=VMEM_LIMIT),
        name="fused_layer",
    )(xp, xs, mod_p, mod_s, cos_p, sin_p, cos_s, sin_s, *weights, ck, cv, s0)


def _rope_tables(pos):
    half = ATT_HEAD_DIM // 2
    inv = 1.0 / (ROPE_THETA ** (np.arange(half, dtype=np.float64) / half))
    ang = pos.astype(np.float64)[:, None] * inv[None, :]
    c, s = np.cos(ang), np.sin(ang)
    return (np.tile(c, (1, 4)).astype(np.float32),
            np.concatenate([-s, s, -s, s], axis=1).astype(np.float32))


def _pair_heads(w):
    shape = w.shape
    w = w.reshape((ATT_KV_HEADS, ATT_GROUP, ATT_HEAD_DIM) + shape[1:])
    return jnp.swapaxes(w, 0, 1).reshape(shape)


def kernel(x_prompt, x_sample, cache_win_k, cache_win_v, state_gla, c_prompt, c_sample, norm_g, w_ada, b_ada, w_in, q_norm_g, k_norm_g, attn_sinks, w_gla_gate, b_gla_gate, gla_norm_g, w_branch_att, w_branch_gla, w_out):
    assert w_in.shape == (1, D_MODEL, R_END), "single-layer trunk"
    wt = jnp.swapaxes(w_in[0], 0, 1)
    wgate = jnp.concatenate([w_gla_gate[0], jnp.zeros((LANES - GLA_GATE_RANK, GLA_KEY_WIDTH), F32)],
                            axis=0).astype(BF16)
    weights = (
        norm_g[0][None, :],
        _pair_heads(wt[0:R_K]).astype(BF16), _pair_heads(wt[R_ZA:R_QG]).astype(BF16), wt.astype(BF16),
        jnp.tile(q_norm_g[0], 2)[None, :], jnp.tile(k_norm_g[0], 2)[None, :],
        attn_sinks[0], wgate, b_gla_gate[0][None, :], gla_norm_g[0][None, :],
        _pair_heads(w_branch_att[0]).astype(BF16), w_branch_gla[0].astype(BF16), w_out[0].astype(BF16),
    )

    c_all = jnp.concatenate([c_sample, c_prompt, jnp.zeros((MOD_ROWS - DEC_BATCH - BATCH, D_MODEL), F32)], axis=0)
    mod = _modulation(c_all, w_ada[0], b_ada[0][None, :])
    mod_p = mod[DEC_BATCH:DEC_BATCH + BATCH].reshape(BATCH, 3, D_MODEL)
    mod_s = mod[:DEC_BATCH].reshape(DEC_BATCH // SEQ_GROUP, SEQ_GROUP, 3 * D_MODEL)

    cos_p, sin_p = _rope_tables(np.arange(SEQ))
    cos_s, sin_s = _rope_tables(PAST_LEN + np.arange(DEC_SEQ))
    cos_s, sin_s = np.tile(cos_s, (SEQ_GROUP, 1)), np.tile(sin_s, (SEQ_GROUP, 1))

    def kv_rows(c):
        return jnp.transpose(c, (0, 2, 3, 1)).reshape(c.shape[0], KV_WIDTH, WINDOW)

    def kv_out(c):
        return jnp.transpose(c.reshape(c.shape[0], ATT_KV_HEADS, ATT_HEAD_DIM, WINDOW), (0, 3, 1, 2))[None]

    y_p, y_s, wk_p, wv_p, st_p, wk_s, wv_s, st_s = _fused_layer(
        x_prompt, x_sample.reshape(DEC_BATCH * DEC_SEQ, D_MODEL), mod_p, mod_s, cos_p, sin_p, cos_s, sin_s, weights,
        kv_rows(cache_win_k[0]), kv_rows(cache_win_v[0]),
        state_gla[0].reshape(DEC_BATCH, GLA_KEY_WIDTH, GLA_DV))

    st_shape = (1, -1, GLA_HEADS, GLA_DK, GLA_DV)
    return (y_p, y_s.reshape(DEC_BATCH, DEC_SEQ, D_MODEL),
            kv_out(wk_p), kv_out(wv_p), st_p.reshape(st_shape),
            kv_out(wk_s), kv_out(wv_s), st_s.reshape(st_shape))
```

```python
import numpy as np

import jax
import jax.numpy as jnp
from jax import lax
from jax.experimental import pallas as pl
from jax.experimental.pallas import tpu as pltpu

F32 = jnp.float32
BF16 = jnp.bfloat16

D_MODEL = 1024
BATCH = 2
SEQ = 8192
DEC_BATCH = 128
DEC_SEQ = 8
PAST_LEN = 16384
ATT_HEADS = 8
ATT_KV_HEADS = 2
ATT_GROUP = ATT_HEADS // ATT_KV_HEADS
ATT_HEAD_DIM = 64
ATT_WIDTH = ATT_HEADS * ATT_HEAD_DIM
KV_WIDTH = ATT_KV_HEADS * ATT_HEAD_DIM
WINDOW = 128
ATT_BLOCK = 128
ROPE_THETA = 10000.0
GLA_HEADS = 4
GLA_WIDTH = D_MODEL // 2
GLA_DV = GLA_WIDTH // GLA_HEADS
GLA_KEY_WIDTH = GLA_WIDTH // 2
GLA_DK = GLA_KEY_WIDTH // GLA_HEADS
GLA_GATE_RANK = 16
GLA_GATE_TAU = 16.0
EPS = 1e-6
NEG = -1e30
LOG2E = 1.4426950408889634
ATT_SCALE = ATT_HEAD_DIM ** -0.5 * LOG2E

LANES = 128
HALF = LANES // 2
MXU_TILE = 256
R_K, R_V, R_ZA = ATT_WIDTH, ATT_WIDTH + KV_WIDTH, ATT_WIDTH + 2 * KV_WIDTH
R_QG = R_ZA + ATT_WIDTH
R_KG = R_QG + GLA_KEY_WIDTH
R_VG = R_KG + GLA_KEY_WIDTH
R_LR = R_VG + GLA_WIDTH
R_ZG = R_LR + GLA_GATE_RANK
R_MA = R_ZG + GLA_WIDTH
R_MG = R_MA + D_MODEL
R_END = R_MG + D_MODEL

T_PROMPT = 512
N_BLOCKS = SEQ // T_PROMPT
GLA_BLOCK = 64
SEQ_GROUP = DEC_BATCH // (BATCH * N_BLOCKS)
T_SAMPLE = SEQ_GROUP * DEC_SEQ
MOD_ROWS = DEC_BATCH + 16
MOD_TILE = 1536
VMEM_LIMIT = 56 * 1024 * 1024


def _sigmoid(x):
    return 1.0 / (1.0 + jnp.exp2(x * -LOG2E))


def _dot(a, b):
    return jnp.dot(a, b, preferred_element_type=F32)


def _dot_nt(a, b):
    return lax.dot_general(a, b, (((1,), (1,)), ((), ())), preferred_element_type=F32)


def _dot_tn(a, b):
    return lax.dot_general(a, b, (((0,), (0,)), ((), ())), preferred_element_type=F32)


def _split3(x):
    hi = x.astype(BF16)
    r1 = x - hi.astype(F32)
    mid = r1.astype(BF16)
    lo = (r1 - mid.astype(F32)).astype(BF16)
    return hi, mid, lo


def _norm_rope_slab(xs, g, cos, sin):
    lane = lax.broadcasted_iota(jnp.int32, xs.shape, 1)
    lo = lane < HALF
    first = (lane & (HALF // 2)) == 0
    sq = xs * xs
    s_lo = jnp.sum(jnp.where(lo, sq, 0.0), axis=-1, keepdims=True)
    s_hi = jnp.sum(jnp.where(lo, 0.0, sq), axis=-1, keepdims=True)
    inv = jnp.where(lo, lax.rsqrt(s_lo * (1.0 / ATT_HEAD_DIM) + EPS),
                    lax.rsqrt(s_hi * (1.0 / ATT_HEAD_DIM) + EPS))
    xn = xs * inv * g
    swapped = jnp.where(first, pltpu.roll(xn, LANES - HALF // 2, 1), pltpu.roll(xn, HALF // 2, 1))
    return xn * cos + swapped * sin


def _log_decay(lrz, wgate_ref, bgate_ref):
    pre = _dot(lrz.astype(BF16), wgate_ref[...]) + bgate_ref[...]
    log_sig = jnp.minimum(pre, 0.0) - jnp.log(1.0 + jnp.exp(-jnp.abs(pre)))
    return log_sig * (1.0 / GLA_GATE_TAU)


def _head_masks(rows):
    lane = lax.broadcasted_iota(jnp.int32, (rows, GLA_KEY_WIDTH), 1)
    return [(lane >= h * GLA_DK) & (lane < (h + 1) * GLA_DK) for h in range(GLA_HEADS)]


def _softmax_block(s, mask, sink_ref):
    probs, rden = [], []
    for h in range(ATT_HEADS):
        sh = jnp.where(mask, s[h * ATT_BLOCK:(h + 1) * ATT_BLOCK], NEG)
        sink = sink_ref[h] * LOG2E
        m = jnp.maximum(jnp.max(sh, axis=-1, keepdims=True), sink)
        e = jnp.exp2(sh - m)
        rden.append(1.0 / (jnp.sum(e, axis=-1, keepdims=True) + jnp.exp2(sink - m)))
        probs.append(e.astype(BF16))
    return jnp.concatenate(probs, axis=0), rden


def _mod_body(cs_ref, cp_ref, w_ref, b_ref, o_ref, act_ref):
    @pl.when(pl.program_id(0) == 0)
    def _():
        cs = cs_ref[...]
        cp = jnp.concatenate([cp_ref[...], jnp.zeros((MOD_ROWS - DEC_BATCH - BATCH, D_MODEL), F32)], axis=0)
        act_ref[0:DEC_BATCH, :] = (cs * _sigmoid(cs)).astype(BF16)
        act_ref[DEC_BATCH:MOD_ROWS, :] = (cp * _sigmoid(cp)).astype(BF16)

    o_ref[...] = _dot(act_ref[...], w_ref[...].astype(BF16)) + b_ref[...]


def _modulation(c_sample, c_prompt, w_ada, b_ada):
    return pl.pallas_call(
        _mod_body,
        grid=(3 * D_MODEL // MOD_TILE,),
        in_specs=[
            pl.BlockSpec((DEC_BATCH, D_MODEL), lambda i: (0, 0)),
            pl.BlockSpec((BATCH, D_MODEL), lambda i: (0, 0)),
            pl.BlockSpec((D_MODEL, MOD_TILE), lambda i: (0, i)),
            pl.BlockSpec((1, MOD_TILE), lambda i: (0, i)),
        ],
        out_specs=pl.BlockSpec((MOD_ROWS, MOD_TILE), lambda i: (0, i)),
        out_shape=jax.ShapeDtypeStruct((MOD_ROWS, 3 * D_MODEL), F32),
        scratch_shapes=[pltpu.VMEM((MOD_ROWS, D_MODEL), BF16)],
        compiler_params=pltpu.CompilerParams(dimension_semantics=("arbitrary",),
                                             vmem_limit_bytes=VMEM_LIMIT),
        name="adaln_mod",
    )(c_sample, c_prompt, w_ada, b_ada)


def _layer_body(xp_ref, xs_ref, modp_ref, mods_ref, cosp_ref, sinp_ref, coss_ref, sins_ref,
                ng_ref, wq_ref, wza_ref, wt_ref, qng_ref, kng_ref, sink_ref,
                wgate_ref, bgate_ref, glang_ref, wba_ref, wbg_ref, wout_ref,
                ck_ref, cv_ref, s0_ref,
                yp_ref, ys_ref, wk_ref, wv_ref, st_ref, ok_ref, ov_ref, s1_ref,
                kprev_ref, vprev_ref, st_scr, klast_scr, vlast_scr):
    TP, TS, C, L, GS = T_PROMPT, T_SAMPLE, GLA_BLOCK, DEC_SEQ, SEQ_GROUP
    n_att, n_gla = TP // ATT_BLOCK, TP // C
    nrow = ATT_HEADS * L
    j = pl.program_id(1)
    last = pl.num_programs(1) - 1

    @pl.when(j == 0)
    def _():
        kprev_ref[...] = jnp.zeros_like(kprev_ref)
        vprev_ref[...] = jnp.zeros_like(vprev_ref)
        st_scr[...] = jnp.zeros_like(st_scr)

    def per_token(a, b):
        return jnp.concatenate([jnp.broadcast_to(mods_ref[u:u + 1, a:b], (L, b - a)) for u in range(GS)], axis=0)

    xp, xs = xp_ref[...], xs_ref[...]
    hp = (xp * lax.rsqrt(jnp.mean(xp * xp, axis=-1, keepdims=True) + EPS) * ng_ref[...]
          * (1.0 + modp_ref[1:2, :]) + modp_ref[0:1, :])
    hs = (xs * lax.rsqrt(jnp.mean(xs * xs, axis=-1, keepdims=True) + EPS) * ng_ref[...]
          * (1.0 + per_token(D_MODEL, 2 * D_MODEL)) + per_token(0, D_MODEL))
    h = jnp.concatenate([hp, hs], axis=0).astype(BF16)

    def proj(a, b):
        return _dot_nt(h, wt_ref[a:b, :])

    cos = jnp.concatenate([cosp_ref[...], coss_ref[...]], axis=0)
    sin = jnp.concatenate([sinp_ref[...], sins_ref[...]], axis=0)
    lane = lax.broadcasted_iota(jnp.int32, (ATT_BLOCK, LANES), 1)
    lo = lane < HALF
    v_ = {}

    def a1_qkv():
        kv = proj(R_K, R_ZA)
        v_["q"], v_["k_raw"], v_["v"] = _dot_nt(h, wq_ref[...]), kv[:, :KV_WIDTH], kv[:, KV_WIDTH:]

    def a2_norm_rope():
        q = v_["q"]
        v_["q_slabs"] = [_norm_rope_slab(q[:, i * LANES:(i + 1) * LANES], qng_ref[...], cos, sin)
                         * ATT_SCALE for i in range(ATT_GROUP)]
        k = _norm_rope_slab(v_["k_raw"], kng_ref[...], cos, sin)
        v = v_["v"]
        klast_scr[...] = k[TP - WINDOW:TP]
        vlast_scr[...] = v[TP - WINDOW:TP]
        v_["k16"], v_["v16"] = k[:TP].astype(BF16), v[:TP].astype(BF16)
        v_["k_s"], v_["v_s"] = k[TP:], v[TP:]

    def g1_decay():
        la = _log_decay(proj(R_LR, R_LR + LANES), wgate_ref, bgate_ref)
        v_["la"], v_["la_s"] = la[:TP], la[TP:]

    def g2_qkv():
        qg = proj(R_QG, R_KG) * (GLA_DK ** -0.5)
        kg = proj(R_KG, R_VG)
        vg = proj(R_VG, R_LR)
        v_["qg"], v_["kg"], v_["vg16"] = qg[:TP], kg[:TP], vg[:TP].astype(BF16)
        v_["qg_s"], v_["kg_s"], v_["vg_s"] = qg[TP:], kg[TP:], vg[TP:]

    def a3_scores():
        k16, v16, q_slabs = v_["k16"], v_["v16"], v_["q_slabs"]
        r = lax.broadcasted_iota(jnp.int32, (ATT_BLOCK, 2 * ATT_BLOCK), 0)
        c = lax.broadcasted_iota(jnp.int32, (ATT_BLOCK, 2 * ATT_BLOCK), 1)
        band = ((c < ATT_BLOCK) & (c >= r)) | ((c >= ATT_BLOCK) & (c - ATT_BLOCK <= r))
        v_["masks"] = [band & ((c >= ATT_BLOCK) | (j > 0))] + [band] * (n_att - 1)
        scores, vcats = [], []
        for i in range(n_att):
            rows = slice(i * ATT_BLOCK, (i + 1) * ATT_BLOCK)
            if i == 0:
                kp, vp = kprev_ref[...], vprev_ref[...]
            else:
                prev = slice((i - 1) * ATT_BLOCK, i * ATT_BLOCK)
                kp, vp = k16[prev], v16[prev]
            pieces = ([jnp.where(lo, s[rows], 0.0).astype(BF16) for s in q_slabs]
                      + [jnp.where(lo, 0.0, s[rows]).astype(BF16) for s in q_slabs])
            scores.append(_dot_nt(jnp.concatenate(pieces, axis=0), jnp.concatenate([kp, k16[rows]], axis=0)))
            vcats.append(jnp.concatenate([vp, v16[rows]], axis=0))
        kprev_ref[...] = k16[TP - ATT_BLOCK:]
        vprev_ref[...] = v16[TP - ATT_BLOCK:]
        v_["scores"], v_["vcats"] = scores, vcats

    v_["soft"] = [None] * n_att

    def softmax_stage(i):
        def run():
            v_["soft"][i] = _softmax_block(v_["scores"][i], v_["masks"][i], sink_ref)
        return run

    a4 = [softmax_stage(i) for i in range(n_att)]

    def a5_values():
        o_rows = []
        for i in range(n_att):
            probs, rden = v_["soft"][i]
            o = _dot(probs, v_["vcats"][i])
            o_rows.append(jnp.concatenate(
                [jnp.where(lo, o[g * ATT_BLOCK:(g + 1) * ATT_BLOCK] * rden[g],
                           o[(g + ATT_GROUP) * ATT_BLOCK:(g + ATT_GROUP + 1) * ATT_BLOCK] * rden[g + ATT_GROUP])
                 for g in range(ATT_GROUP)], axis=1))
        v_["o_a"] = jnp.concatenate(o_rows, axis=0)

    def g3_cumsum():
        tr = lax.broadcasted_iota(jnp.int32, (TP, TP), 0)
        tc = lax.broadcasted_iota(jnp.int32, (TP, TP), 1)
        tri = (((tr // C) == (tc // C)) & (tc <= tr)).astype(BF16)
        hi, mid, lo3 = _split3(v_["la"])
        b = _dot(tri, hi) + _dot(tri, mid) + _dot(tri, lo3)
        v_["b"] = b
        v_["b_last"] = [b[(i + 1) * C - 1:(i + 1) * C] for i in range(n_gla)]
        v_["b_mid"] = [b[i * C + C // 2 - 1:i * C + C // 2] for i in range(n_gla)]

    def g4_decayed():
        b, qg, kg = v_["b"], v_["qg"], v_["kg"]
        b_end = jnp.concatenate([jnp.broadcast_to(bl, (C, GLA_KEY_WIDTH)) for bl in v_["b_last"]], axis=0)
        b_ref = jnp.concatenate([jnp.broadcast_to(bm, (C, GLA_KEY_WIDTH)) for bm in v_["b_mid"]], axis=0)
        q_hat = qg * jnp.exp(b - b_ref)
        v_["k_til"] = (kg * jnp.exp(b_ref - b)).astype(BF16)
        k_dec = kg * jnp.exp(b_end - b)
        hmasks = _head_masks(TP)
        v_["q_m"] = [jnp.where(hm, q_hat, 0.0).astype(BF16) for hm in hmasks]
        v_["k_m"] = [jnp.where(hm, k_dec, 0.0).astype(BF16) for hm in hmasks]

    def g5_states():
        vg16 = v_["vg16"]
        states = [st_scr[...]]
        for i in range(n_gla):
            rows = slice(i * C, (i + 1) * C)
            v_stack = jnp.concatenate([vg16[rows, hh * GLA_DV:(hh + 1) * GLA_DV] for hh in range(GLA_HEADS)],
                                      axis=0)
            k_stack = jnp.concatenate([km[rows] for km in v_["k_m"]], axis=0)
            states.append(states[-1] * jnp.exp(v_["b_last"][i]) + _dot_tn(v_stack, k_stack))
        st_scr[...] = states[-1]
        v_["states"] = states

    def g6_mixed():
        mixed = []
        for i in range(n_gla):
            rows = slice(i * C, (i + 1) * C)
            q_stack = jnp.concatenate([qm[rows] for qm in v_["q_m"]], axis=0)
            s_ref = v_["states"][i] * jnp.exp(v_["b_mid"][i])
            rhs = jnp.concatenate([s_ref.astype(BF16), v_["k_til"][rows]], axis=0)
            mixed.append(_dot_nt(q_stack, rhs))
        v_["mixed"] = mixed

    def g7_outputs():
        sr = lax.broadcasted_iota(jnp.int32, (GLA_HEADS * C, C), 0)
        sc = lax.broadcasted_iota(jnp.int32, (GLA_HEADS * C, C), 1)
        causal = (sr & (C - 1)) >= sc
        vg16 = v_["vg16"]
        og_rows = []
        for i in range(n_gla):
            rows = slice(i * C, (i + 1) * C)
            m_i = v_["mixed"][i]
            a = jnp.where(causal, m_i[:, GLA_DV:], 0.0).astype(BF16)
            og_rows.append(jnp.concatenate(
                [m_i[hh * C:(hh + 1) * C, :GLA_DV]
                 + _dot(a[hh * C:(hh + 1) * C], vg16[rows, hh * GLA_DV:(hh + 1) * GLA_DV])
                 for hh in range(GLA_HEADS)], axis=1))
        v_["o_g"] = jnp.concatenate(og_rows, axis=0)

    seqs = range(GS)

    def s1_inputs():
        lo_s = lax.broadcasted_iota(jnp.int32, (L, LANES), 1) < HALF
        qrows = []
        for u in seqs:
            rows = slice(TP + u * L, TP + (u + 1) * L)
            pieces = ([jnp.where(lo_s, s[rows], 0.0) for s in v_["q_slabs"]]
                      + [jnp.where(lo_s, 0.0, s[rows]) for s in v_["q_slabs"]])
            qrows.append(jnp.concatenate(pieces, axis=0).astype(BF16))
        pad = jnp.zeros((WINDOW - TS, KV_WIDTH), F32)
        k_pad = jnp.concatenate([v_["k_s"], pad], axis=0)
        v_pad = jnp.concatenate([v_["v_s"], pad], axis=0)
        v_["qrows"], v_["k_pad_t"], v_["v_pad_t"], v_["v_pad"] = qrows, k_pad.T, v_pad.T, v_pad
        v_["snew"] = _dot(jnp.concatenate(qrows, axis=0), v_["k_pad_t"].astype(BF16))

    def s2_windows():
        keep_old = lax.broadcasted_iota(jnp.int32, (KV_WIDTH, WINDOW), 1) < WINDOW - L
        for u in seqs:
            shift = WINDOW - L - u * L
            ok_ref[u] = jnp.where(keep_old, pltpu.roll(ck_ref[u], WINDOW - L, 1),
                                  pltpu.roll(v_["k_pad_t"], shift, 1))
            ov_ref[u] = jnp.where(keep_old, pltpu.roll(cv_ref[u], WINDOW - L, 1),
                                  pltpu.roll(v_["v_pad_t"], shift, 1))

    def s3_scores():
        rr = lax.broadcasted_iota(jnp.int32, (nrow, LANES), 0) & (L - 1)
        cc = lax.broadcasted_iota(jnp.int32, (nrow, LANES), 1)
        v_["s_c"] = [jnp.where(cc >= rr, _dot(v_["qrows"][u], ck_ref[u].astype(BF16)), NEG) for u in seqs]
        v_["s_n"] = [jnp.where((cc >= u * L) & (cc <= u * L + rr), v_["snew"][u * nrow:(u + 1) * nrow], NEG)
                     for u in seqs]

    def s4_softmax():
        sink_col = jnp.concatenate([jnp.full((L, 1), sink_ref[hh] * LOG2E, F32) for hh in range(ATT_HEADS)],
                                   axis=0)
        e_c, e_n, rden = [], [], []
        for a, b_ in zip(v_["s_c"], v_["s_n"]):
            m = jnp.maximum(jnp.maximum(jnp.max(a, axis=-1, keepdims=True), jnp.max(b_, axis=-1, keepdims=True)),
                            sink_col)
            ec, en = jnp.exp2(a - m), jnp.exp2(b_ - m)
            rden.append(1.0 / (jnp.sum(ec, axis=-1, keepdims=True) + jnp.sum(en, axis=-1, keepdims=True)
                               + jnp.exp2(sink_col - m)))
            e_c.append(ec.astype(BF16))
            e_n.append(en.astype(BF16))
        v_["e_c"], v_["e_n"], v_["rden_s"] = e_c, e_n, rden

    def s5_values():
        lo_s = lax.broadcasted_iota(jnp.int32, (L, LANES), 1) < HALF
        o_new = _dot(jnp.concatenate(v_["e_n"], axis=0), v_["v_pad"].astype(BF16))
        rows_out = []
        for u in seqs:
            o = ((_dot_nt(v_["e_c"][u], cv_ref[u].astype(BF16)) + o_new[u * nrow:(u + 1) * nrow])
                 * v_["rden_s"][u])
            rows_out.append(jnp.concatenate(
                [jnp.where(lo_s, o[i * L:(i + 1) * L], o[(i + ATT_GROUP) * L:(i + ATT_GROUP + 1) * L])
                 for i in range(ATT_GROUP)], axis=1))
        v_["o_a_s"] = jnp.concatenate(rows_out, axis=0)

    def s6_gla():
        qg, kg, vg, la = v_["qg_s"], v_["kg_s"], v_["vg_s"], v_["la_s"]
        tr = lax.broadcasted_iota(jnp.int32, (TS, TS), 0)
        tc = lax.broadcasted_iota(jnp.int32, (TS, TS), 1)
        same_seq = (tr // L) == (tc // L)
        causal = same_seq & (tc <= tr)
        parts = _split3(la)
        tri, blk = causal.astype(BF16), same_seq.astype(BF16)
        b = _dot(tri, parts[0]) + _dot(tri, parts[1]) + _dot(tri, parts[2])
        b_last = _dot(blk, parts[0]) + _dot(blk, parts[1]) + _dot(blk, parts[2])
        q_hat = qg * jnp.exp(b)
        k_til = (kg * jnp.exp(-b)).astype(BF16)
        k_dec = kg * jnp.exp(b_last - b)
        hmasks = _head_masks(TS)
        q_m = [jnp.where(hm, q_hat, 0.0) for hm in hmasks]
        decay = jnp.exp(b_last)
        zk = jnp.zeros((L, GLA_DK), F32)
        zv = jnp.zeros((L, GLA_DV), F32)
        st0 = [s0_ref[u] for u in seqs]
        inter, upds, dcols = [], [], []
        for u in seqs:
            rows = slice(u * L, (u + 1) * L)
            q_stack = jnp.concatenate([qm[rows] for qm in q_m], axis=0).astype(BF16)
            inter.append(_dot(q_stack, st0[u].astype(BF16)))
            per_head = []
            for hh in range(GLA_HEADS):
                k_rows = jnp.concatenate([k_dec[rows, hh * GLA_DK:(hh + 1) * GLA_DK], zk], axis=0).astype(BF16)
                v_rows = jnp.concatenate([vg[rows, hh * GLA_DV:(hh + 1) * GLA_DV], zv], axis=0).astype(BF16)
                per_head.append(_dot_tn(k_rows, v_rows))
            upds.append(jnp.concatenate(per_head, axis=0))
            dcols.append(decay[rows].T[:, 0:1])
        intra = []
        for hh in range(GLA_HEADS):
            a = jnp.where(causal, _dot_nt(q_m[hh].astype(BF16), k_til), 0.0).astype(BF16)
            intra.append(_dot(a, vg[:, hh * GLA_DV:(hh + 1) * GLA_DV].astype(BF16)))
        for u in seqs:
            s1_ref[u] = dcols[u] * st0[u] + upds[u]
        o_inter = jnp.concatenate(
            [jnp.concatenate([inter[u][hh * L:(hh + 1) * L] for hh in range(GLA_HEADS)], axis=1) for u in seqs],
            axis=0)
        v_["o_g_s"] = o_inter + jnp.concatenate(intra, axis=1)

    gate_parts = {"silu_a": [], "silu_g": [], "merge_a": [], "merge_g": []}

    def gate_chunk(key, w_ref, row, silu):
        def run():
            z = _dot_nt(h, w_ref[row:row + MXU_TILE, :])
            s = _sigmoid(z)
            gate_parts[key].append(z * s if silu else s)
        return run

    t = ([gate_chunk("silu_a", wza_ref, r, True) for r in range(0, ATT_WIDTH, MXU_TILE)]
         + [gate_chunk("silu_g", wt_ref, R_ZG + r, True) for r in range(0, GLA_WIDTH, MXU_TILE)]
         + [gate_chunk("merge_a", wt_ref, R_MA + r, False) for r in range(0, D_MODEL, MXU_TILE)]
         + [gate_chunk("merge_g", wt_ref, R_MG + r, False) for r in range(0, D_MODEL, MXU_TILE)])

    def t_tail():
        silu_a, silu_g, merge_a, merge_g = (jnp.concatenate(gate_parts[key], axis=1)
                                            for key in ("silu_a", "silu_g", "merge_a", "merge_g"))
        o_a = jnp.concatenate([v_["o_a"], v_["o_a_s"]], axis=0) * silu_a
        o_g = jnp.concatenate([v_["o_g"], v_["o_g_s"]], axis=0)
        slabs = []
        for hh in range(GLA_HEADS):
            oh = o_g[:, hh * GLA_DV:(hh + 1) * GLA_DV]
            slabs.append(oh * lax.rsqrt(jnp.mean(oh * oh, axis=-1, keepdims=True) + EPS) * glang_ref[...])
        o_g = jnp.concatenate(slabs, axis=1) * silu_g
        merged = (merge_a * _dot(o_a.astype(BF16), wba_ref[...])
                  + merge_g * _dot(o_g.astype(BF16), wbg_ref[...]))
        out = _dot(merged.astype(BF16), wout_ref[...])
        yp_ref[...] = xp + modp_ref[2:3, :] * out[:TP]
        ys_ref[...] = xs + per_token(2 * D_MODEL, 3 * D_MODEL) * out[TP:]

    for stage in (a1_qkv, g1_decay, t[0], g2_qkv, t[1], a2_norm_rope, t[2], s1_inputs, g3_cumsum, t[3],
                  a3_scores, s2_windows, t[4], g4_decayed, s3_scores, a4[0], t[5], a4[1], s4_softmax, t[6], a4[2],
                  t[7], a4[3], g5_states, s5_values, t[8], a5_values, s6_gla, t[9], g6_mixed, t[10], g7_outputs,
                  t[11], t_tail):
        stage()

    @pl.when(j == last)
    def _():
        wk_ref[...] = klast_scr[...].T
        wv_ref[...] = vlast_scr[...].T
        st_ref[...] = st_scr[...].T


def _fused_layer(xp, xs, mod_p, mod_s, cos_p, sin_p, cos_s, sin_s, weights, ck, cv, s0):
    GS, TS = SEQ_GROUP, T_SAMPLE

    def const(shape):
        zeros = (0,) * len(shape)
        return pl.BlockSpec(shape, lambda b, j: zeros)

    def per_batch(shape):
        return pl.BlockSpec((None,) + shape, lambda b, j: (b, 0, 0))

    def per_step(shape):
        tail = (0,) * (len(shape) - 1)
        return pl.BlockSpec(shape, lambda b, j: (b * N_BLOCKS + j,) + tail)

    prompt_rows = pl.BlockSpec((None, T_PROMPT, D_MODEL), lambda b, j: (b, j, 0))
    table_rows = pl.BlockSpec((T_PROMPT, LANES), lambda b, j: (j, 0))
    weight_specs = [
        const((1, D_MODEL)),
        const((ATT_WIDTH, D_MODEL)),
        const((ATT_WIDTH, D_MODEL)),
        const((R_END, D_MODEL)),
        const((1, LANES)),
        const((1, LANES)),
        pl.BlockSpec(memory_space=pltpu.SMEM),
        const((LANES, GLA_KEY_WIDTH)),
        const((1, GLA_KEY_WIDTH)),
        const((1, GLA_DV)),
        const((ATT_WIDTH, D_MODEL)),
        const((GLA_WIDTH, D_MODEL)),
        const((D_MODEL, D_MODEL)),
    ]
    return pl.pallas_call(
        _layer_body,
        grid=(BATCH, N_BLOCKS),
        in_specs=[prompt_rows, per_step((TS, D_MODEL)), per_batch((3, D_MODEL)),
                  pl.BlockSpec((None, GS, 3 * D_MODEL), lambda b, j: (b * N_BLOCKS + j, 0, 0)),
                  table_rows, table_rows, const((TS, LANES)), const((TS, LANES))]
                 + weight_specs
                 + [per_step((GS, KV_WIDTH, WINDOW)), per_step((GS, KV_WIDTH, WINDOW)),
                    per_step((GS, GLA_KEY_WIDTH, GLA_DV))],
        out_specs=[prompt_rows, per_step((TS, D_MODEL)),
                   per_batch((KV_WIDTH, WINDOW)), per_batch((KV_WIDTH, WINDOW)), per_batch((GLA_KEY_WIDTH, GLA_DV)),
                   per_step((GS, KV_WIDTH, WINDOW)), per_step((GS, KV_WIDTH, WINDOW)),
                   per_step((GS, GLA_KEY_WIDTH, GLA_DV))],
        out_shape=[
            jax.ShapeDtypeStruct((BATCH, SEQ, D_MODEL), F32),
            jax.ShapeDtypeStruct((DEC_BATCH * DEC_SEQ, D_MODEL), F32),
            jax.ShapeDtypeStruct((BATCH, KV_WIDTH, WINDOW), F32),
            jax.ShapeDtypeStruct((BATCH, KV_WIDTH, WINDOW), F32),
            jax.ShapeDtypeStruct((BATCH, GLA_KEY_WIDTH, GLA_DV), F32),
            jax.ShapeDtypeStruct((DEC_BATCH, KV_WIDTH, WINDOW), F32),
            jax.ShapeDtypeStruct((DEC_BATCH, KV_WIDTH, WINDOW), F32),
            jax.ShapeDtypeStruct((DEC_BATCH, GLA_KEY_WIDTH, GLA_DV), F32),
        ],
        scratch_shapes=[
            pltpu.VMEM((ATT_BLOCK, KV_WIDTH), BF16),
            pltpu.VMEM((ATT_BLOCK, KV_WIDTH), BF16),
            pltpu.VMEM((GLA_DV, GLA_KEY_WIDTH), F32),
            pltpu.VMEM((WINDOW, KV_WIDTH), F32),
            pltpu.VMEM((WINDOW, KV_WIDTH), F32),
        ],
        compiler_params=pltpu.CompilerParams(dimension_semantics=("arbitrary", "arbitrary"),
                                             vmem_limit_bytes=VMEM_LIMIT),
        name="fused_layer",
    )(xp, xs, mod_p, mod_s, cos_p, sin_p, cos_s, sin_s, *weights, ck, cv, s0)


def _rope_tables(pos):
    half = ATT_HEAD_DIM // 2
    inv = 1.0 / (ROPE_THETA ** (np.arange(half, dtype=np.float64) / half))
    ang = pos.astype(np.float64)[:, None] * inv[None, :]
    c, s = np.cos(ang), np.sin(ang)
    return (np.tile(c, (1, 4)).astype(np.float32),
            np.concatenate([-s, s, -s, s], axis=1).astype(np.float32))


def _pair_heads(w):
    shape = w.shape
    w = w.reshape((ATT_KV_HEADS, ATT_GROUP, ATT_HEAD_DIM) + shape[1:])
    return jnp.swapaxes(w, 0, 1).reshape(shape)


def kernel(x_prompt, x_sample, cache_win_k, cache_win_v, state_gla, c_prompt, c_sample, norm_g, w_ada, b_ada, w_in, q_norm_g, k_norm_g, attn_sinks, w_gla_gate, b_gla_gate, gla_norm_g, w_branch_att, w_branch_gla, w_out):
    assert w_in.shape == (1, D_MODEL, R_END), "single-layer trunk"
    wt = jnp.swapaxes(w_in[0], 0, 1)
    wgate = jnp.concatenate([w_gla_gate[0], jnp.zeros((LANES - GLA_GATE_RANK, GLA_KEY_WIDTH), F32)],
                            axis=0).astype(BF16)
    weights = (
        norm_g[0][None, :],
        _pair_heads(wt[0:R_K]).astype(BF16), _pair_heads(wt[R_ZA:R_QG]).astype(BF16), wt.astype(BF16),
        jnp.tile(q_norm_g[0], 2)[None, :], jnp.tile(k_norm_g[0], 2)[None, :],
        attn_sinks[0], wgate, b_gla_gate[0][None, :], gla_norm_g[0][None, :],
        _pair_heads(w_branch_att[0]).astype(BF16), w_branch_gla[0].astype(BF16), w_out[0].astype(BF16),
    )

    mod = _modulation(c_sample, c_prompt, w_ada[0], b_ada[0][None, :])
    mod_p = mod[DEC_BATCH:DEC_BATCH + BATCH].reshape(BATCH, 3, D_MODEL)
    mod_s = mod[:DEC_BATCH].reshape(DEC_BATCH // SEQ_GROUP, SEQ_GROUP, 3 * D_MODEL)

    cos_p, sin_p = _rope_tables(np.arange(SEQ))
    cos_s, sin_s = _rope_tables(PAST_LEN + np.arange(DEC_SEQ))
    cos_s, sin_s = np.tile(cos_s, (SEQ_GROUP, 1)), np.tile(sin_s, (SEQ_GROUP, 1))

    def kv_rows(c):
        return jnp.transpose(c, (0, 2, 3, 1)).reshape(c.shape[0], KV_WIDTH, WINDOW)

    def kv_out(c):
        return jnp.transpose(c.reshape(c.shape[0], ATT_KV_HEADS, ATT_HEAD_DIM, WINDOW), (0, 3, 1, 2))[None]

    y_p, y_s, wk_p, wv_p, st_p, wk_s, wv_s, st_s = _fused_layer(
        x_prompt, x_sample.reshape(DEC_BATCH * DEC_SEQ, D_MODEL), mod_p, mod_s, cos_p, sin_p, cos_s, sin_s, weights,
        kv_rows(cache_win_k[0]), kv_rows(cache_win_v[0]),
        state_gla[0].reshape(DEC_BATCH, GLA_KEY_WIDTH, GLA_DV))

    st_shape = (1, -1, GLA_HEADS, GLA_DK, GLA_DV)
    return (y_p, y_s.reshape(DEC_BATCH, DEC_SEQ, D_MODEL),
            kv_out(wk_p), kv_out(wv_p), st_p.reshape(st_shape),
            kv_out(wk_s), kv_out(wv_s), st_s.reshape(st_shape))
```

```python
import numpy as np

import jax
import jax.numpy as jnp
from jax import lax
from jax.experimental import pallas as pl
from jax.experimental.pallas import tpu as pltpu

F32 = jnp.float32
BF16 = jnp.bfloat16

D_MODEL = 1024
BATCH = 2
SEQ = 8192
DEC_BATCH = 128
DEC_SEQ = 8
PAST_LEN = 16384
ATT_HEADS = 8
ATT_KV_HEADS = 2
ATT_GROUP = ATT_HEADS // ATT_KV_HEADS
ATT_HEAD_DIM = 64
ATT_WIDTH = ATT_HEADS * ATT_HEAD_DIM
KV_WIDTH = ATT_KV_HEADS * ATT_HEAD_DIM
WINDOW = 128
ATT_BLOCK = 128
ROPE_THETA = 10000.0
GLA_HEADS = 4
GLA_WIDTH = D_MODEL // 2
GLA_DV = GLA_WIDTH // GLA_HEADS
GLA_KEY_WIDTH = GLA_WIDTH // 2
GLA_DK = GLA_KEY_WIDTH // GLA_HEADS
GLA_GATE_RANK = 16
GLA_GATE_TAU = 16.0
EPS = 1e-6
NEG = -1e30
LOG2E = 1.4426950408889634
ATT_SCALE = ATT_HEAD_DIM ** -0.5 * LOG2E

LANES = 128
HALF = LANES // 2
MXU_TILE = 256
R_K, R_V, R_ZA = ATT_WIDTH, ATT_WIDTH + KV_WIDTH, ATT_WIDTH + 2 * KV_WIDTH
R_QG = R_ZA + ATT_WIDTH
R_KG = R_QG + GLA_KEY_WIDTH
R_VG = R_KG + GLA_KEY_WIDTH
R_LR = R_VG + GLA_WIDTH
R_ZG = R_LR + GLA_GATE_RANK
R_MA = R_ZG + GLA_WIDTH
R_MG = R_MA + D_MODEL
R_END = R_MG + D_MODEL

T_PROMPT = 512
N_BLOCKS = SEQ // T_PROMPT
GLA_BLOCK = 64
SEQ_GROUP = DEC_BATCH // (BATCH * N_BLOCKS)
T_SAMPLE = SEQ_GROUP * DEC_SEQ
MOD_ROWS = DEC_BATCH + 16
MOD_TILE = 1536
MOD_BLOCK = 8
VMEM_LIMIT = 56 * 1024 * 1024


def _sigmoid(x):
    return 1.0 / (1.0 + jnp.exp2(x * -LOG2E))


def _dot(a, b):
    return jnp.dot(a, b, preferred_element_type=F32)


def _dot_nt(a, b):
    return lax.dot_general(a, b, (((1,), (1,)), ((), ())), preferred_element_type=F32)


def _dot_tn(a, b):
    return lax.dot_general(a, b, (((0,), (0,)), ((), ())), preferred_element_type=F32)


def _split3(x):
    hi = x.astype(BF16)
    r1 = x - hi.astype(F32)
    mid = r1.astype(BF16)
    lo = (r1 - mid.astype(F32)).astype(BF16)
    return hi, mid, lo


def _norm_rope_slab(xs, g, cos, sin):
    lane = lax.broadcasted_iota(jnp.int32, xs.shape, 1)
    lo = lane < HALF
    first = (lane & (HALF // 2)) == 0
    sq = xs * xs
    s_lo = jnp.sum(jnp.where(lo, sq, 0.0), axis=-1, keepdims=True)
    s_hi = jnp.sum(jnp.where(lo, 0.0, sq), axis=-1, keepdims=True)
    inv = jnp.where(lo, lax.rsqrt(s_lo * (1.0 / ATT_HEAD_DIM) + EPS),
                    lax.rsqrt(s_hi * (1.0 / ATT_HEAD_DIM) + EPS))
    xn = xs * inv * g
    swapped = jnp.where(first, pltpu.roll(xn, LANES - HALF // 2, 1), pltpu.roll(xn, HALF // 2, 1))
    return xn * cos + swapped * sin


def _log_decay(lrz, wgate_ref, bgate_ref):
    pre = _dot(lrz.astype(BF16), wgate_ref[...]) + bgate_ref[...]
    log_sig = jnp.minimum(pre, 0.0) - jnp.log(1.0 + jnp.exp(-jnp.abs(pre)))
    return log_sig * (1.0 / GLA_GATE_TAU)


def _head_masks(rows):
    lane = lax.broadcasted_iota(jnp.int32, (rows, GLA_KEY_WIDTH), 1)
    return [(lane >= h * GLA_DK) & (lane < (h + 1) * GLA_DK) for h in range(GLA_HEADS)]


def _softmax_block(s, mask, sink_ref):
    probs, rden = [], []
    for h in range(ATT_HEADS):
        sh = jnp.where(mask, s[h * ATT_BLOCK:(h + 1) * ATT_BLOCK], NEG)
        sink = sink_ref[h] * LOG2E
        m = jnp.maximum(jnp.max(sh, axis=-1, keepdims=True), sink)
        e = jnp.exp2(sh - m)
        rden.append(1.0 / (jnp.sum(e, axis=-1, keepdims=True) + jnp.exp2(sink - m)))
        probs.append(e.astype(BF16))
    return jnp.concatenate(probs, axis=0), rden


def _paired_rows(lo_ref, hi_ref, hi_base):
    rows = []
    for g in range(ATT_GROUP):
        rows.append(lo_ref[g * ATT_HEAD_DIM:(g + 1) * ATT_HEAD_DIM, :])
        rows.append(hi_ref[hi_base + g * ATT_HEAD_DIM:hi_base + (g + 1) * ATT_HEAD_DIM, :])
    return jnp.concatenate(rows, axis=0).astype(BF16)


def _prep_body(cs_ref, cp_ref, w_ref, b_ref, wq_in, wza_lo, wza_hi, wba_in, wbg_in, wout_in,
               mod_ref, wq_out, wza_out, wba_out, wbg_out, wout_out, act_ref):
    half = ATT_WIDTH // 2

    @pl.when(pl.program_id(0) == 0)
    def _():
        cs = cs_ref[...]
        cp = jnp.concatenate([cp_ref[...], jnp.zeros((MOD_ROWS - DEC_BATCH - BATCH, D_MODEL), F32)], axis=0)
        act_ref[0:DEC_BATCH, :] = (cs * _sigmoid(cs)).astype(BF16)
        act_ref[DEC_BATCH:MOD_ROWS, :] = (cp * _sigmoid(cp)).astype(BF16)
        wq_out[...] = _paired_rows(wq_in, wq_in, half)
        wza_out[...] = _paired_rows(wza_lo, wza_hi, 0)
        wba_out[...] = _paired_rows(wba_in, wba_in, half)
        wbg_out[...] = wbg_in[...].astype(BF16)
        wout_out[...] = wout_in[...].astype(BF16)

    mod_ref[...] = _dot(act_ref[...], w_ref[...].astype(BF16)) + b_ref[...]


def _prepare(c_sample, c_prompt, w_ada, b_ada, wt, w_branch_att, w_branch_gla, w_out):
    half = ATT_WIDTH // 2
    whole = lambda shape: pl.BlockSpec(shape, lambda i: (0,) * len(shape))
    return pl.pallas_call(
        _prep_body,
        grid=(3 * D_MODEL // MOD_TILE,),
        in_specs=[
            whole((DEC_BATCH, D_MODEL)),
            whole((BATCH, D_MODEL)),
            pl.BlockSpec((D_MODEL, MOD_TILE), lambda i: (0, i)),
            pl.BlockSpec((1, MOD_TILE), lambda i: (0, i)),
            pl.BlockSpec((ATT_WIDTH, D_MODEL), lambda i: (0, 0)),
            pl.BlockSpec((half, D_MODEL), lambda i: (R_ZA // half, 0)),
            pl.BlockSpec((half, D_MODEL), lambda i: (R_ZA // half + 1, 0)),
            whole((ATT_WIDTH, D_MODEL)), whole((GLA_WIDTH, D_MODEL)), whole((D_MODEL, D_MODEL)),
        ],
        out_specs=[pl.BlockSpec((MOD_ROWS, MOD_TILE), lambda i: (0, i)),
                   whole((ATT_WIDTH, D_MODEL)), whole((ATT_WIDTH, D_MODEL)), whole((ATT_WIDTH, D_MODEL)),
                   whole((GLA_WIDTH, D_MODEL)), whole((D_MODEL, D_MODEL))],
        out_shape=[jax.ShapeDtypeStruct((MOD_ROWS, 3 * D_MODEL), F32),
                   jax.ShapeDtypeStruct((ATT_WIDTH, D_MODEL), BF16), jax.ShapeDtypeStruct((ATT_WIDTH, D_MODEL), BF16),
                   jax.ShapeDtypeStruct((ATT_WIDTH, D_MODEL), BF16), jax.ShapeDtypeStruct((GLA_WIDTH, D_MODEL), BF16),
                   jax.ShapeDtypeStruct((D_MODEL, D_MODEL), BF16)],
        scratch_shapes=[pltpu.VMEM((MOD_ROWS, D_MODEL), BF16)],
        compiler_params=pltpu.CompilerParams(dimension_semantics=("arbitrary",),
                                             vmem_limit_bytes=VMEM_LIMIT),
        name="prepare",
    )(c_sample, c_prompt, w_ada, b_ada, wt, wt, wt, w_branch_att, w_branch_gla, w_out)


def _layer_body(xp_ref, xs_ref, modp_ref, mods_ref, cosp_ref, sinp_ref, coss_ref, sins_ref,
                ng_ref, wq_ref, wza_ref, wt_ref, qng_ref, kng_ref, sink_ref,
                wgate_ref, bgate_ref, glang_ref, wba_ref, wbg_ref, wout_ref,
                ck_ref, cv_ref, s0_ref,
                yp_ref, ys_ref, wk_ref, wv_ref, st_ref, ok_ref, ov_ref, s1_ref,
                kprev_ref, vprev_ref, st_scr, klast_scr, vlast_scr):
    TP, TS, C, L, GS = T_PROMPT, T_SAMPLE, GLA_BLOCK, DEC_SEQ, SEQ_GROUP
    n_att, n_gla = TP // ATT_BLOCK, TP // C
    nrow = ATT_HEADS * L
    j = pl.program_id(1)
    last = pl.num_programs(1) - 1

    @pl.when(j == 0)
    def _():
        kprev_ref[...] = jnp.zeros_like(kprev_ref)
        vprev_ref[...] = jnp.zeros_like(vprev_ref)
        st_scr[...] = jnp.zeros_like(st_scr)

    row_s = ((pl.program_id(0) * pl.num_programs(1) + j) % (MOD_BLOCK // GS)) * GS
    row_p = pl.program_id(0)

    def per_token(a, b):
        return jnp.concatenate([jnp.broadcast_to(mods_ref[pl.ds(row_s + u, 1), a:b], (L, b - a)) for u in range(GS)],
                               axis=0)

    def prompt_mod(k):
        return modp_ref[pl.ds(row_p, 1), k * D_MODEL:(k + 1) * D_MODEL]

    xp, xs = xp_ref[...], xs_ref[...]
    hp = (xp * lax.rsqrt(jnp.mean(xp * xp, axis=-1, keepdims=True) + EPS) * ng_ref[...]
          * (1.0 + prompt_mod(1)) + prompt_mod(0))
    hs = (xs * lax.rsqrt(jnp.mean(xs * xs, axis=-1, keepdims=True) + EPS) * ng_ref[...]
          * (1.0 + per_token(D_MODEL, 2 * D_MODEL)) + per_token(0, D_MODEL))
    h = jnp.concatenate([hp, hs], axis=0).astype(BF16)

    def proj(a, b):
        return _dot_nt(h, wt_ref[a:b, :])

    cos = jnp.concatenate([cosp_ref[...], coss_ref[...]], axis=0)
    sin = jnp.concatenate([sinp_ref[...], sins_ref[...]], axis=0)
    lane = lax.broadcasted_iota(jnp.int32, (ATT_BLOCK, LANES), 1)
    lo = lane < HALF
    v_ = {}

    def a1_qkv():
        kv = proj(R_K, R_ZA)
        v_["q"], v_["k_raw"], v_["v"] = _dot_nt(h, wq_ref[...]), kv[:, :KV_WIDTH], kv[:, KV_WIDTH:]

    def a2_norm_rope():
        q = v_["q"]
        v_["q_slabs"] = [_norm_rope_slab(q[:, i * LANES:(i + 1) * LANES], qng_ref[...], cos, sin)
                         * ATT_SCALE for i in range(ATT_GROUP)]
        k = _norm_rope_slab(v_["k_raw"], kng_ref[...], cos, sin)
        v = v_["v"]
        klast_scr[...] = k[TP - WINDOW:TP]
        vlast_scr[...] = v[TP - WINDOW:TP]
        v_["k16"], v_["v16"] = k[:TP].astype(BF16), v[:TP].astype(BF16)
        v_["k_s"], v_["v_s"] = k[TP:], v[TP:]

    def g1_decay():
        la = _log_decay(proj(R_LR, R_LR + LANES), wgate_ref, bgate_ref)
        v_["la"], v_["la_s"] = la[:TP], la[TP:]

    def g2_qkv():
        qg = proj(R_QG, R_KG) * (GLA_DK ** -0.5)
        kg = proj(R_KG, R_VG)
        vg = proj(R_VG, R_LR)
        v_["qg"], v_["kg"], v_["vg16"] = qg[:TP], kg[:TP], vg[:TP].astype(BF16)
        v_["qg_s"], v_["kg_s"], v_["vg_s"] = qg[TP:], kg[TP:], vg[TP:]

    def a3_scores():
        k16, v16, q_slabs = v_["k16"], v_["v16"], v_["q_slabs"]
        r = lax.broadcasted_iota(jnp.int32, (ATT_BLOCK, 2 * ATT_BLOCK), 0)
        c = lax.broadcasted_iota(jnp.int32, (ATT_BLOCK, 2 * ATT_BLOCK), 1)
        band = ((c < ATT_BLOCK) & (c >= r)) | ((c >= ATT_BLOCK) & (c - ATT_BLOCK <= r))
        v_["masks"] = [band & ((c >= ATT_BLOCK) | (j > 0))] + [band] * (n_att - 1)
        scores, vcats = [], []
        for i in range(n_att):
            rows = slice(i * ATT_BLOCK, (i + 1) * ATT_BLOCK)
            if i == 0:
                kp, vp = kprev_ref[...], vprev_ref[...]
            else:
                prev = slice((i - 1) * ATT_BLOCK, i * ATT_BLOCK)
                kp, vp = k16[prev], v16[prev]
            pieces = ([jnp.where(lo, s[rows], 0.0).astype(BF16) for s in q_slabs]
                      + [jnp.where(lo, 0.0, s[rows]).astype(BF16) for s in q_slabs])
            scores.append(_dot_nt(jnp.concatenate(pieces, axis=0), jnp.concatenate([kp, k16[rows]], axis=0)))
            vcats.append(jnp.concatenate([vp, v16[rows]], axis=0))
        kprev_ref[...] = k16[TP - ATT_BLOCK:]
        vprev_ref[...] = v16[TP - ATT_BLOCK:]
        v_["scores"], v_["vcats"] = scores, vcats

    v_["soft"] = [None] * n_att

    def softmax_stage(i):
        def run():
            v_["soft"][i] = _softmax_block(v_["scores"][i], v_["masks"][i], sink_ref)
        return run

    a4 = [softmax_stage(i) for i in range(n_att)]

    def a5_values():
        o_rows = []
        for i in range(n_att):
            probs, rden = v_["soft"][i]
            o = _dot(probs, v_["vcats"][i])
            o_rows.append(jnp.concatenate(
                [jnp.where(lo, o[g * ATT_BLOCK:(g + 1) * ATT_BLOCK] * rden[g],
                           o[(g + ATT_GROUP) * ATT_BLOCK:(g + ATT_GROUP + 1) * ATT_BLOCK] * rden[g + ATT_GROUP])
                 for g in range(ATT_GROUP)], axis=1))
        v_["o_a"] = jnp.concatenate(o_rows, axis=0)

    def g3_cumsum():
        tr = lax.broadcasted_iota(jnp.int32, (TP, TP), 0)
        tc = lax.broadcasted_iota(jnp.int32, (TP, TP), 1)
        tri = (((tr // C) == (tc // C)) & (tc <= tr)).astype(BF16)
        hi, mid, lo3 = _split3(v_["la"])
        b = _dot(tri, hi) + _dot(tri, mid) + _dot(tri, lo3)
        v_["b"] = b
        v_["b_last"] = [b[(i + 1) * C - 1:(i + 1) * C] for i in range(n_gla)]
        v_["b_mid"] = [b[i * C + C // 2 - 1:i * C + C // 2] for i in range(n_gla)]

    def g4_decayed():
        b, qg, kg = v_["b"], v_["qg"], v_["kg"]
        b_end = jnp.concatenate([jnp.broadcast_to(bl, (C, GLA_KEY_WIDTH)) for bl in v_["b_last"]], axis=0)
        b_ref = jnp.concatenate([jnp.broadcast_to(bm, (C, GLA_KEY_WIDTH)) for bm in v_["b_mid"]], axis=0)
        q_hat = qg * jnp.exp(b - b_ref)
        v_["k_til"] = (kg * jnp.exp(b_ref - b)).astype(BF16)
        k_dec = kg * jnp.exp(b_end - b)
        hmasks = _head_masks(TP)
        v_["q_m"] = [jnp.where(hm, q_hat, 0.0).astype(BF16) for hm in hmasks]
        v_["k_m"] = [jnp.where(hm, k_dec, 0.0).astype(BF16) for hm in hmasks]

    def g5_states():
        vg16 = v_["vg16"]
        states = [st_scr[...]]
        for i in range(n_gla):
            rows = slice(i * C, (i + 1) * C)
            v_stack = jnp.concatenate([vg16[rows, hh * GLA_DV:(hh + 1) * GLA_DV] for hh in range(GLA_HEADS)],
                                      axis=0)
            k_stack = jnp.concatenate([km[rows] for km in v_["k_m"]], axis=0)
            states.append(states[-1] * jnp.exp(v_["b_last"][i]) + _dot_tn(v_stack, k_stack))
        st_scr[...] = states[-1]
        v_["states"] = states

    def g6_mixed():
        mixed = []
        for i in range(n_gla):
            rows = slice(i * C, (i + 1) * C)
            q_stack = jnp.concatenate([qm[rows] for qm in v_["q_m"]], axis=0)
            s_ref = v_["states"][i] * jnp.exp(v_["b_mid"][i])
            rhs = jnp.concatenate([s_ref.astype(BF16), v_["k_til"][rows]], axis=0)
            mixed.append(_dot_nt(q_stack, rhs))
        v_["mixed"] = mixed

    def g7_outputs():
        sr = lax.broadcasted_iota(jnp.int32, (GLA_HEADS * C, C), 0)
        sc = lax.broadcasted_iota(jnp.int32, (GLA_HEADS * C, C), 1)
        causal = (sr & (C - 1)) >= sc
        vg16 = v_["vg16"]
        og_rows = []
        for i in range(n_gla):
            rows = slice(i * C, (i + 1) * C)
            m_i = v_["mixed"][i]
            a = jnp.where(causal, m_i[:, GLA_DV:], 0.0).astype(BF16)
            og_rows.append(jnp.concatenate(
                [m_i[hh * C:(hh + 1) * C, :GLA_DV]
                 + _dot(a[hh * C:(hh + 1) * C], vg16[rows, hh * GLA_DV:(hh + 1) * GLA_DV])
                 for hh in range(GLA_HEADS)], axis=1))
        v_["o_g"] = jnp.concatenate(og_rows, axis=0)

    seqs = range(GS)

    def s1_inputs():
        lo_s = lax.broadcasted_iota(jnp.int32, (L, LANES), 1) < HALF
        qrows = []
        for u in seqs:
            rows = slice(TP + u * L, TP + (u + 1) * L)
            pieces = ([jnp.where(lo_s, s[rows], 0.0) for s in v_["q_slabs"]]
                      + [jnp.where(lo_s, 0.0, s[rows]) for s in v_["q_slabs"]])
            qrows.append(jnp.concatenate(pieces, axis=0).astype(BF16))
        pad = jnp.zeros((WINDOW - TS, KV_WIDTH), F32)
        k_pad = jnp.concatenate([v_["k_s"], pad], axis=0)
        v_pad = jnp.concatenate([v_["v_s"], pad], axis=0)
        v_["qrows"], v_["k_pad_t"], v_["v_pad_t"], v_["v_pad"] = qrows, k_pad.T, v_pad.T, v_pad
        v_["snew"] = _dot(jnp.concatenate(qrows, axis=0), v_["k_pad_t"].astype(BF16))

    def s2_windows():
        keep_old = lax.broadcasted_iota(jnp.int32, (KV_WIDTH, WINDOW), 1) < WINDOW - L
        for u in seqs:
            shift = WINDOW - L - u * L
            ok_ref[u] = jnp.where(keep_old, pltpu.roll(ck_ref[u], WINDOW - L, 1),
                                  pltpu.roll(v_["k_pad_t"], shift, 1))
            ov_ref[u] = jnp.where(keep_old, pltpu.roll(cv_ref[u], WINDOW - L, 1),
                                  pltpu.roll(v_["v_pad_t"], shift, 1))

    def s3_scores():
        rr = lax.broadcasted_iota(jnp.int32, (nrow, LANES), 0) & (L - 1)
        cc = lax.broadcasted_iota(jnp.int32, (nrow, LANES), 1)
        v_["s_c"] = [jnp.where(cc >= rr, _dot(v_["qrows"][u], ck_ref[u].astype(BF16)), NEG) for u in seqs]
        v_["s_n"] = [jnp.where((cc >= u * L) & (cc <= u * L + rr), v_["snew"][u * nrow:(u + 1) * nrow], NEG)
                     for u in seqs]

    def s4_softmax():
        sink_col = jnp.concatenate([jnp.full((L, 1), sink_ref[hh] * LOG2E, F32) for hh in range(ATT_HEADS)],
                                   axis=0)
        e_c, e_n, rden = [], [], []
        for a, b_ in zip(v_["s_c"], v_["s_n"]):
            m = jnp.maximum(jnp.maximum(jnp.max(a, axis=-1, keepdims=True), jnp.max(b_, axis=-1, keepdims=True)),
                            sink_col)
            ec, en = jnp.exp2(a - m), jnp.exp2(b_ - m)
            rden.append(1.0 / (jnp.sum(ec, axis=-1, keepdims=True) + jnp.sum(en, axis=-1, keepdims=True)
                               + jnp.exp2(sink_col - m)))
            e_c.append(ec.astype(BF16))
            e_n.append(en.astype(BF16))
        v_["e_c"], v_["e_n"], v_["rden_s"] = e_c, e_n, rden

    def s5_values():
        lo_s = lax.broadcasted_iota(jnp.int32, (L, LANES), 1) < HALF
        o_new = _dot(jnp.concatenate(v_["e_n"], axis=0), v_["v_pad"].astype(BF16))
        rows_out = []
        for u in seqs:
            o = ((_dot_nt(v_["e_c"][u], cv_ref[u].astype(BF16)) + o_new[u * nrow:(u + 1) * nrow])
                 * v_["rden_s"][u])
            rows_out.append(jnp.concatenate(
                [jnp.where(lo_s, o[i * L:(i + 1) * L], o[(i + ATT_GROUP) * L:(i + ATT_GROUP + 1) * L])
                 for i in range(ATT_GROUP)], axis=1))
        v_["o_a_s"] = jnp.concatenate(rows_out, axis=0)

    def s6_gla():
        qg, kg, vg, la = v_["qg_s"], v_["kg_s"], v_["vg_s"], v_["la_s"]
        tr = lax.broadcasted_iota(jnp.int32, (TS, TS), 0)
        tc = lax.broadcasted_iota(jnp.int32, (TS, TS), 1)
        same_seq = (tr // L) == (tc // L)
        causal = same_seq & (tc <= tr)
        parts = _split3(la)
        tri, blk = causal.astype(BF16), same_seq.astype(BF16)
        b = _dot(tri, parts[0]) + _dot(tri, parts[1]) + _dot(tri, parts[2])
        b_last = _dot(blk, parts[0]) + _dot(blk, parts[1]) + _dot(blk, parts[2])
        q_hat = qg * jnp.exp(b)
        k_til = (kg * jnp.exp(-b)).astype(BF16)
        k_dec = kg * jnp.exp(b_last - b)
        hmasks = _head_masks(TS)
        q_m = [jnp.where(hm, q_hat, 0.0) for hm in hmasks]
        decay = jnp.exp(b_last)
        zk = jnp.zeros((L, GLA_DK), F32)
        zv = jnp.zeros((L, GLA_DV), F32)
        st0 = [s0_ref[u] for u in seqs]
        inter, upds, dcols = [], [], []
        for u in seqs:
            rows = slice(u * L, (u + 1) * L)
            q_stack = jnp.concatenate([qm[rows] for qm in q_m], axis=0).astype(BF16)
            inter.append(_dot(q_stack, st0[u].astype(BF16)))
            per_head = []
            for hh in range(GLA_HEADS):
                k_rows = jnp.concatenate([k_dec[rows, hh * GLA_DK:(hh + 1) * GLA_DK], zk], axis=0).astype(BF16)
                v_rows = jnp.concatenate([vg[rows, hh * GLA_DV:(hh + 1) * GLA_DV], zv], axis=0).astype(BF16)
                per_head.append(_dot_tn(k_rows, v_rows))
            upds.append(jnp.concatenate(per_head, axis=0))
            dcols.append(decay[rows].T[:, 0:1])
        intra = []
        for hh in range(GLA_HEADS):
            a = jnp.where(causal, _dot_nt(q_m[hh].astype(BF16), k_til), 0.0).astype(BF16)
            intra.append(_dot(a, vg[:, hh * GLA_DV:(hh + 1) * GLA_DV].astype(BF16)))
        for u in seqs:
            s1_ref[u] = dcols[u] * st0[u] + upds[u]
        o_inter = jnp.concatenate(
            [jnp.concatenate([inter[u][hh * L:(hh + 1) * L] for hh in range(GLA_HEADS)], axis=1) for u in seqs],
            axis=0)
        v_["o_g_s"] = o_inter + jnp.concatenate(intra, axis=1)

    gate_parts = {"silu_a": [], "silu_g": [], "merge_a": [], "merge_g": []}

    def gate_chunk(key, w_ref, row, silu):
        def run():
            z = _dot_nt(h, w_ref[row:row + MXU_TILE, :])
            s = _sigmoid(z)
            gate_parts[key].append(z * s if silu else s)
        return run

    t = ([gate_chunk("silu_a", wza_ref, r, True) for r in range(0, ATT_WIDTH, MXU_TILE)]
         + [gate_chunk("silu_g", wt_ref, R_ZG + r, True) for r in range(0, GLA_WIDTH, MXU_TILE)]
         + [gate_chunk("merge_a", wt_ref, R_MA + r, False) for r in range(0, D_MODEL, MXU_TILE)]
         + [gate_chunk("merge_g", wt_ref, R_MG + r, False) for r in range(0, D_MODEL, MXU_TILE)])

    def t_tail():
        silu_a, silu_g, merge_a, merge_g = (jnp.concatenate(gate_parts[key], axis=1)
                                            for key in ("silu_a", "silu_g", "merge_a", "merge_g"))
        o_a = jnp.concatenate([v_["o_a"], v_["o_a_s"]], axis=0) * silu_a
        o_g = jnp.concatenate([v_["o_g"], v_["o_g_s"]], axis=0)
        slabs = []
        for hh in range(GLA_HEADS):
            oh = o_g[:, hh * GLA_DV:(hh + 1) * GLA_DV]
            slabs.append(oh * lax.rsqrt(jnp.mean(oh * oh, axis=-1, keepdims=True) + EPS) * glang_ref[...])
        o_g = jnp.concatenate(slabs, axis=1) * silu_g
        merged = (merge_a * _dot(o_a.astype(BF16), wba_ref[...])
                  + merge_g * _dot(o_g.astype(BF16), wbg_ref[...]))
        out = _dot(merged.astype(BF16), wout_ref[...])
        yp_ref[...] = xp + prompt_mod(2) * out[:TP]
        ys_ref[...] = xs + per_token(2 * D_MODEL, 3 * D_MODEL) * out[TP:]

    for stage in (a1_qkv, g1_decay, t[0], g2_qkv, t[1], a2_norm_rope, t[2], s1_inputs, g3_cumsum, t[3],
                  a3_scores, s2_windows, t[4], g4_decayed, s3_scores, a4[0], t[5], a4[1], s4_softmax, t[6], a4[2],
                  t[7], a4[3], g5_states, s5_values, t[8], a5_values, s6_gla, t[9], g6_mixed, t[10], g7_outputs,
                  t[11], t_tail):
        stage()

    @pl.when(j == last)
    def _():
        wk_ref[...] = klast_scr[...].T
        wv_ref[...] = vlast_scr[...].T
        st_ref[...] = st_scr[...].T


def _fused_layer(xp, xs, mod, cos_p, sin_p, cos_s, sin_s, weights, ck, cv, s0):
    GS, TS = SEQ_GROUP, T_SAMPLE

    def const(shape):
        zeros = (0,) * len(shape)
        return pl.BlockSpec(shape, lambda b, j: zeros)

    def per_batch(shape):
        return pl.BlockSpec((None,) + shape, lambda b, j: (b, 0, 0))

    def per_step(shape):
        tail = (0,) * (len(shape) - 1)
        return pl.BlockSpec(shape, lambda b, j: (b * N_BLOCKS + j,) + tail)

    prompt_rows = pl.BlockSpec((None, T_PROMPT, D_MODEL), lambda b, j: (b, j, 0))
    table_rows = pl.BlockSpec((T_PROMPT, LANES), lambda b, j: (j, 0))
    weight_specs = [
        const((1, D_MODEL)),
        const((ATT_WIDTH, D_MODEL)),
        const((ATT_WIDTH, D_MODEL)),
        const((R_END, D_MODEL)),
        const((1, LANES)),
        const((1, LANES)),
        pl.BlockSpec(memory_space=pltpu.SMEM),
        const((LANES, GLA_KEY_WIDTH)),
        const((1, GLA_KEY_WIDTH)),
        const((1, GLA_DV)),
        const((ATT_WIDTH, D_MODEL)),
        const((GLA_WIDTH, D_MODEL)),
        const((D_MODEL, D_MODEL)),
    ]
    return pl.pallas_call(
        _layer_body,
        grid=(BATCH, N_BLOCKS),
        in_specs=[prompt_rows, per_step((TS, D_MODEL)),
                  pl.BlockSpec((MOD_BLOCK, 3 * D_MODEL), lambda b, j: (DEC_BATCH // MOD_BLOCK, 0)),
                  pl.BlockSpec((MOD_BLOCK, 3 * D_MODEL), lambda b, j: ((b * N_BLOCKS + j) * GS // MOD_BLOCK, 0)),
                  table_rows, table_rows, const((TS, LANES)), const((TS, LANES))]
                 + weight_specs
                 + [per_step((GS, KV_WIDTH, WINDOW)), per_step((GS, KV_WIDTH, WINDOW)),
                    per_step((GS, GLA_KEY_WIDTH, GLA_DV))],
        out_specs=[prompt_rows, per_step((TS, D_MODEL)),
                   per_batch((KV_WIDTH, WINDOW)), per_batch((KV_WIDTH, WINDOW)), per_batch((GLA_KEY_WIDTH, GLA_DV)),
                   per_step((GS, KV_WIDTH, WINDOW)), per_step((GS, KV_WIDTH, WINDOW)),
                   per_step((GS, GLA_KEY_WIDTH, GLA_DV))],
        out_shape=[
            jax.ShapeDtypeStruct((BATCH, SEQ, D_MODEL), F32),
            jax.ShapeDtypeStruct((DEC_BATCH * DEC_SEQ, D_MODEL), F32),
            jax.ShapeDtypeStruct((BATCH, KV_WIDTH, WINDOW), F32),
            jax.ShapeDtypeStruct((BATCH, KV_WIDTH, WINDOW), F32),
            jax.ShapeDtypeStruct((BATCH, GLA_KEY_WIDTH, GLA_DV), F32),
            jax.ShapeDtypeStruct((DEC_BATCH, KV_WIDTH, WINDOW), F32),
            jax.ShapeDtypeStruct((DEC_BATCH, KV_WIDTH, WINDOW), F32),
            jax.ShapeDtypeStruct((DEC_BATCH, GLA_KEY_WIDTH, GLA_DV), F32),
        ],
        scratch_shapes=[
            pltpu.VMEM((ATT_BLOCK, KV_WIDTH), BF16),
            pltpu.VMEM((ATT_BLOCK, KV_WIDTH), BF16),
            pltpu.VMEM((GLA_DV, GLA_KEY_WIDTH), F32),
            pltpu.VMEM((WINDOW, KV_WIDTH), F32),
            pltpu.VMEM((WINDOW, KV_WIDTH), F32),
        ],
        compiler_params=pltpu.CompilerParams(dimension_semantics=("arbitrary", "arbitrary"),
                                             vmem_limit_bytes=VMEM_LIMIT),
        name="fused_layer",
    )(xp, xs, mod, mod, cos_p, sin_p, cos_s, sin_s, *weights, ck, cv, s0)


def _rope_tables(pos):
    half = ATT_HEAD_DIM // 2
    inv = 1.0 / (ROPE_THETA ** (np.arange(half, dtype=np.float64) / half))
    ang = pos.astype(np.float64)[:, None] * inv[None, :]
    c, s = np.cos(ang), np.sin(ang)
    return (np.tile(c, (1, 4)).astype(np.float32),
            np.concatenate([-s, s, -s, s], axis=1).astype(np.float32))


def kernel(x_prompt, x_sample, cache_win_k, cache_win_v, state_gla, c_prompt, c_sample, norm_g, w_ada, b_ada, w_in, q_norm_g, k_norm_g, attn_sinks, w_gla_gate, b_gla_gate, gla_norm_g, w_branch_att, w_branch_gla, w_out):
    assert w_in.shape == (1, D_MODEL, R_END), "single-layer trunk"
    wt = jnp.swapaxes(w_in[0], 0, 1)
    wgate = jnp.concatenate([w_gla_gate[0], jnp.zeros((LANES - GLA_GATE_RANK, GLA_KEY_WIDTH), F32)],
                            axis=0).astype(BF16)
    mod, wq16, wza16, wba16, wbg16, wout16 = _prepare(c_sample, c_prompt, w_ada[0], b_ada[0][None, :], wt,
                                                      w_branch_att[0], w_branch_gla[0], w_out[0])
    weights = (
        norm_g[0][None, :], wq16, wza16, wt.astype(BF16),
        jnp.tile(q_norm_g[0], 2)[None, :], jnp.tile(k_norm_g[0], 2)[None, :],
        attn_sinks[0], wgate, b_gla_gate[0][None, :], gla_norm_g[0][None, :], wba16, wbg16, wout16,
    )

    cos_p, sin_p = _rope_tables(np.arange(SEQ))
    cos_s, sin_s = _rope_tables(PAST_LEN + np.arange(DEC_SEQ))
    cos_s, sin_s = np.tile(cos_s, (SEQ_GROUP, 1)), np.tile(sin_s, (SEQ_GROUP, 1))

    def kv_rows(c):
        return jnp.transpose(c, (0, 2, 3, 1)).reshape(c.shape[0], KV_WIDTH, WINDOW)

    def kv_out(c):
        return jnp.transpose(c.reshape(c.shape[0], ATT_KV_HEADS, ATT_HEAD_DIM, WINDOW), (0, 3, 1, 2))[None]

    y_p, y_s, wk_p, wv_p, st_p, wk_s, wv_s, st_s = _fused_layer(
        x_prompt, x_sample.reshape(DEC_BATCH * DEC_SEQ, D_MODEL), mod, cos_p, sin_p, cos_s, sin_s, weights,
        kv_rows(cache_win_k[0]), kv_rows(cache_win_v[0]),
        state_gla[0].reshape(DEC_BATCH, GLA_KEY_WIDTH, GLA_DV))

    st_shape = (1, -1, GLA_HEADS, GLA_DK, GLA_DV)
    return (y_p, y_s.reshape(DEC_BATCH, DEC_SEQ, D_MODEL),
            kv_out(wk_p), kv_out(wv_p), st_p.reshape(st_shape),
            kv_out(wk_s), kv_out(wv_s), st_s.reshape(st_shape))
```

```python
import numpy as np

import jax
import jax.numpy as jnp
from jax import lax
from jax.experimental import pallas as pl
from jax.experimental.pallas import tpu as pltpu

F32 = jnp.float32
BF16 = jnp.bfloat16

D_MODEL = 1024
BATCH = 2
SEQ = 8192
DEC_BATCH = 128
DEC_SEQ = 8
PAST_LEN = 16384
ATT_HEADS = 8
ATT_KV_HEADS = 2
ATT_GROUP = ATT_HEADS // ATT_KV_HEADS
ATT_HEAD_DIM = 64
ATT_WIDTH = ATT_HEADS * ATT_HEAD_DIM
KV_WIDTH = ATT_KV_HEADS * ATT_HEAD_DIM
WINDOW = 128
ATT_BLOCK = 128
ROPE_THETA = 10000.0
GLA_HEADS = 4
GLA_WIDTH = D_MODEL // 2
GLA_DV = GLA_WIDTH // GLA_HEADS
GLA_KEY_WIDTH = GLA_WIDTH // 2
GLA_DK = GLA_KEY_WIDTH // GLA_HEADS
GLA_GATE_RANK = 16
GLA_GATE_TAU = 16.0
EPS = 1e-6
NEG = -1e30
LOG2E = 1.4426950408889634
ATT_SCALE = ATT_HEAD_DIM ** -0.5 * LOG2E

LANES = 128
HALF = LANES // 2
MXU_TILE = 256
R_K, R_V, R_ZA = ATT_WIDTH, ATT_WIDTH + KV_WIDTH, ATT_WIDTH + 2 * KV_WIDTH
R_QG = R_ZA + ATT_WIDTH
R_KG = R_QG + GLA_KEY_WIDTH
R_VG = R_KG + GLA_KEY_WIDTH
R_LR = R_VG + GLA_WIDTH
R_ZG = R_LR + GLA_GATE_RANK
R_MA = R_ZG + GLA_WIDTH
R_MG = R_MA + D_MODEL
R_END = R_MG + D_MODEL

T_PROMPT = 512
N_BLOCKS = SEQ // T_PROMPT
GLA_BLOCK = 64
SEQ_GROUP = DEC_BATCH // (BATCH * N_BLOCKS)
T_SAMPLE = SEQ_GROUP * DEC_SEQ
MOD_ROWS = DEC_BATCH + 16
MOD_TILE = 1536
MOD_BLOCK = 8
VMEM_LIMIT = 56 * 1024 * 1024


def _sigmoid(x):
    return 1.0 / (1.0 + jnp.exp2(x * -LOG2E))


def _dot(a, b):
    return jnp.dot(a, b, preferred_element_type=F32)


def _dot_nt(a, b):
    return lax.dot_general(a, b, (((1,), (1,)), ((), ())), preferred_element_type=F32)


def _dot_tn(a, b):
    return lax.dot_general(a, b, (((0,), (0,)), ((), ())), preferred_element_type=F32)


def _split3(x):
    hi = x.astype(BF16)
    r1 = x - hi.astype(F32)
    mid = r1.astype(BF16)
    lo = (r1 - mid.astype(F32)).astype(BF16)
    return hi, mid, lo


def _norm_rope_slab(xs, g, cos, sin):
    lane = lax.broadcasted_iota(jnp.int32, xs.shape, 1)
    lo = lane < HALF
    first = (lane & (HALF // 2)) == 0
    sq = xs * xs
    s_lo = jnp.sum(jnp.where(lo, sq, 0.0), axis=-1, keepdims=True)
    s_hi = jnp.sum(jnp.where(lo, 0.0, sq), axis=-1, keepdims=True)
    inv = jnp.where(lo, lax.rsqrt(s_lo * (1.0 / ATT_HEAD_DIM) + EPS),
                    lax.rsqrt(s_hi * (1.0 / ATT_HEAD_DIM) + EPS))
    xn = xs * inv * g
    swapped = jnp.where(first, pltpu.roll(xn, LANES - HALF // 2, 1), pltpu.roll(xn, HALF // 2, 1))
    return xn * cos + swapped * sin


def _log_decay(lrz, wgate_ref, bgate_ref):
    pre = _dot(lrz.astype(BF16), wgate_ref[...]) + bgate_ref[...]
    log_sig = jnp.minimum(pre, 0.0) - jnp.log(1.0 + jnp.exp(-jnp.abs(pre)))
    return log_sig * (1.0 / GLA_GATE_TAU)


def _head_masks(rows):
    lane = lax.broadcasted_iota(jnp.int32, (rows, GLA_KEY_WIDTH), 1)
    return [(lane >= h * GLA_DK) & (lane < (h + 1) * GLA_DK) for h in range(GLA_HEADS)]


def _softmax_block(s, mask, sink_ref):
    probs, rden = [], []
    for h in range(ATT_HEADS):
        sh = jnp.where(mask, s[h * ATT_BLOCK:(h + 1) * ATT_BLOCK], NEG)
        sink = sink_ref[h] * LOG2E
        m = jnp.maximum(jnp.max(sh, axis=-1, keepdims=True), sink)
        e = jnp.exp2(sh - m)
        rden.append(1.0 / (jnp.sum(e, axis=-1, keepdims=True) + jnp.exp2(sink - m)))
        probs.append(e.astype(BF16))
    return jnp.concatenate(probs, axis=0), rden


def _paired_rows(lo_ref, hi_ref, hi_base):
    rows = []
    for g in range(ATT_GROUP):
        rows.append(lo_ref[g * ATT_HEAD_DIM:(g + 1) * ATT_HEAD_DIM, :])
        rows.append(hi_ref[hi_base + g * ATT_HEAD_DIM:hi_base + (g + 1) * ATT_HEAD_DIM, :])
    return jnp.concatenate(rows, axis=0).astype(BF16)


def _prep_body(cs_ref, cp_ref, w_ref, b_ref, wq_in, wza_lo, wza_hi, wba_in, wbg_in, wout_in,
               mod_ref, wq_out, wza_out, wba_out, wbg_out, wout_out, act_ref):
    half = ATT_WIDTH // 2

    @pl.when(pl.program_id(0) == 0)
    def _():
        cs = cs_ref[...]
        cp = jnp.concatenate([cp_ref[...], jnp.zeros((MOD_ROWS - DEC_BATCH - BATCH, D_MODEL), F32)], axis=0)
        act_ref[0:DEC_BATCH, :] = (cs * _sigmoid(cs)).astype(BF16)
        act_ref[DEC_BATCH:MOD_ROWS, :] = (cp * _sigmoid(cp)).astype(BF16)
        wq_out[...] = _paired_rows(wq_in, wq_in, half)
        wza_out[...] = _paired_rows(wza_lo, wza_hi, 0)
        wba_out[...] = _paired_rows(wba_in, wba_in, half)
        wbg_out[...] = wbg_in[...].astype(BF16)
        wout_out[...] = wout_in[...].astype(BF16)

    mod_ref[...] = _dot(act_ref[...], w_ref[...].astype(BF16)) + b_ref[...]


def _prepare(c_sample, c_prompt, w_ada, b_ada, wt, w_branch_att, w_branch_gla, w_out):
    half = ATT_WIDTH // 2
    whole = lambda shape: pl.BlockSpec(shape, lambda i: (0,) * len(shape))
    return pl.pallas_call(
        _prep_body,
        grid=(3 * D_MODEL // MOD_TILE,),
        in_specs=[
            whole((DEC_BATCH, D_MODEL)),
            whole((BATCH, D_MODEL)),
            pl.BlockSpec((D_MODEL, MOD_TILE), lambda i: (0, i)),
            pl.BlockSpec((1, MOD_TILE), lambda i: (0, i)),
            pl.BlockSpec((ATT_WIDTH, D_MODEL), lambda i: (0, 0)),
            pl.BlockSpec((half, D_MODEL), lambda i: (R_ZA // half, 0)),
            pl.BlockSpec((half, D_MODEL), lambda i: (R_ZA // half + 1, 0)),
            whole((ATT_WIDTH, D_MODEL)), whole((GLA_WIDTH, D_MODEL)), whole((D_MODEL, D_MODEL)),
        ],
        out_specs=[pl.BlockSpec((MOD_ROWS, MOD_TILE), lambda i: (0, i)),
                   whole((ATT_WIDTH, D_MODEL)), whole((ATT_WIDTH, D_MODEL)), whole((ATT_WIDTH, D_MODEL)),
                   whole((GLA_WIDTH, D_MODEL)), whole((D_MODEL, D_MODEL))],
        out_shape=[jax.ShapeDtypeStruct((MOD_ROWS, 3 * D_MODEL), F32),
                   jax.ShapeDtypeStruct((ATT_WIDTH, D_MODEL), BF16), jax.ShapeDtypeStruct((ATT_WIDTH, D_MODEL), BF16),
                   jax.ShapeDtypeStruct((ATT_WIDTH, D_MODEL), BF16), jax.ShapeDtypeStruct((GLA_WIDTH, D_MODEL), BF16),
                   jax.ShapeDtypeStruct((D_MODEL, D_MODEL), BF16)],
        scratch_shapes=[pltpu.VMEM((MOD_ROWS, D_MODEL), BF16)],
        compiler_params=pltpu.CompilerParams(dimension_semantics=("arbitrary",),
                                             vmem_limit_bytes=VMEM_LIMIT),
        name="prepare",
    )(c_sample, c_prompt, w_ada, b_ada, wt, wt, wt, w_branch_att, w_branch_gla, w_out)


def _layer_body(xp_ref, xs_ref, modp_ref, mods_ref, cosp_ref, sinp_ref, coss_ref, sins_ref,
                ng_ref, wq_ref, wza_ref, wt_ref, qng_ref, kng_ref, sink_ref,
                wgate_ref, bgate_ref, glang_ref, wba_ref, wbg_ref, wout_ref,
                ck_ref, cv_ref, s0_ref,
                yp_ref, ys_ref, wk_ref, wv_ref, st_ref, ok_ref, ov_ref, s1_ref,
                kprev_ref, vprev_ref, st_scr, klast_scr, vlast_scr):
    TP, TS, C, L, GS = T_PROMPT, T_SAMPLE, GLA_BLOCK, DEC_SEQ, SEQ_GROUP
    n_att, n_gla = TP // ATT_BLOCK, TP // C
    nrow = ATT_HEADS * L
    j = pl.program_id(1)
    last = pl.num_programs(1) - 1

    @pl.when(j == 0)
    def _():
        kprev_ref[...] = jnp.zeros_like(kprev_ref)
        vprev_ref[...] = jnp.zeros_like(vprev_ref)
        st_scr[...] = jnp.zeros_like(st_scr)

    row_s = ((pl.program_id(0) * pl.num_programs(1) + j) % (MOD_BLOCK // GS)) * GS
    row_p = pl.program_id(0)

    def per_token(a, b):
        return jnp.concatenate([jnp.broadcast_to(mods_ref[pl.ds(row_s + u, 1), a:b], (L, b - a)) for u in range(GS)],
                               axis=0)

    def prompt_mod(k):
        return modp_ref[pl.ds(row_p, 1), k * D_MODEL:(k + 1) * D_MODEL]

    xp, xs = xp_ref[...], xs_ref[...]
    hp = (xp * lax.rsqrt(jnp.mean(xp * xp, axis=-1, keepdims=True) + EPS) * ng_ref[...]
          * (1.0 + prompt_mod(1)) + prompt_mod(0))
    hs = (xs * lax.rsqrt(jnp.mean(xs * xs, axis=-1, keepdims=True) + EPS) * ng_ref[...]
          * (1.0 + per_token(D_MODEL, 2 * D_MODEL)) + per_token(0, D_MODEL))
    h = jnp.concatenate([hp, hs], axis=0).astype(BF16)

    def proj(a, b):
        return _dot_nt(h, wt_ref[a:b, :])

    cos = jnp.concatenate([cosp_ref[...], coss_ref[...]], axis=0)
    sin = jnp.concatenate([sinp_ref[...], sins_ref[...]], axis=0)
    lane = lax.broadcasted_iota(jnp.int32, (ATT_BLOCK, LANES), 1)
    lo = lane < HALF
    v_ = {}

    def a1_qkv():
        kv = proj(R_K, R_ZA)
        v_["q"], v_["k_raw"], v_["v"] = _dot_nt(h, wq_ref[...]), kv[:, :KV_WIDTH], kv[:, KV_WIDTH:]

    def a2_norm_rope():
        q = v_["q"]
        v_["q_slabs"] = [_norm_rope_slab(q[:, i * LANES:(i + 1) * LANES], qng_ref[...], cos, sin)
                         * ATT_SCALE for i in range(ATT_GROUP)]
        k = _norm_rope_slab(v_["k_raw"], kng_ref[...], cos, sin)
        v = v_["v"]
        klast_scr[...] = k[TP - WINDOW:TP]
        vlast_scr[...] = v[TP - WINDOW:TP]
        v_["k16"], v_["v16"] = k[:TP].astype(BF16), v[:TP].astype(BF16)
        v_["k_s"], v_["v_s"] = k[TP:], v[TP:]

    def g1_decay():
        la = _log_decay(proj(R_LR, R_LR + LANES), wgate_ref, bgate_ref)
        v_["la"], v_["la_s"] = la[:TP], la[TP:]

    def g2_qkv():
        qg = proj(R_QG, R_KG) * (GLA_DK ** -0.5)
        kg = proj(R_KG, R_VG)
        vg = proj(R_VG, R_LR)
        v_["qg"], v_["kg"], v_["vg16"] = qg[:TP], kg[:TP], vg[:TP].astype(BF16)
        v_["qg_s"], v_["kg_s"], v_["vg_s"] = qg[TP:], kg[TP:], vg[TP:]

    def a3_masks():
        r = lax.broadcasted_iota(jnp.int32, (ATT_BLOCK, 2 * ATT_BLOCK), 0)
        c = lax.broadcasted_iota(jnp.int32, (ATT_BLOCK, 2 * ATT_BLOCK), 1)
        band = ((c < ATT_BLOCK) & (c >= r)) | ((c >= ATT_BLOCK) & (c - ATT_BLOCK <= r))
        v_["masks"] = [band & ((c >= ATT_BLOCK) | (j > 0))] + [band] * (n_att - 1)
        kprev, vprev = kprev_ref[...], vprev_ref[...]
        v_["kprev"], v_["vprev"] = kprev, vprev
        kprev_ref[...] = v_["k16"][TP - ATT_BLOCK:]
        vprev_ref[...] = v_["v16"][TP - ATT_BLOCK:]

    v_["scores"], v_["vcats"], v_["soft"], v_["o_rows"] = ([None] * n_att for _ in range(4))

    def scores_stage(i):
        def run():
            k16, v16, q_slabs = v_["k16"], v_["v16"], v_["q_slabs"]
            rows = slice(i * ATT_BLOCK, (i + 1) * ATT_BLOCK)
            if i == 0:
                kp, vp = v_["kprev"], v_["vprev"]
            else:
                prev = slice((i - 1) * ATT_BLOCK, i * ATT_BLOCK)
                kp, vp = k16[prev], v16[prev]
            pieces = ([jnp.where(lo, s[rows], 0.0).astype(BF16) for s in q_slabs]
                      + [jnp.where(lo, 0.0, s[rows]).astype(BF16) for s in q_slabs])
            v_["scores"][i] = _dot_nt(jnp.concatenate(pieces, axis=0), jnp.concatenate([kp, k16[rows]], axis=0))
            v_["vcats"][i] = jnp.concatenate([vp, v16[rows]], axis=0)
        return run

    def softmax_stage(i):
        def run():
            v_["soft"][i] = _softmax_block(v_["scores"][i], v_["masks"][i], sink_ref)
        return run

    def values_stage(i):
        def run():
            probs, rden = v_["soft"][i]
            o = _dot(probs, v_["vcats"][i])
            v_["o_rows"][i] = jnp.concatenate(
                [jnp.where(lo, o[g * ATT_BLOCK:(g + 1) * ATT_BLOCK] * rden[g],
                           o[(g + ATT_GROUP) * ATT_BLOCK:(g + ATT_GROUP + 1) * ATT_BLOCK] * rden[g + ATT_GROUP])
                 for g in range(ATT_GROUP)], axis=1)
        return run

    a3 = [scores_stage(i) for i in range(n_att)]
    a4 = [softmax_stage(i) for i in range(n_att)]
    a5 = [values_stage(i) for i in range(n_att)]

    def g3_cumsum():
        tr = lax.broadcasted_iota(jnp.int32, (TP, TP), 0)
        tc = lax.broadcasted_iota(jnp.int32, (TP, TP), 1)
        tri = (((tr // C) == (tc // C)) & (tc <= tr)).astype(BF16)
        hi, mid, lo3 = _split3(v_["la"])
        b = _dot(tri, hi) + _dot(tri, mid) + _dot(tri, lo3)
        v_["b"] = b
        v_["b_last"] = [b[(i + 1) * C - 1:(i + 1) * C] for i in range(n_gla)]
        v_["b_mid"] = [b[i * C + C // 2 - 1:i * C + C // 2] for i in range(n_gla)]

    def g4_decayed():
        b, qg, kg = v_["b"], v_["qg"], v_["kg"]
        b_end = jnp.concatenate([jnp.broadcast_to(bl, (C, GLA_KEY_WIDTH)) for bl in v_["b_last"]], axis=0)
        b_ref = jnp.concatenate([jnp.broadcast_to(bm, (C, GLA_KEY_WIDTH)) for bm in v_["b_mid"]], axis=0)
        q_hat = qg * jnp.exp(b - b_ref)
        v_["k_til"] = (kg * jnp.exp(b_ref - b)).astype(BF16)
        k_dec = kg * jnp.exp(b_end - b)
        hmasks = _head_masks(TP)
        v_["q_m"] = [jnp.where(hm, q_hat, 0.0).astype(BF16) for hm in hmasks]
        v_["k_m"] = [jnp.where(hm, k_dec, 0.0).astype(BF16) for hm in hmasks]

    def g5_states():
        vg16 = v_["vg16"]
        states = [st_scr[...]]
        for i in range(n_gla):
            rows = slice(i * C, (i + 1) * C)
            v_stack = jnp.concatenate([vg16[rows, hh * GLA_DV:(hh + 1) * GLA_DV] for hh in range(GLA_HEADS)],
                                      axis=0)
            k_stack = jnp.concatenate([km[rows] for km in v_["k_m"]], axis=0)
            states.append(states[-1] * jnp.exp(v_["b_last"][i]) + _dot_tn(v_stack, k_stack))
        st_scr[...] = states[-1]
        v_["states"] = states

    def g6_mixed():
        mixed = []
        for i in range(n_gla):
            rows = slice(i * C, (i + 1) * C)
            q_stack = jnp.concatenate([qm[rows] for qm in v_["q_m"]], axis=0)
            s_ref = v_["states"][i] * jnp.exp(v_["b_mid"][i])
            rhs = jnp.concatenate([s_ref.astype(BF16), v_["k_til"][rows]], axis=0)
            mixed.append(_dot_nt(q_stack, rhs))
        v_["mixed"] = mixed

    def g7_outputs():
        sr = lax.broadcasted_iota(jnp.int32, (GLA_HEADS * C, C), 0)
        sc = lax.broadcasted_iota(jnp.int32, (GLA_HEADS * C, C), 1)
        causal = (sr & (C - 1)) >= sc
        vg16 = v_["vg16"]
        og_rows = []
        for i in range(n_gla):
            rows = slice(i * C, (i + 1) * C)
            m_i = v_["mixed"][i]
            a = jnp.where(causal, m_i[:, GLA_DV:], 0.0).astype(BF16)
            og_rows.append(jnp.concatenate(
                [m_i[hh * C:(hh + 1) * C, :GLA_DV]
                 + _dot(a[hh * C:(hh + 1) * C], vg16[rows, hh * GLA_DV:(hh + 1) * GLA_DV])
                 for hh in range(GLA_HEADS)], axis=1))
        v_["o_g"] = jnp.concatenate(og_rows, axis=0)

    seqs = range(GS)

    def s1_inputs():
        lo_s = lax.broadcasted_iota(jnp.int32, (L, LANES), 1) < HALF
        qrows = []
        for u in seqs:
            rows = slice(TP + u * L, TP + (u + 1) * L)
            pieces = ([jnp.where(lo_s, s[rows], 0.0) for s in v_["q_slabs"]]
                      + [jnp.where(lo_s, 0.0, s[rows]) for s in v_["q_slabs"]])
            qrows.append(jnp.concatenate(pieces, axis=0).astype(BF16))
        pad = jnp.zeros((WINDOW - TS, KV_WIDTH), F32)
        k_pad = jnp.concatenate([v_["k_s"], pad], axis=0)
        v_pad = jnp.concatenate([v_["v_s"], pad], axis=0)
        v_["qrows"], v_["k_pad_t"], v_["v_pad_t"], v_["v_pad"] = qrows, k_pad.T, v_pad.T, v_pad
        v_["snew"] = _dot(jnp.concatenate(qrows, axis=0), v_["k_pad_t"].astype(BF16))

    def s2_windows():
        keep_old = lax.broadcasted_iota(jnp.int32, (KV_WIDTH, WINDOW), 1) < WINDOW - L
        for u in seqs:
            shift = WINDOW - L - u * L
            ok_ref[u] = jnp.where(keep_old, pltpu.roll(ck_ref[u], WINDOW - L, 1),
                                  pltpu.roll(v_["k_pad_t"], shift, 1))
            ov_ref[u] = jnp.where(keep_old, pltpu.roll(cv_ref[u], WINDOW - L, 1),
                                  pltpu.roll(v_["v_pad_t"], shift, 1))

    def s3_scores():
        rr = lax.broadcasted_iota(jnp.int32, (nrow, LANES), 0) & (L - 1)
        cc = lax.broadcasted_iota(jnp.int32, (nrow, LANES), 1)
        v_["s_c"] = [jnp.where(cc >= rr, _dot(v_["qrows"][u], ck_ref[u].astype(BF16)), NEG) for u in seqs]
        v_["s_n"] = [jnp.where((cc >= u * L) & (cc <= u * L + rr), v_["snew"][u * nrow:(u + 1) * nrow], NEG)
                     for u in seqs]

    def s4_softmax():
        sink_col = jnp.concatenate([jnp.full((L, 1), sink_ref[hh] * LOG2E, F32) for hh in range(ATT_HEADS)],
                                   axis=0)
        e_c, e_n, rden = [], [], []
        for a, b_ in zip(v_["s_c"], v_["s_n"]):
            m = jnp.maximum(jnp.maximum(jnp.max(a, axis=-1, keepdims=True), jnp.max(b_, axis=-1, keepdims=True)),
                            sink_col)
            ec, en = jnp.exp2(a - m), jnp.exp2(b_ - m)
            rden.append(1.0 / (jnp.sum(ec, axis=-1, keepdims=True) + jnp.sum(en, axis=-1, keepdims=True)
                               + jnp.exp2(sink_col - m)))
            e_c.append(ec.astype(BF16))
            e_n.append(en.astype(BF16))
        v_["e_c"], v_["e_n"], v_["rden_s"] = e_c, e_n, rden

    def s5_values():
        lo_s = lax.broadcasted_iota(jnp.int32, (L, LANES), 1) < HALF
        o_new = _dot(jnp.concatenate(v_["e_n"], axis=0), v_["v_pad"].astype(BF16))
        rows_out = []
        for u in seqs:
            o = ((_dot_nt(v_["e_c"][u], cv_ref[u].astype(BF16)) + o_new[u * nrow:(u + 1) * nrow])
                 * v_["rden_s"][u])
            rows_out.append(jnp.concatenate(
                [jnp.where(lo_s, o[i * L:(i + 1) * L], o[(i + ATT_GROUP) * L:(i + ATT_GROUP + 1) * L])
                 for i in range(ATT_GROUP)], axis=1))
        v_["o_a_s"] = jnp.concatenate(rows_out, axis=0)

    def s6_gla():
        qg, kg, vg, la = v_["qg_s"], v_["kg_s"], v_["vg_s"], v_["la_s"]
        tr = lax.broadcasted_iota(jnp.int32, (TS, TS), 0)
        tc = lax.broadcasted_iota(jnp.int32, (TS, TS), 1)
        same_seq = (tr // L) == (tc // L)
        causal = same_seq & (tc <= tr)
        parts = _split3(la)
        tri, blk = causal.astype(BF16), same_seq.astype(BF16)
        b = _dot(tri, parts[0]) + _dot(tri, parts[1]) + _dot(tri, parts[2])
        b_last = _dot(blk, parts[0]) + _dot(blk, parts[1]) + _dot(blk, parts[2])
        q_hat = qg * jnp.exp(b)
        k_til = (kg * jnp.exp(-b)).astype(BF16)
        k_dec = kg * jnp.exp(b_last - b)
        hmasks = _head_masks(TS)
        q_m = [jnp.where(hm, q_hat, 0.0) for hm in hmasks]
        decay = jnp.exp(b_last)
        zk = jnp.zeros((L, GLA_DK), F32)
        zv = jnp.zeros((L, GLA_DV), F32)
        st0 = [s0_ref[u] for u in seqs]
        inter, upds, dcols = [], [], []
        for u in seqs:
            rows = slice(u * L, (u + 1) * L)
            q_stack = jnp.concatenate([qm[rows] for qm in q_m], axis=0).astype(BF16)
            inter.append(_dot(q_stack, st0[u].astype(BF16)))
            per_head = []
            for hh in range(GLA_HEADS):
                k_rows = jnp.concatenate([k_dec[rows, hh * GLA_DK:(hh + 1) * GLA_DK], zk], axis=0).astype(BF16)
                v_rows = jnp.concatenate([vg[rows, hh * GLA_DV:(hh + 1) * GLA_DV], zv], axis=0).astype(BF16)
                per_head.append(_dot_tn(k_rows, v_rows))
            upds.append(jnp.concatenate(per_head, axis=0))
            dcols.append(decay[rows].T[:, 0:1])
        intra = []
        for hh in range(GLA_HEADS):
            a = jnp.where(causal, _dot_nt(q_m[hh].astype(BF16), k_til), 0.0).astype(BF16)
            intra.append(_dot(a, vg[:, hh * GLA_DV:(hh + 1) * GLA_DV].astype(BF16)))
        for u in seqs:
            s1_ref[u] = dcols[u] * st0[u] + upds[u]
        o_inter = jnp.concatenate(
            [jnp.concatenate([inter[u][hh * L:(hh + 1) * L] for hh in range(GLA_HEADS)], axis=1) for u in seqs],
            axis=0)
        v_["o_g_s"] = o_inter + jnp.concatenate(intra, axis=1)

    gate_parts = {"silu_a": [], "silu_g": [], "merge_a": [], "merge_g": []}

    def gate_chunk(key, w_ref, row, silu):
        def run():
            z = _dot_nt(h, w_ref[row:row + MXU_TILE, :])
            s = _sigmoid(z)
            gate_parts[key].append(z * s if silu else s)
        return run

    t = ([gate_chunk("silu_a", wza_ref, r, True) for r in range(0, ATT_WIDTH, MXU_TILE)]
         + [gate_chunk("silu_g", wt_ref, R_ZG + r, True) for r in range(0, GLA_WIDTH, MXU_TILE)]
         + [gate_chunk("merge_a", wt_ref, R_MA + r, False) for r in range(0, D_MODEL, MXU_TILE)]
         + [gate_chunk("merge_g", wt_ref, R_MG + r, False) for r in range(0, D_MODEL, MXU_TILE)])

    def t_tail():
        silu_a, silu_g, merge_a, merge_g = (jnp.concatenate(gate_parts[key], axis=1)
                                            for key in ("silu_a", "silu_g", "merge_a", "merge_g"))
        o_a = jnp.concatenate(v_["o_rows"] + [v_["o_a_s"]], axis=0) * silu_a
        o_g = jnp.concatenate([v_["o_g"], v_["o_g_s"]], axis=0)
        slabs = []
        for hh in range(GLA_HEADS):
            oh = o_g[:, hh * GLA_DV:(hh + 1) * GLA_DV]
            slabs.append(oh * lax.rsqrt(jnp.mean(oh * oh, axis=-1, keepdims=True) + EPS) * glang_ref[...])
        o_g = jnp.concatenate(slabs, axis=1) * silu_g
        merged = (merge_a * _dot(o_a.astype(BF16), wba_ref[...])
                  + merge_g * _dot(o_g.astype(BF16), wbg_ref[...]))
        out = _dot(merged.astype(BF16), wout_ref[...])
        yp_ref[...] = xp + prompt_mod(2) * out[:TP]
        ys_ref[...] = xs + per_token(2 * D_MODEL, 3 * D_MODEL) * out[TP:]

    for stage in (a1_qkv, g1_decay, g2_qkv, t[0], t[1], g3_cumsum, a2_norm_rope, t[2], s1_inputs, t[3], a3_masks,
                  a3[0], s2_windows, a4[0], t[4], a3[2], a3[1], g4_decayed, a4[1], s3_scores, a5[0], t[5], t[6],
                  a3[3], a4[2], s4_softmax, a5[1], a4[3], t[7], a5[2], g5_states, g6_mixed, s5_values, s6_gla,
                  t[8], a5[3], t[9], t[10], g7_outputs, t[11], t_tail):
        stage()

    @pl.when(j == last)
    def _():
        wk_ref[...] = klast_scr[...].T
        wv_ref[...] = vlast_scr[...].T
        st_ref[...] = st_scr[...].T


def _fused_layer(xp, xs, mod, cos_p, sin_p, cos_s, sin_s, weights, ck, cv, s0):
    GS, TS = SEQ_GROUP, T_SAMPLE

    def const(shape):
        zeros = (0,) * len(shape)
        return pl.BlockSpec(shape, lambda b, j: zeros)

    def per_batch(shape):
        return pl.BlockSpec((None,) + shape, lambda b, j: (b, 0, 0))

    def per_step(shape):
        tail = (0,) * (len(shape) - 1)
        return pl.BlockSpec(shape, lambda b, j: (b * N_BLOCKS + j,) + tail)

    prompt_rows = pl.BlockSpec((None, T_PROMPT, D_MODEL), lambda b, j: (b, j, 0))
    table_rows = pl.BlockSpec((T_PROMPT, LANES), lambda b, j: (j, 0))
    weight_specs = [
        const((1, D_MODEL)),
        const((ATT_WIDTH, D_MODEL)),
        const((ATT_WIDTH, D_MODEL)),
        const((R_END, D_MODEL)),
        const((1, LANES)),
        const((1, LANES)),
        pl.BlockSpec(memory_space=pltpu.SMEM),
        const((LANES, GLA_KEY_WIDTH)),
        const((1, GLA_KEY_WIDTH)),
        const((1, GLA_DV)),
        const((ATT_WIDTH, D_MODEL)),
        const((GLA_WIDTH, D_MODEL)),
        const((D_MODEL, D_MODEL)),
    ]
    return pl.pallas_call(
        _layer_body,
        grid=(BATCH, N_BLOCKS),
        in_specs=[prompt_rows, per_step((TS, D_MODEL)),
                  pl.BlockSpec((MOD_BLOCK, 3 * D_MODEL), lambda b, j: (DEC_BATCH // MOD_BLOCK, 0)),
                  pl.BlockSpec((MOD_BLOCK, 3 * D_MODEL), lambda b, j: ((b * N_BLOCKS + j) * GS // MOD_BLOCK, 0)),
                  table_rows, table_rows, const((TS, LANES)), const((TS, LANES))]
                 + weight_specs
                 + [per_step((GS, KV_WIDTH, WINDOW)), per_step((GS, KV_WIDTH, WINDOW)),
                    per_step((GS, GLA_KEY_WIDTH, GLA_DV))],
        out_specs=[prompt_rows, per_step((TS, D_MODEL)),
                   per_batch((KV_WIDTH, WINDOW)), per_batch((KV_WIDTH, WINDOW)), per_batch((GLA_KEY_WIDTH, GLA_DV)),
                   per_step((GS, KV_WIDTH, WINDOW)), per_step((GS, KV_WIDTH, WINDOW)),
                   per_step((GS, GLA_KEY_WIDTH, GLA_DV))],
        out_shape=[
            jax.ShapeDtypeStruct((BATCH, SEQ, D_MODEL), F32),
            jax.ShapeDtypeStruct((DEC_BATCH * DEC_SEQ, D_MODEL), F32),
            jax.ShapeDtypeStruct((BATCH, KV_WIDTH, WINDOW), F32),
            jax.ShapeDtypeStruct((BATCH, KV_WIDTH, WINDOW), F32),
            jax.ShapeDtypeStruct((BATCH, GLA_KEY_WIDTH, GLA_DV), F32),
            jax.ShapeDtypeStruct((DEC_BATCH, KV_WIDTH, WINDOW), F32),
            jax.ShapeDtypeStruct((DEC_BATCH, KV_WIDTH, WINDOW), F32),
            jax.ShapeDtypeStruct((DEC_BATCH, GLA_KEY_WIDTH, GLA_DV), F32),
        ],
        scratch_shapes=[
            pltpu.VMEM((ATT_BLOCK, KV_WIDTH), BF16),
            pltpu.VMEM((ATT_BLOCK, KV_WIDTH), BF16),
            pltpu.VMEM((GLA_DV, GLA_KEY_WIDTH), F32),
            pltpu.VMEM((WINDOW, KV_WIDTH), F32),
            pltpu.VMEM((WINDOW, KV_WIDTH), F32),
        ],
        compiler_params=pltpu.CompilerParams(dimension_semantics=("arbitrary", "arbitrary"),
                                             vmem_limit_bytes=VMEM_LIMIT),
        name="fused_layer",
    )(xp, xs, mod, mod, cos_p, sin_p, cos_s, sin_s, *weights, ck, cv, s0)


def _rope_tables(pos):
    half = ATT_HEAD_DIM // 2
    inv = 1.0 / (ROPE_THETA ** (np.arange(half, dtype=np.float64) / half))
    ang = pos.astype(np.float64)[:, None] * inv[None, :]
    c, s = np.cos(ang), np.sin(ang)
    return (np.tile(c, (1, 4)).astype(np.float32),
            np.concatenate([-s, s, -s, s], axis=1).astype(np.float32))


def kernel(x_prompt, x_sample, cache_win_k, cache_win_v, state_gla, c_prompt, c_sample, norm_g, w_ada, b_ada, w_in, q_norm_g, k_norm_g, attn_sinks, w_gla_gate, b_gla_gate, gla_norm_g, w_branch_att, w_branch_gla, w_out):
    assert w_in.shape == (1, D_MODEL, R_END), "single-layer trunk"
    wt = jnp.swapaxes(w_in[0], 0, 1)
    wgate = jnp.concatenate([w_gla_gate[0], jnp.zeros((LANES - GLA_GATE_RANK, GLA_KEY_WIDTH), F32)],
                            axis=0).astype(BF16)
    mod, wq16, wza16, wba16, wbg16, wout16 = _prepare(c_sample, c_prompt, w_ada[0], b_ada[0][None, :], wt,
                                                      w_branch_att[0], w_branch_gla[0], w_out[0])
    weights = (
        norm_g[0][None, :], wq16, wza16, wt.astype(BF16),
        jnp.tile(q_norm_g[0], 2)[None, :], jnp.tile(k_norm_g[0], 2)[None, :],
        attn_sinks[0], wgate, b_gla_gate[0][None, :], gla_norm_g[0][None, :], wba16, wbg16, wout16,
    )

    cos_p, sin_p = _rope_tables(np.arange(SEQ))
    cos_s, sin_s = _rope_tables(PAST_LEN + np.arange(DEC_SEQ))
    cos_s, sin_s = np.tile(cos_s, (SEQ_GROUP, 1)), np.tile(sin_s, (SEQ_GROUP, 1))

    def kv_rows(c):
        return jnp.transpose(c, (0, 2, 3, 1)).reshape(c.shape[0], KV_WIDTH, WINDOW)

    def kv_out(c):
        return jnp.transpose(c.reshape(c.shape[0], ATT_KV_HEADS, ATT_HEAD_DIM, WINDOW), (0, 3, 1, 2))[None]

    y_p, y_s, wk_p, wv_p, st_p, wk_s, wv_s, st_s = _fused_layer(
        x_prompt, x_sample.reshape(DEC_BATCH * DEC_SEQ, D_MODEL), mod, cos_p, sin_p, cos_s, sin_s, weights,
        kv_rows(cache_win_k[0]), kv_rows(cache_win_v[0]),
        state_gla[0].reshape(DEC_BATCH, GLA_KEY_WIDTH, GLA_DV))

    st_shape = (1, -1, GLA_HEADS, GLA_DK, GLA_DV)
    return (y_p, y_s.reshape(DEC_BATCH, DEC_SEQ, D_MODEL),
            kv_out(wk_p), kv_out(wv_p), st_p.reshape(st_shape),
            kv_out(wk_s), kv_out(wv_s), st_s.reshape(st_shape))
```

```python
import numpy as np

import jax
import jax.numpy as jnp
from jax import lax
from jax.experimental import pallas as pl
from jax.experimental.pallas import tpu as pltpu

F32 = jnp.float32
BF16 = jnp.bfloat16

D_MODEL = 1024
BATCH = 2
SEQ = 8192
DEC_BATCH = 128
DEC_SEQ = 8
PAST_LEN = 16384
ATT_HEADS = 8
ATT_KV_HEADS = 2
ATT_GROUP = ATT_HEADS // ATT_KV_HEADS
ATT_HEAD_DIM = 64
ATT_WIDTH = ATT_HEADS * ATT_HEAD_DIM
KV_WIDTH = ATT_KV_HEADS * ATT_HEAD_DIM
WINDOW = 128
ATT_BLOCK = 128
ROPE_THETA = 10000.0
GLA_HEADS = 4
GLA_WIDTH = D_MODEL // 2
GLA_DV = GLA_WIDTH // GLA_HEADS
GLA_KEY_WIDTH = GLA_WIDTH // 2
GLA_DK = GLA_KEY_WIDTH // GLA_HEADS
GLA_GATE_RANK = 16
GLA_GATE_TAU = 16.0
EPS = 1e-6
NEG = -1e30
LOG2E = 1.4426950408889634
ATT_SCALE = ATT_HEAD_DIM ** -0.5 * LOG2E

LANES = 128
HALF = LANES // 2
MXU_TILE = 256
R_K, R_V, R_ZA = ATT_WIDTH, ATT_WIDTH + KV_WIDTH, ATT_WIDTH + 2 * KV_WIDTH
R_QG = R_ZA + ATT_WIDTH
R_KG = R_QG + GLA_KEY_WIDTH
R_VG = R_KG + GLA_KEY_WIDTH
R_LR = R_VG + GLA_WIDTH
R_ZG = R_LR + GLA_GATE_RANK
R_MA = R_ZG + GLA_WIDTH
R_MG = R_MA + D_MODEL
R_END = R_MG + D_MODEL

T_PROMPT = 512
N_BLOCKS = SEQ // T_PROMPT
GLA_BLOCK = 32
SEQ_GROUP = DEC_BATCH // (BATCH * N_BLOCKS)
T_SAMPLE = SEQ_GROUP * DEC_SEQ
MOD_ROWS = DEC_BATCH + 16
MOD_TILE = 1536
MOD_BLOCK = 8
VMEM_LIMIT = 56 * 1024 * 1024


def _sigmoid(x):
    return 1.0 / (1.0 + jnp.exp2(x * -LOG2E))


def _dot(a, b):
    return jnp.dot(a, b, preferred_element_type=F32)


def _dot_nt(a, b):
    return lax.dot_general(a, b, (((1,), (1,)), ((), ())), preferred_element_type=F32)


def _dot_tn(a, b):
    return lax.dot_general(a, b, (((0,), (0,)), ((), ())), preferred_element_type=F32)


def _split3(x):
    hi = x.astype(BF16)
    r1 = x - hi.astype(F32)
    mid = r1.astype(BF16)
    lo = (r1 - mid.astype(F32)).astype(BF16)
    return hi, mid, lo


def _norm_rope_slab(xs, g, cos, sin):
    lane = lax.broadcasted_iota(jnp.int32, xs.shape, 1)
    lo = lane < HALF
    first = (lane & (HALF // 2)) == 0
    sq = xs * xs
    s_lo = jnp.sum(jnp.where(lo, sq, 0.0), axis=-1, keepdims=True)
    s_hi = jnp.sum(jnp.where(lo, 0.0, sq), axis=-1, keepdims=True)
    inv = jnp.where(lo, lax.rsqrt(s_lo * (1.0 / ATT_HEAD_DIM) + EPS),
                    lax.rsqrt(s_hi * (1.0 / ATT_HEAD_DIM) + EPS))
    xn = xs * inv * g
    swapped = jnp.where(first, pltpu.roll(xn, LANES - HALF // 2, 1), pltpu.roll(xn, HALF // 2, 1))
    return xn * cos + swapped * sin


def _log_decay(lrz, wgate_ref, bgate_ref):
    pre = _dot(lrz.astype(BF16), wgate_ref[...]) + bgate_ref[...]
    log_sig = jnp.minimum(pre, 0.0) - jnp.log(1.0 + jnp.exp(-jnp.abs(pre)))
    return log_sig * (1.0 / GLA_GATE_TAU)


def _head_masks(rows):
    lane = lax.broadcasted_iota(jnp.int32, (rows, GLA_KEY_WIDTH), 1)
    return [(lane >= h * GLA_DK) & (lane < (h + 1) * GLA_DK) for h in range(GLA_HEADS)]


def _softmax_block(s, mask, sink_ref):
    probs, rden = [], []
    for h in range(ATT_HEADS):
        sh = jnp.where(mask, s[h * ATT_BLOCK:(h + 1) * ATT_BLOCK], NEG)
        sink = sink_ref[h] * LOG2E
        m = jnp.maximum(jnp.max(sh, axis=-1, keepdims=True), sink)
        e = jnp.exp2(sh - m)
        rden.append(1.0 / (jnp.sum(e, axis=-1, keepdims=True) + jnp.exp2(sink - m)))
        probs.append(e.astype(BF16))
    return jnp.concatenate(probs, axis=0), rden


def _paired_rows(lo_ref, hi_ref, hi_base):
    rows = []
    for g in range(ATT_GROUP):
        rows.append(lo_ref[g * ATT_HEAD_DIM:(g + 1) * ATT_HEAD_DIM, :])
        rows.append(hi_ref[hi_base + g * ATT_HEAD_DIM:hi_base + (g + 1) * ATT_HEAD_DIM, :])
    return jnp.concatenate(rows, axis=0).astype(BF16)


def _prep_body(cs_ref, cp_ref, w_ref, b_ref, wq_in, wza_lo, wza_hi, wba_in, wbg_in, wout_in,
               mod_ref, wq_out, wza_out, wba_out, wbg_out, wout_out, act_ref):
    half = ATT_WIDTH // 2

    @pl.when(pl.program_id(0) == 0)
    def _():
        cs = cs_ref[...]
        cp = jnp.concatenate([cp_ref[...], jnp.zeros((MOD_ROWS - DEC_BATCH - BATCH, D_MODEL), F32)], axis=0)
        act_ref[0:DEC_BATCH, :] = (cs * _sigmoid(cs)).astype(BF16)
        act_ref[DEC_BATCH:MOD_ROWS, :] = (cp * _sigmoid(cp)).astype(BF16)
        wq_out[...] = _paired_rows(wq_in, wq_in, half)
        wza_out[...] = _paired_rows(wza_lo, wza_hi, 0)
        wba_out[...] = _paired_rows(wba_in, wba_in, half)
        wbg_out[...] = wbg_in[...].astype(BF16)
        wout_out[...] = wout_in[...].astype(BF16)

    mod_ref[...] = _dot(act_ref[...], w_ref[...].astype(BF16)) + b_ref[...]


def _prepare(c_sample, c_prompt, w_ada, b_ada, wt, w_branch_att, w_branch_gla, w_out):
    half = ATT_WIDTH // 2
    whole = lambda shape: pl.BlockSpec(shape, lambda i: (0,) * len(shape))
    return pl.pallas_call(
        _prep_body,
        grid=(3 * D_MODEL // MOD_TILE,),
        in_specs=[
            whole((DEC_BATCH, D_MODEL)),
            whole((BATCH, D_MODEL)),
            pl.BlockSpec((D_MODEL, MOD_TILE), lambda i: (0, i)),
            pl.BlockSpec((1, MOD_TILE), lambda i: (0, i)),
            pl.BlockSpec((ATT_WIDTH, D_MODEL), lambda i: (0, 0)),
            pl.BlockSpec((half, D_MODEL), lambda i: (R_ZA // half, 0)),
            pl.BlockSpec((half, D_MODEL), lambda i: (R_ZA // half + 1, 0)),
            whole((ATT_WIDTH, D_MODEL)), whole((GLA_WIDTH, D_MODEL)), whole((D_MODEL, D_MODEL)),
        ],
        out_specs=[pl.BlockSpec((MOD_ROWS, MOD_TILE), lambda i: (0, i)),
                   whole((ATT_WIDTH, D_MODEL)), whole((ATT_WIDTH, D_MODEL)), whole((ATT_WIDTH, D_MODEL)),
                   whole((GLA_WIDTH, D_MODEL)), whole((D_MODEL, D_MODEL))],
        out_shape=[jax.ShapeDtypeStruct((MOD_ROWS, 3 * D_MODEL), F32),
                   jax.ShapeDtypeStruct((ATT_WIDTH, D_MODEL), BF16), jax.ShapeDtypeStruct((ATT_WIDTH, D_MODEL), BF16),
                   jax.ShapeDtypeStruct((ATT_WIDTH, D_MODEL), BF16), jax.ShapeDtypeStruct((GLA_WIDTH, D_MODEL), BF16),
                   jax.ShapeDtypeStruct((D_MODEL, D_MODEL), BF16)],
        scratch_shapes=[pltpu.VMEM((MOD_ROWS, D_MODEL), BF16)],
        compiler_params=pltpu.CompilerParams(dimension_semantics=("arbitrary",),
                                             vmem_limit_bytes=VMEM_LIMIT),
        name="prepare",
    )(c_sample, c_prompt, w_ada, b_ada, wt, wt, wt, w_branch_att, w_branch_gla, w_out)


def _layer_body(xp_ref, xs_ref, modp_ref, mods_ref, cosp_ref, sinp_ref, coss_ref, sins_ref,
                ng_ref, wq_ref, wza_ref, wt_ref, qng_ref, kng_ref, sink_ref,
                wgate_ref, bgate_ref, glang_ref, wba_ref, wbg_ref, wout_ref,
                ck_ref, cv_ref, s0_ref,
                yp_ref, ys_ref, wk_ref, wv_ref, st_ref, ok_ref, ov_ref, s1_ref,
                kprev_ref, vprev_ref, st_scr, klast_scr, vlast_scr):
    TP, TS, C, L, GS = T_PROMPT, T_SAMPLE, GLA_BLOCK, DEC_SEQ, SEQ_GROUP
    n_att, n_gla = TP // ATT_BLOCK, TP // C
    nrow = ATT_HEADS * L
    j = pl.program_id(1)
    last = pl.num_programs(1) - 1

    @pl.when(j == 0)
    def _():
        kprev_ref[...] = jnp.zeros_like(kprev_ref)
        vprev_ref[...] = jnp.zeros_like(vprev_ref)
        st_scr[...] = jnp.zeros_like(st_scr)

    row_s = ((pl.program_id(0) * pl.num_programs(1) + j) % (MOD_BLOCK // GS)) * GS
    row_p = pl.program_id(0)

    def per_token(a, b):
        return jnp.concatenate([jnp.broadcast_to(mods_ref[pl.ds(row_s + u, 1), a:b], (L, b - a)) for u in range(GS)],
                               axis=0)

    def prompt_mod(k):
        return modp_ref[pl.ds(row_p, 1), k * D_MODEL:(k + 1) * D_MODEL]

    xp, xs = xp_ref[...], xs_ref[...]
    hp = (xp * lax.rsqrt(jnp.mean(xp * xp, axis=-1, keepdims=True) + EPS) * ng_ref[...]
          * (1.0 + prompt_mod(1)) + prompt_mod(0))
    hs = (xs * lax.rsqrt(jnp.mean(xs * xs, axis=-1, keepdims=True) + EPS) * ng_ref[...]
          * (1.0 + per_token(D_MODEL, 2 * D_MODEL)) + per_token(0, D_MODEL))
    h = jnp.concatenate([hp, hs], axis=0).astype(BF16)

    def proj(a, b):
        return _dot_nt(h, wt_ref[a:b, :])

    cos = jnp.concatenate([cosp_ref[...], coss_ref[...]], axis=0)
    sin = jnp.concatenate([sinp_ref[...], sins_ref[...]], axis=0)
    lane = lax.broadcasted_iota(jnp.int32, (ATT_BLOCK, LANES), 1)
    lo = lane < HALF
    v_ = {}

    def a1_qkv():
        kv = proj(R_K, R_ZA)
        v_["q"], v_["k_raw"], v_["v"] = _dot_nt(h, wq_ref[...]), kv[:, :KV_WIDTH], kv[:, KV_WIDTH:]

    def a2_norm_rope():
        q = v_["q"]
        v_["q_slabs"] = [_norm_rope_slab(q[:, i * LANES:(i + 1) * LANES], qng_ref[...], cos, sin)
                         * ATT_SCALE for i in range(ATT_GROUP)]
        k = _norm_rope_slab(v_["k_raw"], kng_ref[...], cos, sin)
        v = v_["v"]
        klast_scr[...] = k[TP - WINDOW:TP]
        vlast_scr[...] = v[TP - WINDOW:TP]
        v_["k16"], v_["v16"] = k[:TP].astype(BF16), v[:TP].astype(BF16)
        v_["k_s"], v_["v_s"] = k[TP:], v[TP:]

    def g1_decay():
        la = _log_decay(proj(R_LR, R_LR + LANES), wgate_ref, bgate_ref)
        v_["la"], v_["la_s"] = la[:TP], la[TP:]

    def g2_qkv():
        qg = proj(R_QG, R_KG) * (GLA_DK ** -0.5)
        kg = proj(R_KG, R_VG)
        vg = proj(R_VG, R_LR)
        v_["qg"], v_["kg"], v_["vg16"] = qg[:TP], kg[:TP], vg[:TP].astype(BF16)
        v_["qg_s"], v_["kg_s"], v_["vg_s"] = qg[TP:], kg[TP:], vg[TP:]

    def a3_masks():
        r = lax.broadcasted_iota(jnp.int32, (ATT_BLOCK, 2 * ATT_BLOCK), 0)
        c = lax.broadcasted_iota(jnp.int32, (ATT_BLOCK, 2 * ATT_BLOCK), 1)
        band = ((c < ATT_BLOCK) & (c >= r)) | ((c >= ATT_BLOCK) & (c - ATT_BLOCK <= r))
        v_["masks"] = [band & ((c >= ATT_BLOCK) | (j > 0))] + [band] * (n_att - 1)
        kprev, vprev = kprev_ref[...], vprev_ref[...]
        v_["kprev"], v_["vprev"] = kprev, vprev
        kprev_ref[...] = v_["k16"][TP - ATT_BLOCK:]
        vprev_ref[...] = v_["v16"][TP - ATT_BLOCK:]

    v_["scores"], v_["vcats"], v_["soft"], v_["o_rows"] = ([None] * n_att for _ in range(4))

    def scores_stage(i):
        def run():
            k16, v16, q_slabs = v_["k16"], v_["v16"], v_["q_slabs"]
            rows = slice(i * ATT_BLOCK, (i + 1) * ATT_BLOCK)
            if i == 0:
                kp, vp = v_["kprev"], v_["vprev"]
            else:
                prev = slice((i - 1) * ATT_BLOCK, i * ATT_BLOCK)
                kp, vp = k16[prev], v16[prev]
            pieces = ([jnp.where(lo, s[rows], 0.0).astype(BF16) for s in q_slabs]
                      + [jnp.where(lo, 0.0, s[rows]).astype(BF16) for s in q_slabs])
            v_["scores"][i] = _dot_nt(jnp.concatenate(pieces, axis=0), jnp.concatenate([kp, k16[rows]], axis=0))
            v_["vcats"][i] = jnp.concatenate([vp, v16[rows]], axis=0)
        return run

    def softmax_stage(i):
        def run():
            v_["soft"][i] = _softmax_block(v_["scores"][i], v_["masks"][i], sink_ref)
        return run

    def values_stage(i):
        def run():
            probs, rden = v_["soft"][i]
            o = _dot(probs, v_["vcats"][i])
            v_["o_rows"][i] = jnp.concatenate(
                [jnp.where(lo, o[g * ATT_BLOCK:(g + 1) * ATT_BLOCK] * rden[g],
                           o[(g + ATT_GROUP) * ATT_BLOCK:(g + ATT_GROUP + 1) * ATT_BLOCK] * rden[g + ATT_GROUP])
                 for g in range(ATT_GROUP)], axis=1)
        return run

    a3 = [scores_stage(i) for i in range(n_att)]
    a4 = [softmax_stage(i) for i in range(n_att)]
    a5 = [values_stage(i) for i in range(n_att)]

    def g3_cumsum():
        tr = lax.broadcasted_iota(jnp.int32, (TP, TP), 0)
        tc = lax.broadcasted_iota(jnp.int32, (TP, TP), 1)
        tri = (((tr // C) == (tc // C)) & (tc <= tr)).astype(BF16)
        hi, mid, lo3 = _split3(v_["la"])
        b = _dot(tri, hi) + _dot(tri, mid) + _dot(tri, lo3)
        v_["b"] = b
        v_["b_last"] = [b[(i + 1) * C - 1:(i + 1) * C] for i in range(n_gla)]
        v_["b_mid"] = [b[i * C + C // 2 - 1:i * C + C // 2] for i in range(n_gla)]

    def g4_decayed():
        b, qg, kg = v_["b"], v_["qg"], v_["kg"]
        b_end = jnp.concatenate([jnp.broadcast_to(bl, (C, GLA_KEY_WIDTH)) for bl in v_["b_last"]], axis=0)
        b_ref = jnp.concatenate([jnp.broadcast_to(bm, (C, GLA_KEY_WIDTH)) for bm in v_["b_mid"]], axis=0)
        q_hat = qg * jnp.exp(b - b_ref)
        v_["k_til"] = (kg * jnp.exp(b_ref - b)).astype(BF16)
        k_dec = kg * jnp.exp(b_end - b)
        hmasks = _head_masks(TP)
        v_["q_m"] = [jnp.where(hm, q_hat, 0.0).astype(BF16) for hm in hmasks]
        v_["k_m"] = [jnp.where(hm, k_dec, 0.0).astype(BF16) for hm in hmasks]

    def g5_states():
        vg16 = v_["vg16"]
        states = [st_scr[...]]
        for i in range(n_gla):
            rows = slice(i * C, (i + 1) * C)
            v_stack = jnp.concatenate([vg16[rows, hh * GLA_DV:(hh + 1) * GLA_DV] for hh in range(GLA_HEADS)],
                                      axis=0)
            k_stack = jnp.concatenate([km[rows] for km in v_["k_m"]], axis=0)
            states.append(states[-1] * jnp.exp(v_["b_last"][i]) + _dot_tn(v_stack, k_stack))
        st_scr[...] = states[-1]
        v_["states"] = states

    def g6_mixed():
        mixed = []
        for i in range(n_gla):
            rows = slice(i * C, (i + 1) * C)
            q_stack = jnp.concatenate([qm[rows] for qm in v_["q_m"]], axis=0)
            s_ref = v_["states"][i] * jnp.exp(v_["b_mid"][i])
            rhs = jnp.concatenate([s_ref.astype(BF16), v_["k_til"][rows]], axis=0)
            mixed.append(_dot_nt(q_stack, rhs))
        v_["mixed"] = mixed

    def g7_outputs():
        sr = lax.broadcasted_iota(jnp.int32, (GLA_HEADS * C, C), 0)
        sc = lax.broadcasted_iota(jnp.int32, (GLA_HEADS * C, C), 1)
        causal = (sr & (C - 1)) >= sc
        vg16 = v_["vg16"]
        og_rows = []
        for i in range(n_gla):
            rows = slice(i * C, (i + 1) * C)
            m_i = v_["mixed"][i]
            a = jnp.where(causal, m_i[:, GLA_DV:], 0.0).astype(BF16)
            og_rows.append(jnp.concatenate(
                [m_i[hh * C:(hh + 1) * C, :GLA_DV]
                 + _dot(a[hh * C:(hh + 1) * C], vg16[rows, hh * GLA_DV:(hh + 1) * GLA_DV])
                 for hh in range(GLA_HEADS)], axis=1))
        v_["o_g"] = jnp.concatenate(og_rows, axis=0)

    seqs = range(GS)

    def s1_inputs():
        lo_s = lax.broadcasted_iota(jnp.int32, (L, LANES), 1) < HALF
        qrows = []
        for u in seqs:
            rows = slice(TP + u * L, TP + (u + 1) * L)
            pieces = ([jnp.where(lo_s, s[rows], 0.0) for s in v_["q_slabs"]]
                      + [jnp.where(lo_s, 0.0, s[rows]) for s in v_["q_slabs"]])
            qrows.append(jnp.concatenate(pieces, axis=0).astype(BF16))
        pad = jnp.zeros((WINDOW - TS, KV_WIDTH), F32)
        k_pad = jnp.concatenate([v_["k_s"], pad], axis=0)
        v_pad = jnp.concatenate([v_["v_s"], pad], axis=0)
        v_["qrows"], v_["k_pad_t"], v_["v_pad_t"], v_["v_pad"] = qrows, k_pad.T, v_pad.T, v_pad
        v_["snew"] = _dot(jnp.concatenate(qrows, axis=0), v_["k_pad_t"].astype(BF16))

    def s2_windows():
        keep_old = lax.broadcasted_iota(jnp.int32, (KV_WIDTH, WINDOW), 1) < WINDOW - L
        for u in seqs:
            shift = WINDOW - L - u * L
            ok_ref[u] = jnp.where(keep_old, pltpu.roll(ck_ref[u], WINDOW - L, 1),
                                  pltpu.roll(v_["k_pad_t"], shift, 1))
            ov_ref[u] = jnp.where(keep_old, pltpu.roll(cv_ref[u], WINDOW - L, 1),
                                  pltpu.roll(v_["v_pad_t"], shift, 1))

    def s3_scores():
        rr = lax.broadcasted_iota(jnp.int32, (nrow, LANES), 0) & (L - 1)
        cc = lax.broadcasted_iota(jnp.int32, (nrow, LANES), 1)
        v_["s_c"] = [jnp.where(cc >= rr, _dot(v_["qrows"][u], ck_ref[u].astype(BF16)), NEG) for u in seqs]
        v_["s_n"] = [jnp.where((cc >= u * L) & (cc <= u * L + rr), v_["snew"][u * nrow:(u + 1) * nrow], NEG)
                     for u in seqs]

    def s4_softmax():
        sink_col = jnp.concatenate([jnp.full((L, 1), sink_ref[hh] * LOG2E, F32) for hh in range(ATT_HEADS)],
                                   axis=0)
        e_c, e_n, rden = [], [], []
        for a, b_ in zip(v_["s_c"], v_["s_n"]):
            m = jnp.maximum(jnp.maximum(jnp.max(a, axis=-1, keepdims=True), jnp.max(b_, axis=-1, keepdims=True)),
                            sink_col)
            ec, en = jnp.exp2(a - m), jnp.exp2(b_ - m)
            rden.append(1.0 / (jnp.sum(ec, axis=-1, keepdims=True) + jnp.sum(en, axis=-1, keepdims=True)
                               + jnp.exp2(sink_col - m)))
            e_c.append(ec.astype(BF16))
            e_n.append(en.astype(BF16))
        v_["e_c"], v_["e_n"], v_["rden_s"] = e_c, e_n, rden

    def s5_values():
        lo_s = lax.broadcasted_iota(jnp.int32, (L, LANES), 1) < HALF
        o_new = _dot(jnp.concatenate(v_["e_n"], axis=0), v_["v_pad"].astype(BF16))
        rows_out = []
        for u in seqs:
            o = ((_dot_nt(v_["e_c"][u], cv_ref[u].astype(BF16)) + o_new[u * nrow:(u + 1) * nrow])
                 * v_["rden_s"][u])
            rows_out.append(jnp.concatenate(
                [jnp.where(lo_s, o[i * L:(i + 1) * L], o[(i + ATT_GROUP) * L:(i + ATT_GROUP + 1) * L])
                 for i in range(ATT_GROUP)], axis=1))
        v_["o_a_s"] = jnp.concatenate(rows_out, axis=0)

    def s6_gla():
        qg, kg, vg, la = v_["qg_s"], v_["kg_s"], v_["vg_s"], v_["la_s"]
        tr = lax.broadcasted_iota(jnp.int32, (TS, TS), 0)
        tc = lax.broadcasted_iota(jnp.int32, (TS, TS), 1)
        same_seq = (tr // L) == (tc // L)
        causal = same_seq & (tc <= tr)
        parts = _split3(la)
        tri, blk = causal.astype(BF16), same_seq.astype(BF16)
        b = _dot(tri, parts[0]) + _dot(tri, parts[1]) + _dot(tri, parts[2])
        b_last = _dot(blk, parts[0]) + _dot(blk, parts[1]) + _dot(blk, parts[2])
        q_hat = qg * jnp.exp(b)
        k_til = (kg * jnp.exp(-b)).astype(BF16)
        k_dec = kg * jnp.exp(b_last - b)
        hmasks = _head_masks(TS)
        q_m = [jnp.where(hm, q_hat, 0.0) for hm in hmasks]
        decay = jnp.exp(b_last)
        zk = jnp.zeros((L, GLA_DK), F32)
        zv = jnp.zeros((L, GLA_DV), F32)
        st0 = [s0_ref[u] for u in seqs]
        inter, upds, dcols = [], [], []
        for u in seqs:
            rows = slice(u * L, (u + 1) * L)
            q_stack = jnp.concatenate([qm[rows] for qm in q_m], axis=0).astype(BF16)
            inter.append(_dot(q_stack, st0[u].astype(BF16)))
            per_head = []
            for hh in range(GLA_HEADS):
                k_rows = jnp.concatenate([k_dec[rows, hh * GLA_DK:(hh + 1) * GLA_DK], zk], axis=0).astype(BF16)
                v_rows = jnp.concatenate([vg[rows, hh * GLA_DV:(hh + 1) * GLA_DV], zv], axis=0).astype(BF16)
                per_head.append(_dot_tn(k_rows, v_rows))
            upds.append(jnp.concatenate(per_head, axis=0))
            dcols.append(decay[rows].T[:, 0:1])
        intra = []
        for hh in range(GLA_HEADS):
            a = jnp.where(causal, _dot_nt(q_m[hh].astype(BF16), k_til), 0.0).astype(BF16)
            intra.append(_dot(a, vg[:, hh * GLA_DV:(hh + 1) * GLA_DV].astype(BF16)))
        for u in seqs:
            s1_ref[u] = dcols[u] * st0[u] + upds[u]
        o_inter = jnp.concatenate(
            [jnp.concatenate([inter[u][hh * L:(hh + 1) * L] for hh in range(GLA_HEADS)], axis=1) for u in seqs],
            axis=0)
        v_["o_g_s"] = o_inter + jnp.concatenate(intra, axis=1)

    gate_parts = {"silu_a": [], "silu_g": [], "merge_a": [], "merge_g": []}

    def gate_chunk(key, w_ref, row, silu):
        def run():
            z = _dot_nt(h, w_ref[row:row + MXU_TILE, :])
            s = _sigmoid(z)
            gate_parts[key].append(z * s if silu else s)
        return run

    t = ([gate_chunk("silu_a", wza_ref, r, True) for r in range(0, ATT_WIDTH, MXU_TILE)]
         + [gate_chunk("silu_g", wt_ref, R_ZG + r, True) for r in range(0, GLA_WIDTH, MXU_TILE)]
         + [gate_chunk("merge_a", wt_ref, R_MA + r, False) for r in range(0, D_MODEL, MXU_TILE)]
         + [gate_chunk("merge_g", wt_ref, R_MG + r, False) for r in range(0, D_MODEL, MXU_TILE)])

    def t_tail():
        silu_a, silu_g, merge_a, merge_g = (jnp.concatenate(gate_parts[key], axis=1)
                                            for key in ("silu_a", "silu_g", "merge_a", "merge_g"))
        o_a = jnp.concatenate(v_["o_rows"] + [v_["o_a_s"]], axis=0) * silu_a
        o_g = jnp.concatenate([v_["o_g"], v_["o_g_s"]], axis=0)
        slabs = []
        for hh in range(GLA_HEADS):
            oh = o_g[:, hh * GLA_DV:(hh + 1) * GLA_DV]
            slabs.append(oh * lax.rsqrt(jnp.mean(oh * oh, axis=-1, keepdims=True) + EPS) * glang_ref[...])
        o_g = jnp.concatenate(slabs, axis=1) * silu_g
        merged = (merge_a * _dot(o_a.astype(BF16), wba_ref[...])
                  + merge_g * _dot(o_g.astype(BF16), wbg_ref[...]))
        out = _dot(merged.astype(BF16), wout_ref[...])
        yp_ref[...] = xp + prompt_mod(2) * out[:TP]
        ys_ref[...] = xs + per_token(2 * D_MODEL, 3 * D_MODEL) * out[TP:]

    for stage in (a1_qkv, g1_decay, g2_qkv, t[0], t[1], g3_cumsum, a2_norm_rope, t[2], s1_inputs, t[3], a3_masks,
                  a3[0], s2_windows, a4[0], t[4], a3[2], a3[1], g4_decayed, a4[1], s3_scores, a5[0], t[5], t[6],
                  a3[3], a4[2], s4_softmax, a5[1], a4[3], t[7], a5[2], g5_states, g6_mixed, s5_values, s6_gla,
                  t[8], a5[3], t[9], t[10], g7_outputs, t[11], t_tail):
        stage()

    @pl.when(j == last)
    def _():
        wk_ref[...] = klast_scr[...].T
        wv_ref[...] = vlast_scr[...].T
        st_ref[...] = st_scr[...].T


def _fused_layer(xp, xs, mod, cos_p, sin_p, cos_s, sin_s, weights, ck, cv, s0):
    GS, TS = SEQ_GROUP, T_SAMPLE

    def const(shape):
        zeros = (0,) * len(shape)
        return pl.BlockSpec(shape, lambda b, j: zeros)

    def per_batch(shape):
        return pl.BlockSpec((None,) + shape, lambda b, j: (b, 0, 0))

    def per_step(shape):
        tail = (0,) * (len(shape) - 1)
        return pl.BlockSpec(shape, lambda b, j: (b * N_BLOCKS + j,) + tail)

    prompt_rows = pl.BlockSpec((None, T_PROMPT, D_MODEL), lambda b, j: (b, j, 0))
    table_rows = pl.BlockSpec((T_PROMPT, LANES), lambda b, j: (j, 0))
    weight_specs = [
        const((1, D_MODEL)),
        const((ATT_WIDTH, D_MODEL)),
        const((ATT_WIDTH, D_MODEL)),
        const((R_END, D_MODEL)),
        const((1, LANES)),
        const((1, LANES)),
        pl.BlockSpec(memory_space=pltpu.SMEM),
        const((LANES, GLA_KEY_WIDTH)),
        const((1, GLA_KEY_WIDTH)),
        const((1, GLA_DV)),
        const((ATT_WIDTH, D_MODEL)),
        const((GLA_WIDTH, D_MODEL)),
        const((D_MODEL, D_MODEL)),
    ]
    return pl.pallas_call(
        _layer_body,
        grid=(BATCH, N_BLOCKS),
        in_specs=[prompt_rows, per_step((TS, D_MODEL)),
                  pl.BlockSpec((MOD_BLOCK, 3 * D_MODEL), lambda b, j: (DEC_BATCH // MOD_BLOCK, 0)),
                  pl.BlockSpec((MOD_BLOCK, 3 * D_MODEL), lambda b, j: ((b * N_BLOCKS + j) * GS // MOD_BLOCK, 0)),
                  table_rows, table_rows, const((TS, LANES)), const((TS, LANES))]
                 + weight_specs
                 + [per_step((GS, KV_WIDTH, WINDOW)), per_step((GS, KV_WIDTH, WINDOW)),
                    per_step((GS, GLA_KEY_WIDTH, GLA_DV))],
        out_specs=[prompt_rows, per_step((TS, D_MODEL)),
                   per_batch((KV_WIDTH, WINDOW)), per_batch((KV_WIDTH, WINDOW)), per_batch((GLA_KEY_WIDTH, GLA_DV)),
                   per_step((GS, KV_WIDTH, WINDOW)), per_step((GS, KV_WIDTH, WINDOW)),
                   per_step((GS, GLA_KEY_WIDTH, GLA_DV))],
        out_shape=[
            jax.ShapeDtypeStruct((BATCH, SEQ, D_MODEL), F32),
            jax.ShapeDtypeStruct((DEC_BATCH * DEC_SEQ, D_MODEL), F32),
            jax.ShapeDtypeStruct((BATCH, KV_WIDTH, WINDOW), F32),
            jax.ShapeDtypeStruct((BATCH, KV_WIDTH, WINDOW), F32),
            jax.ShapeDtypeStruct((BATCH, GLA_KEY_WIDTH, GLA_DV), F32),
            jax.ShapeDtypeStruct((DEC_BATCH, KV_WIDTH, WINDOW), F32),
            jax.ShapeDtypeStruct((DEC_BATCH, KV_WIDTH, WINDOW), F32),
            jax.ShapeDtypeStruct((DEC_BATCH, GLA_KEY_WIDTH, GLA_DV), F32),
        ],
        scratch_shapes=[
            pltpu.VMEM((ATT_BLOCK, KV_WIDTH), BF16),
            pltpu.VMEM((ATT_BLOCK, KV_WIDTH), BF16),
            pltpu.VMEM((GLA_DV, GLA_KEY_WIDTH), F32),
            pltpu.VMEM((WINDOW, KV_WIDTH), F32),
            pltpu.VMEM((WINDOW, KV_WIDTH), F32),
        ],
        compiler_params=pltpu.CompilerParams(dimension_semantics=("arbitrary", "arbitrary"),
                                             vmem_limit_bytes=VMEM_LIMIT),
        name="fused_layer",
    )(xp, xs, mod, mod, cos_p, sin_p, cos_s, sin_s, *weights, ck, cv, s0)


def _rope_tables(pos):
    half = ATT_HEAD_DIM // 2
    inv = 1.0 / (ROPE_THETA ** (np.arange(half, dtype=np.float64) / half))
    ang = pos.astype(np.float64)[:, None] * inv[None, :]
    c, s = np.cos(ang), np.sin(ang)
    return (np.tile(c, (1, 4)).astype(np.float32),
            np.concatenate([-s, s, -s, s], axis=1).astype(np.float32))


def kernel(x_prompt, x_sample, cache_win_k, cache_win_v, state_gla, c_prompt, c_sample, norm_g, w_ada, b_ada, w_in, q_norm_g, k_norm_g, attn_sinks, w_gla_gate, b_gla_gate, gla_norm_g, w_branch_att, w_branch_gla, w_out):
    assert w_in.shape == (1, D_MODEL, R_END), "single-layer trunk"
    wt = jnp.swapaxes(w_in[0], 0, 1)
    wgate = jnp.concatenate([w_gla_gate[0], jnp.zeros((LANES - GLA_GATE_RANK, GLA_KEY_WIDTH), F32)],
                            axis=0).astype(BF16)
    mod, wq16, wza16, wba16, wbg16, wout16 = _prepare(c_sample, c_prompt, w_ada[0], b_ada[0][None, :], wt,
                                                      w_branch_att[0], w_branch_gla[0], w_out[0])
    weights = (
        norm_g[0][None, :], wq16, wza16, wt.astype(BF16),
        jnp.tile(q_norm_g[0], 2)[None, :], jnp.tile(k_norm_g[0], 2)[None, :],
        attn_sinks[0], wgate, b_gla_gate[0][None, :], gla_norm_g[0][None, :], wba16, wbg16, wout16,
    )

    cos_p, sin_p = _rope_tables(np.arange(SEQ))
    cos_s, sin_s = _rope_tables(PAST_LEN + np.arange(DEC_SEQ))
    cos_s, sin_s = np.tile(cos_s, (SEQ_GROUP, 1)), np.tile(sin_s, (SEQ_GROUP, 1))

    def kv_rows(c):
        return jnp.transpose(c, (0, 2, 3, 1)).reshape(c.shape[0], KV_WIDTH, WINDOW)

    def kv_out(c):
        return jnp.transpose(c.reshape(c.shape[0], ATT_KV_HEADS, ATT_HEAD_DIM, WINDOW), (0, 3, 1, 2))[None]

    y_p, y_s, wk_p, wv_p, st_p, wk_s, wv_s, st_s = _fused_layer(
        x_prompt, x_sample.reshape(DEC_BATCH * DEC_SEQ, D_MODEL), mod, cos_p, sin_p, cos_s, sin_s, weights,
        kv_rows(cache_win_k[0]), kv_rows(cache_win_v[0]),
        state_gla[0].reshape(DEC_BATCH, GLA_KEY_WIDTH, GLA_DV))

    st_shape = (1, -1, GLA_HEADS, GLA_DK, GLA_DV)
    return (y_p, y_s.reshape(DEC_BATCH, DEC_SEQ, D_MODEL),
            kv_out(wk_p), kv_out(wv_p), st_p.reshape(st_shape),
            kv_out(wk_s), kv_out(wv_s), st_s.reshape(st_shape))
```

```python
import numpy as np

import jax
import jax.numpy as jnp
from jax import lax
from jax.experimental import pallas as pl
from jax.experimental.pallas import tpu as pltpu

F32 = jnp.float32
BF16 = jnp.bfloat16

D_MODEL = 1024
BATCH = 2
SEQ = 8192
DEC_BATCH = 128
DEC_SEQ = 8
PAST_LEN = 16384
ATT_HEADS = 8
ATT_KV_HEADS = 2
ATT_GROUP = ATT_HEADS // ATT_KV_HEADS
ATT_HEAD_DIM = 64
ATT_WIDTH = ATT_HEADS * ATT_HEAD_DIM
KV_WIDTH = ATT_KV_HEADS * ATT_HEAD_DIM
WINDOW = 128
ATT_BLOCK = 128
ROPE_THETA = 10000.0
GLA_HEADS = 4
GLA_WIDTH = D_MODEL // 2
GLA_DV = GLA_WIDTH // GLA_HEADS
GLA_KEY_WIDTH = GLA_WIDTH // 2
GLA_DK = GLA_KEY_WIDTH // GLA_HEADS
GLA_GATE_RANK = 16
GLA_GATE_TAU = 16.0
EPS = 1e-6
NEG = -1e30
LOG2E = 1.4426950408889634
ATT_SCALE = ATT_HEAD_DIM ** -0.5 * LOG2E

LANES = 128
HALF = LANES // 2
MXU_TILE = 256
R_K, R_V, R_ZA = ATT_WIDTH, ATT_WIDTH + KV_WIDTH, ATT_WIDTH + 2 * KV_WIDTH
R_QG = R_ZA + ATT_WIDTH
R_KG = R_QG + GLA_KEY_WIDTH
R_VG = R_KG + GLA_KEY_WIDTH
R_LR = R_VG + GLA_WIDTH
R_ZG = R_LR + GLA_GATE_RANK
R_MA = R_ZG + GLA_WIDTH
R_MG = R_MA + D_MODEL
R_END = R_MG + D_MODEL

T_PROMPT = 512
N_BLOCKS = SEQ // T_PROMPT
GLA_BLOCK = 64
SEQ_GROUP = DEC_BATCH // (BATCH * N_BLOCKS)
T_SAMPLE = SEQ_GROUP * DEC_SEQ
MOD_ROWS = DEC_BATCH + 16
MOD_TILE = 1536
MOD_BLOCK = 8
VMEM_LIMIT = 56 * 1024 * 1024


def _sigmoid(x):
    return 1.0 / (1.0 + jnp.exp2(x * -LOG2E))


def _dot(a, b):
    return jnp.dot(a, b, preferred_element_type=F32)


def _dot_nt(a, b):
    return lax.dot_general(a, b, (((1,), (1,)), ((), ())), preferred_element_type=F32)


def _dot_tn(a, b):
    return lax.dot_general(a, b, (((0,), (0,)), ((), ())), preferred_element_type=F32)


def _split3(x):
    hi = x.astype(BF16)
    r1 = x - hi.astype(F32)
    mid = r1.astype(BF16)
    lo = (r1 - mid.astype(F32)).astype(BF16)
    return hi, mid, lo


def _norm_rope_slab(xs, g, cos, sin):
    lane = lax.broadcasted_iota(jnp.int32, xs.shape, 1)
    lo = lane < HALF
    first = (lane & (HALF // 2)) == 0
    sq = xs * xs
    s_lo = jnp.sum(jnp.where(lo, sq, 0.0), axis=-1, keepdims=True)
    s_hi = jnp.sum(jnp.where(lo, 0.0, sq), axis=-1, keepdims=True)
    inv = jnp.where(lo, lax.rsqrt(s_lo * (1.0 / ATT_HEAD_DIM) + EPS),
                    lax.rsqrt(s_hi * (1.0 / ATT_HEAD_DIM) + EPS))
    xn = xs * inv * g
    swapped = jnp.where(first, pltpu.roll(xn, LANES - HALF // 2, 1), pltpu.roll(xn, HALF // 2, 1))
    return xn * cos + swapped * sin


def _log_decay(lrz, wgate_ref, bgate_ref):
    pre = _dot(lrz.astype(BF16), wgate_ref[...]) + bgate_ref[...]
    log_sig = jnp.minimum(pre, 0.0) - jnp.log(1.0 + jnp.exp(-jnp.abs(pre)))
    return log_sig * (1.0 / GLA_GATE_TAU)


def _head_masks(rows):
    lane = lax.broadcasted_iota(jnp.int32, (rows, GLA_KEY_WIDTH), 1)
    return [(lane >= h * GLA_DK) & (lane < (h + 1) * GLA_DK) for h in range(GLA_HEADS)]


def _softmax_block(s, mask, sink_ref):
    probs, rden = [], []
    for h in range(ATT_HEADS):
        sh = jnp.where(mask, s[h * ATT_BLOCK:(h + 1) * ATT_BLOCK], NEG)
        sink = sink_ref[h] * LOG2E
        m = jnp.maximum(jnp.max(sh, axis=-1, keepdims=True), sink)
        rden.append(jnp.exp2(sink - m))
        probs.append(jnp.exp2(sh - m).astype(BF16))
    return jnp.concatenate(probs, axis=0), rden


def _paired_rows(lo_ref, hi_ref, hi_base):
    rows = []
    for g in range(ATT_GROUP):
        rows.append(lo_ref[g * ATT_HEAD_DIM:(g + 1) * ATT_HEAD_DIM, :])
        rows.append(hi_ref[hi_base + g * ATT_HEAD_DIM:hi_base + (g + 1) * ATT_HEAD_DIM, :])
    return jnp.concatenate(rows, axis=0).astype(BF16)


def _prep_body(cs_ref, cp_ref, w_ref, b_ref, wq_in, wza_lo, wza_hi, wba_in, wbg_in, wout_in,
               mod_ref, wq_out, wza_out, wba_out, wbg_out, wout_out, act_ref):
    half = ATT_WIDTH // 2

    @pl.when(pl.program_id(0) == 0)
    def _():
        cs = cs_ref[...]
        cp = jnp.concatenate([cp_ref[...], jnp.zeros((MOD_ROWS - DEC_BATCH - BATCH, D_MODEL), F32)], axis=0)
        act_ref[0:DEC_BATCH, :] = (cs * _sigmoid(cs)).astype(BF16)
        act_ref[DEC_BATCH:MOD_ROWS, :] = (cp * _sigmoid(cp)).astype(BF16)
        wq_out[...] = _paired_rows(wq_in, wq_in, half)
        wza_out[...] = _paired_rows(wza_lo, wza_hi, 0)
        wba_out[...] = _paired_rows(wba_in, wba_in, half)
        wbg_out[...] = wbg_in[...].astype(BF16)
        wout_out[...] = wout_in[...].astype(BF16)

    mod_ref[...] = _dot(act_ref[...], w_ref[...].astype(BF16)) + b_ref[...]


def _prepare(c_sample, c_prompt, w_ada, b_ada, wt, w_branch_att, w_branch_gla, w_out):
    half = ATT_WIDTH // 2
    whole = lambda shape: pl.BlockSpec(shape, lambda i: (0,) * len(shape))
    return pl.pallas_call(
        _prep_body,
        grid=(3 * D_MODEL // MOD_TILE,),
        in_specs=[
            whole((DEC_BATCH, D_MODEL)),
            whole((BATCH, D_MODEL)),
            pl.BlockSpec((D_MODEL, MOD_TILE), lambda i: (0, i)),
            pl.BlockSpec((1, MOD_TILE), lambda i: (0, i)),
            pl.BlockSpec((ATT_WIDTH, D_MODEL), lambda i: (0, 0)),
            pl.BlockSpec((half, D_MODEL), lambda i: (R_ZA // half, 0)),
            pl.BlockSpec((half, D_MODEL), lambda i: (R_ZA // half + 1, 0)),
            whole((ATT_WIDTH, D_MODEL)), whole((GLA_WIDTH, D_MODEL)), whole((D_MODEL, D_MODEL)),
        ],
        out_specs=[pl.BlockSpec((MOD_ROWS, MOD_TILE), lambda i: (0, i)),
                   whole((ATT_WIDTH, D_MODEL)), whole((ATT_WIDTH, D_MODEL)), whole((ATT_WIDTH, D_MODEL)),
                   whole((GLA_WIDTH, D_MODEL)), whole((D_MODEL, D_MODEL))],
        out_shape=[jax.ShapeDtypeStruct((MOD_ROWS, 3 * D_MODEL), F32),
                   jax.ShapeDtypeStruct((ATT_WIDTH, D_MODEL), BF16), jax.ShapeDtypeStruct((ATT_WIDTH, D_MODEL), BF16),
                   jax.ShapeDtypeStruct((ATT_WIDTH, D_MODEL), BF16), jax.ShapeDtypeStruct((GLA_WIDTH, D_MODEL), BF16),
                   jax.ShapeDtypeStruct((D_MODEL, D_MODEL), BF16)],
        scratch_shapes=[pltpu.VMEM((MOD_ROWS, D_MODEL), BF16)],
        compiler_params=pltpu.CompilerParams(dimension_semantics=("arbitrary",),
                                             vmem_limit_bytes=VMEM_LIMIT),
        name="prepare",
    )(c_sample, c_prompt, w_ada, b_ada, wt, wt, wt, w_branch_att, w_branch_gla, w_out)


def _layer_body(xp_ref, xs_ref, modp_ref, mods_ref, cosp_ref, sinp_ref, coss_ref, sins_ref,
                ng_ref, wq_ref, wza_ref, wt_ref, qng_ref, kng_ref, sink_ref,
                wgate_ref, bgate_ref, glang_ref, wba_ref, wbg_ref, wout_ref,
                ck_ref, cv_ref, s0_ref,
                yp_ref, ys_ref, wk_ref, wv_ref, st_ref, ok_ref, ov_ref, s1_ref,
                kprev_ref, vprev_ref, st_scr, klast_scr, vlast_scr):
    TP, TS, C, L, GS = T_PROMPT, T_SAMPLE, GLA_BLOCK, DEC_SEQ, SEQ_GROUP
    n_att, n_gla = TP // ATT_BLOCK, TP // C
    nrow = ATT_HEADS * L
    j = pl.program_id(1)
    last = pl.num_programs(1) - 1

    @pl.when(j == 0)
    def _():
        kprev_ref[...] = jnp.zeros_like(kprev_ref)
        vprev_ref[...] = jnp.zeros_like(vprev_ref)
        st_scr[...] = jnp.zeros_like(st_scr)

    row_s = ((pl.program_id(0) * pl.num_programs(1) + j) % (MOD_BLOCK // GS)) * GS
    row_p = pl.program_id(0)

    def per_token(a, b):
        return jnp.concatenate([jnp.broadcast_to(mods_ref[pl.ds(row_s + u, 1), a:b], (L, b - a)) for u in range(GS)],
                               axis=0)

    def prompt_mod(k):
        return modp_ref[pl.ds(row_p, 1), k * D_MODEL:(k + 1) * D_MODEL]

    xp, xs = xp_ref[...], xs_ref[...]
    hp = (xp * lax.rsqrt(jnp.mean(xp * xp, axis=-1, keepdims=True) + EPS) * ng_ref[...]
          * (1.0 + prompt_mod(1)) + prompt_mod(0))
    hs = (xs * lax.rsqrt(jnp.mean(xs * xs, axis=-1, keepdims=True) + EPS) * ng_ref[...]
          * (1.0 + per_token(D_MODEL, 2 * D_MODEL)) + per_token(0, D_MODEL))
    h = jnp.concatenate([hp, hs], axis=0).astype(BF16)

    def proj(a, b):
        return _dot_nt(h, wt_ref[a:b, :])

    cos = jnp.concatenate([cosp_ref[...], coss_ref[...]], axis=0)
    sin = jnp.concatenate([sinp_ref[...], sins_ref[...]], axis=0)
    lane = lax.broadcasted_iota(jnp.int32, (ATT_BLOCK, LANES), 1)
    lo = lane < HALF
    v_ = {}

    def a1_qkv():
        kv = proj(R_K, R_ZA)
        v_["q"], v_["k_raw"], v_["v"] = _dot_nt(h, wq_ref[...]), kv[:, :KV_WIDTH], kv[:, KV_WIDTH:]

    def a2_norm_rope():
        q = v_["q"]
        v_["q_slabs"] = [_norm_rope_slab(q[:, i * LANES:(i + 1) * LANES], qng_ref[...], cos, sin)
                         * ATT_SCALE for i in range(ATT_GROUP)]
        k = _norm_rope_slab(v_["k_raw"], kng_ref[...], cos, sin)
        v = v_["v"]
        klast_scr[...] = k[TP - WINDOW:TP]
        vlast_scr[...] = v[TP - WINDOW:TP]
        v_["k16"], v_["v16"] = k[:TP].astype(BF16), v[:TP].astype(BF16)
        v_["k_s"], v_["v_s"] = k[TP:], v[TP:]

    def g1_decay():
        la = _log_decay(proj(R_LR, R_LR + LANES), wgate_ref, bgate_ref)
        v_["la"], v_["la_s"] = la[:TP], la[TP:]

    def g2_qkv():
        qg = proj(R_QG, R_KG) * (GLA_DK ** -0.5)
        kg = proj(R_KG, R_VG)
        vg = proj(R_VG, R_LR)
        v_["qg"], v_["kg"], v_["vg16"] = qg[:TP], kg[:TP], vg[:TP].astype(BF16)
        v_["qg_s"], v_["kg_s"], v_["vg_s"] = qg[TP:], kg[TP:], vg[TP:]

    def a3_masks():
        r = lax.broadcasted_iota(jnp.int32, (ATT_BLOCK, 2 * ATT_BLOCK), 0)
        c = lax.broadcasted_iota(jnp.int32, (ATT_BLOCK, 2 * ATT_BLOCK), 1)
        band = ((c < ATT_BLOCK) & (c >= r)) | ((c >= ATT_BLOCK) & (c - ATT_BLOCK <= r))
        v_["masks"] = [band & ((c >= ATT_BLOCK) | (j > 0))] + [band] * (n_att - 1)
        kprev, vprev = kprev_ref[...], vprev_ref[...]
        v_["kprev"], v_["vprev"] = kprev, vprev
        kprev_ref[...] = v_["k16"][TP - ATT_BLOCK:]
        vprev_ref[...] = v_["v16"][TP - ATT_BLOCK:]

    v_["scores"], v_["vcats"], v_["soft"], v_["o_rows"] = ([None] * n_att for _ in range(4))

    def scores_stage(i):
        def run():
            k16, v16, q_slabs = v_["k16"], v_["v16"], v_["q_slabs"]
            rows = slice(i * ATT_BLOCK, (i + 1) * ATT_BLOCK)
            if i == 0:
                kp, vp = v_["kprev"], v_["vprev"]
            else:
                prev = slice((i - 1) * ATT_BLOCK, i * ATT_BLOCK)
                kp, vp = k16[prev], v16[prev]
            pieces = ([jnp.where(lo, s[rows], 0.0).astype(BF16) for s in q_slabs]
                      + [jnp.where(lo, 0.0, s[rows]).astype(BF16) for s in q_slabs])
            v_["scores"][i] = _dot_nt(jnp.concatenate(pieces, axis=0), jnp.concatenate([kp, k16[rows]], axis=0))
            vcat = jnp.concatenate([vp, v16[rows]], axis=0)
            v_["vcats"][i] = jnp.concatenate([vcat, jnp.ones_like(vcat)], axis=1)
        return run

    def softmax_stage(i):
        def run():
            v_["soft"][i] = _softmax_block(v_["scores"][i], v_["masks"][i], sink_ref)
        return run

    def values_stage(i):
        def run():
            probs, sink_term = v_["soft"][i]
            o = _dot(probs, v_["vcats"][i])

            def head(hh):
                blk = o[hh * ATT_BLOCK:(hh + 1) * ATT_BLOCK]
                return blk[:, :KV_WIDTH] * (1.0 / (blk[:, KV_WIDTH:] + sink_term[hh]))

            v_["o_rows"][i] = jnp.concatenate(
                [jnp.where(lo, head(g), head(g + ATT_GROUP)) for g in range(ATT_GROUP)], axis=1)
        return run

    a3 = [scores_stage(i) for i in range(n_att)]
    a4 = [softmax_stage(i) for i in range(n_att)]
    a5 = [values_stage(i) for i in range(n_att)]

    def g3_cumsum():
        tr = lax.broadcasted_iota(jnp.int32, (TP, TP), 0)
        tc = lax.broadcasted_iota(jnp.int32, (TP, TP), 1)
        tri = (((tr // C) == (tc // C)) & (tc <= tr)).astype(BF16)
        hi, mid, lo3 = _split3(v_["la"])
        b = _dot(tri, hi) + _dot(tri, mid) + _dot(tri, lo3)
        v_["b"] = b
        v_["b_last"] = [b[(i + 1) * C - 1:(i + 1) * C] for i in range(n_gla)]
        v_["b_mid"] = [b[i * C + C // 2 - 1:i * C + C // 2] for i in range(n_gla)]

    def g4_decayed():
        b, qg, kg = v_["b"], v_["qg"], v_["kg"]
        b_end = jnp.concatenate([jnp.broadcast_to(bl, (C, GLA_KEY_WIDTH)) for bl in v_["b_last"]], axis=0)
        b_ref = jnp.concatenate([jnp.broadcast_to(bm, (C, GLA_KEY_WIDTH)) for bm in v_["b_mid"]], axis=0)
        q_hat = qg * jnp.exp(b - b_ref)
        v_["k_til"] = (kg * jnp.exp(b_ref - b)).astype(BF16)
        k_dec = kg * jnp.exp(b_end - b)
        hmasks = _head_masks(TP)
        v_["q_m"] = [jnp.where(hm, q_hat, 0.0).astype(BF16) for hm in hmasks]
        v_["k_m"] = [jnp.where(hm, k_dec, 0.0).astype(BF16) for hm in hmasks]

    def g5_states():
        vg16 = v_["vg16"]
        states = [st_scr[...]]
        for i in range(n_gla):
            rows = slice(i * C, (i + 1) * C)
            v_stack = jnp.concatenate([vg16[rows, hh * GLA_DV:(hh + 1) * GLA_DV] for hh in range(GLA_HEADS)],
                                      axis=0)
            k_stack = jnp.concatenate([km[rows] for km in v_["k_m"]], axis=0)
            states.append(states[-1] * jnp.exp(v_["b_last"][i]) + _dot_tn(v_stack, k_stack))
        st_scr[...] = states[-1]
        v_["states"] = states

    def g6_mixed():
        mixed = []
        for i in range(n_gla):
            rows = slice(i * C, (i + 1) * C)
            q_stack = jnp.concatenate([qm[rows] for qm in v_["q_m"]], axis=0)
            s_ref = v_["states"][i] * jnp.exp(v_["b_mid"][i])
            rhs = jnp.concatenate([s_ref.astype(BF16), v_["k_til"][rows]], axis=0)
            mixed.append(_dot_nt(q_stack, rhs))
        v_["mixed"] = mixed

    def g7_outputs():
        sr = lax.broadcasted_iota(jnp.int32, (GLA_HEADS * C, C), 0)
        sc = lax.broadcasted_iota(jnp.int32, (GLA_HEADS * C, C), 1)
        causal = (sr & (C - 1)) >= sc
        vg16 = v_["vg16"]
        og_rows = []
        for i in range(n_gla):
            rows = slice(i * C, (i + 1) * C)
            m_i = v_["mixed"][i]
            a = jnp.where(causal, m_i[:, GLA_DV:], 0.0).astype(BF16)
            og_rows.append(jnp.concatenate(
                [m_i[hh * C:(hh + 1) * C, :GLA_DV]
                 + _dot(a[hh * C:(hh + 1) * C], vg16[rows, hh * GLA_DV:(hh + 1) * GLA_DV])
                 for hh in range(GLA_HEADS)], axis=1))
        v_["o_g"] = jnp.concatenate(og_rows, axis=0)

    seqs = range(GS)

    def s1_inputs():
        lo_s = lax.broadcasted_iota(jnp.int32, (L, LANES), 1) < HALF
        qrows = []
        for u in seqs:
            rows = slice(TP + u * L, TP + (u + 1) * L)
            pieces = ([jnp.where(lo_s, s[rows], 0.0) for s in v_["q_slabs"]]
                      + [jnp.where(lo_s, 0.0, s[rows]) for s in v_["q_slabs"]])
            qrows.append(jnp.concatenate(pieces, axis=0).astype(BF16))
        pad = jnp.zeros((WINDOW - TS, KV_WIDTH), F32)
        k_pad = jnp.concatenate([v_["k_s"], pad], axis=0)
        v_pad = jnp.concatenate([v_["v_s"], pad], axis=0)
        v_["qrows"], v_["k_pad_t"], v_["v_pad_t"], v_["v_pad"] = qrows, k_pad.T, v_pad.T, v_pad
        v_["snew"] = _dot(jnp.concatenate(qrows, axis=0), v_["k_pad_t"].astype(BF16))

    def s2_windows():
        keep_old = lax.broadcasted_iota(jnp.int32, (KV_WIDTH, WINDOW), 1) < WINDOW - L
        for u in seqs:
            shift = WINDOW - L - u * L
            ok_ref[u] = jnp.where(keep_old, pltpu.roll(ck_ref[u], WINDOW - L, 1),
                                  pltpu.roll(v_["k_pad_t"], shift, 1))
            ov_ref[u] = jnp.where(keep_old, pltpu.roll(cv_ref[u], WINDOW - L, 1),
                                  pltpu.roll(v_["v_pad_t"], shift, 1))

    def s3_scores():
        rr = lax.broadcasted_iota(jnp.int32, (nrow, LANES), 0) & (L - 1)
        cc = lax.broadcasted_iota(jnp.int32, (nrow, LANES), 1)
        v_["s_c"] = [jnp.where(cc >= rr, _dot(v_["qrows"][u], ck_ref[u].astype(BF16)), NEG) for u in seqs]
        v_["s_n"] = [jnp.where((cc >= u * L) & (cc <= u * L + rr), v_["snew"][u * nrow:(u + 1) * nrow], NEG)
                     for u in seqs]

    def s4_softmax():
        sink_col = jnp.concatenate([jnp.full((L, 1), sink_ref[hh] * LOG2E, F32) for hh in range(ATT_HEADS)],
                                   axis=0)
        e_c, e_n, rden = [], [], []
        for a, b_ in zip(v_["s_c"], v_["s_n"]):
            m = jnp.maximum(jnp.maximum(jnp.max(a, axis=-1, keepdims=True), jnp.max(b_, axis=-1, keepdims=True)),
                            sink_col)
            ec, en = jnp.exp2(a - m), jnp.exp2(b_ - m)
            rden.append(1.0 / (jnp.sum(ec, axis=-1, keepdims=True) + jnp.sum(en, axis=-1, keepdims=True)
                               + jnp.exp2(sink_col - m)))
            e_c.append(ec.astype(BF16))
            e_n.append(en.astype(BF16))
        v_["e_c"], v_["e_n"], v_["rden_s"] = e_c, e_n, rden

    def s5_values():
        lo_s = lax.broadcasted_iota(jnp.int32, (L, LANES), 1) < HALF
        o_new = _dot(jnp.concatenate(v_["e_n"], axis=0), v_["v_pad"].astype(BF16))
        rows_out = []
        for u in seqs:
            o = ((_dot_nt(v_["e_c"][u], cv_ref[u].astype(BF16)) + o_new[u * nrow:(u + 1) * nrow])
                 * v_["rden_s"][u])
            rows_out.append(jnp.concatenate(
                [jnp.where(lo_s, o[i * L:(i + 1) * L], o[(i + ATT_GROUP) * L:(i + ATT_GROUP + 1) * L])
                 for i in range(ATT_GROUP)], axis=1))
        v_["o_a_s"] = jnp.concatenate(rows_out, axis=0)

    def s6_gla():
        qg, kg, vg, la = v_["qg_s"], v_["kg_s"], v_["vg_s"], v_["la_s"]
        tr = lax.broadcasted_iota(jnp.int32, (TS, TS), 0)
        tc = lax.broadcasted_iota(jnp.int32, (TS, TS), 1)
        same_seq = (tr // L) == (tc // L)
        causal = same_seq & (tc <= tr)
        parts = _split3(la)
        tri, blk = causal.astype(BF16), same_seq.astype(BF16)
        b = _dot(tri, parts[0]) + _dot(tri, parts[1]) + _dot(tri, parts[2])
        b_last = _dot(blk, parts[0]) + _dot(blk, parts[1]) + _dot(blk, parts[2])
        q_hat = qg * jnp.exp(b)
        k_til = (kg * jnp.exp(-b)).astype(BF16)
        k_dec = kg * jnp.exp(b_last - b)
        hmasks = _head_masks(TS)
        q_m = [jnp.where(hm, q_hat, 0.0) for hm in hmasks]
        decay = jnp.exp(b_last)
        zk = jnp.zeros((L, GLA_DK), F32)
        zv = jnp.zeros((L, GLA_DV), F32)
        st0 = [s0_ref[u] for u in seqs]
        inter, upds, dcols = [], [], []
        for u in seqs:
            rows = slice(u * L, (u + 1) * L)
            q_stack = jnp.concatenate([qm[rows] for qm in q_m], axis=0).astype(BF16)
            inter.append(_dot(q_stack, st0[u].astype(BF16)))
            per_head = []
            for hh in range(GLA_HEADS):
                k_rows = jnp.concatenate([k_dec[rows, hh * GLA_DK:(hh + 1) * GLA_DK], zk], axis=0).astype(BF16)
                v_rows = jnp.concatenate([vg[rows, hh * GLA_DV:(hh + 1) * GLA_DV], zv], axis=0).astype(BF16)
                per_head.append(_dot_tn(k_rows, v_rows))
            upds.append(jnp.concatenate(per_head, axis=0))
            dcols.append(decay[rows].T[:, 0:1])
        intra = []
        for hh in range(GLA_HEADS):
            a = jnp.where(causal, _dot_nt(q_m[hh].astype(BF16), k_til), 0.0).astype(BF16)
            intra.append(_dot(a, vg[:, hh * GLA_DV:(hh + 1) * GLA_DV].astype(BF16)))
        for u in seqs:
            s1_ref[u] = dcols[u] * st0[u] + upds[u]
        o_inter = jnp.concatenate(
            [jnp.concatenate([inter[u][hh * L:(hh + 1) * L] for hh in range(GLA_HEADS)], axis=1) for u in seqs],
            axis=0)
        v_["o_g_s"] = o_inter + jnp.concatenate(intra, axis=1)

    gate_parts = {"silu_a": [], "silu_g": [], "merge_a": [], "merge_g": []}

    def gate_chunk(key, w_ref, row, silu):
        def run():
            z = _dot_nt(h, w_ref[row:row + MXU_TILE, :])
            s = _sigmoid(z)
            gate_parts[key].append(z * s if silu else s)
        return run

    t = ([gate_chunk("silu_a", wza_ref, r, True) for r in range(0, ATT_WIDTH, MXU_TILE)]
         + [gate_chunk("silu_g", wt_ref, R_ZG + r, True) for r in range(0, GLA_WIDTH, MXU_TILE)]
         + [gate_chunk("merge_a", wt_ref, R_MA + r, False) for r in range(0, D_MODEL, MXU_TILE)]
         + [gate_chunk("merge_g", wt_ref, R_MG + r, False) for r in range(0, D_MODEL, MXU_TILE)])

    def t_tail():
        silu_a, silu_g, merge_a, merge_g = (jnp.concatenate(gate_parts[key], axis=1)
                                            for key in ("silu_a", "silu_g", "merge_a", "merge_g"))
        o_a = jnp.concatenate(v_["o_rows"] + [v_["o_a_s"]], axis=0) * silu_a
        o_g = jnp.concatenate([v_["o_g"], v_["o_g_s"]], axis=0)
        slabs = []
        for hh in range(GLA_HEADS):
            oh = o_g[:, hh * GLA_DV:(hh + 1) * GLA_DV]
            slabs.append(oh * lax.rsqrt(jnp.mean(oh * oh, axis=-1, keepdims=True) + EPS) * glang_ref[...])
        o_g = jnp.concatenate(slabs, axis=1) * silu_g
        merged = (merge_a * _dot(o_a.astype(BF16), wba_ref[...])
                  + merge_g * _dot(o_g.astype(BF16), wbg_ref[...]))
        out = _dot(merged.astype(BF16), wout_ref[...])
        yp_ref[...] = xp + prompt_mod(2) * out[:TP]
        ys_ref[...] = xs + per_token(2 * D_MODEL, 3 * D_MODEL) * out[TP:]

    for stage in (a1_qkv, g1_decay, g2_qkv, t[0], t[1], g3_cumsum, a2_norm_rope, t[2], s1_inputs, t[3], a3_masks,
                  a3[0], s2_windows, a4[0], t[4], a3[2], a3[1], g4_decayed, a4[1], s3_scores, a5[0], t[5], t[6],
                  a3[3], a4[2], s4_softmax, a5[1], a4[3], t[7], a5[2], g5_states, g6_mixed, s5_values, s6_gla,
                  t[8], a5[3], t[9], t[10], g7_outputs, t[11], t_tail):
        stage()

    @pl.when(j == last)
    def _():
        wk_ref[...] = klast_scr[...].T
        wv_ref[...] = vlast_scr[...].T
        st_ref[...] = st_scr[...].T


def _fused_layer(xp, xs, mod, cos_p, sin_p, cos_s, sin_s, weights, ck, cv, s0):
    GS, TS = SEQ_GROUP, T_SAMPLE

    def const(shape):
        zeros = (0,) * len(shape)
        return pl.BlockSpec(shape, lambda b, j: zeros)

    def per_batch(shape):
        return pl.BlockSpec((None,) + shape, lambda b, j: (b, 0, 0))

    def per_step(shape):
        tail = (0,) * (len(shape) - 1)
        return pl.BlockSpec(shape, lambda b, j: (b * N_BLOCKS + j,) + tail)

    prompt_rows = pl.BlockSpec((None, T_PROMPT, D_MODEL), lambda b, j: (b, j, 0))
    table_rows = pl.BlockSpec((T_PROMPT, LANES), lambda b, j: (j, 0))
    weight_specs = [
        const((1, D_MODEL)),
        const((ATT_WIDTH, D_MODEL)),
        const((ATT_WIDTH, D_MODEL)),
        const((R_END, D_MODEL)),
        const((1, LANES)),
        const((1, LANES)),
        pl.BlockSpec(memory_space=pltpu.SMEM),
        const((LANES, GLA_KEY_WIDTH)),
        const((1, GLA_KEY_WIDTH)),
        const((1, GLA_DV)),
        const((ATT_WIDTH, D_MODEL)),
        const((GLA_WIDTH, D_MODEL)),
        const((D_MODEL, D_MODEL)),
    ]
    return pl.pallas_call(
        _layer_body,
        grid=(BATCH, N_BLOCKS),
        in_specs=[prompt_rows, per_step((TS, D_MODEL)),
                  pl.BlockSpec((MOD_BLOCK, 3 * D_MODEL), lambda b, j: (DEC_BATCH // MOD_BLOCK, 0)),
                  pl.BlockSpec((MOD_BLOCK, 3 * D_MODEL), lambda b, j: ((b * N_BLOCKS + j) * GS // MOD_BLOCK, 0)),
                  table_rows, table_rows, const((TS, LANES)), const((TS, LANES))]
                 + weight_specs
                 + [per_step((GS, KV_WIDTH, WINDOW)), per_step((GS, KV_WIDTH, WINDOW)),
                    per_step((GS, GLA_KEY_WIDTH, GLA_DV))],
        out_specs=[prompt_rows, per_step((TS, D_MODEL)),
                   per_batch((KV_WIDTH, WINDOW)), per_batch((KV_WIDTH, WINDOW)), per_batch((GLA_KEY_WIDTH, GLA_DV)),
                   per_step((GS, KV_WIDTH, WINDOW)), per_step((GS, KV_WIDTH, WINDOW)),
                   per_step((GS, GLA_KEY_WIDTH, GLA_DV))],
        out_shape=[
            jax.ShapeDtypeStruct((BATCH, SEQ, D_MODEL), F32),
            jax.ShapeDtypeStruct((DEC_BATCH * DEC_SEQ, D_MODEL), F32),
            jax.ShapeDtypeStruct((BATCH, KV_WIDTH, WINDOW), F32),
            jax.ShapeDtypeStruct((BATCH, KV_WIDTH, WINDOW), F32),
            jax.ShapeDtypeStruct((BATCH, GLA_KEY_WIDTH, GLA_DV), F32),
            jax.ShapeDtypeStruct((DEC_BATCH, KV_WIDTH, WINDOW), F32),
            jax.ShapeDtypeStruct((DEC_BATCH, KV_WIDTH, WINDOW), F32),
            jax.ShapeDtypeStruct((DEC_BATCH, GLA_KEY_WIDTH, GLA_DV), F32),
        ],
        scratch_shapes=[
            pltpu.VMEM((ATT_BLOCK, KV_WIDTH), BF16),
            pltpu.VMEM((ATT_BLOCK, KV_WIDTH), BF16),
            pltpu.VMEM((GLA_DV, GLA_KEY_WIDTH), F32),
            pltpu.VMEM((WINDOW, KV_WIDTH), F32),
            pltpu.VMEM((WINDOW, KV_WIDTH), F32),
        ],
        compiler_params=pltpu.CompilerParams(dimension_semantics=("arbitrary", "arbitrary"),
                                             vmem_limit_bytes=VMEM_LIMIT),
        name="fused_layer",
    )(xp, xs, mod, mod, cos_p, sin_p, cos_s, sin_s, *weights, ck, cv, s0)


def _rope_tables(pos):
    half = ATT_HEAD_DIM // 2
    inv = 1.0 / (ROPE_THETA ** (np.arange(half, dtype=np.float64) / half))
    ang = pos.astype(np.float64)[:, None] * inv[None, :]
    c, s = np.cos(ang), np.sin(ang)
    return (np.tile(c, (1, 4)).astype(np.float32),
            np.concatenate([-s, s, -s, s], axis=1).astype(np.float32))


def kernel(x_prompt, x_sample, cache_win_k, cache_win_v, state_gla, c_prompt, c_sample, norm_g, w_ada, b_ada, w_in, q_norm_g, k_norm_g, attn_sinks, w_gla_gate, b_gla_gate, gla_norm_g, w_branch_att, w_branch_gla, w_out):
    assert w_in.shape == (1, D_MODEL, R_END), "single-layer trunk"
    wt = jnp.swapaxes(w_in[0], 0, 1)
    wgate = jnp.concatenate([w_gla_gate[0], jnp.zeros((LANES - GLA_GATE_RANK, GLA_KEY_WIDTH), F32)],
                            axis=0).astype(BF16)
    mod, wq16, wza16, wba16, wbg16, wout16 = _prepare(c_sample, c_prompt, w_ada[0], b_ada[0][None, :], wt,
                                                      w_branch_att[0], w_branch_gla[0], w_out[0])
    weights = (
        norm_g[0][None, :], wq16, wza16, wt.astype(BF16),
        jnp.tile(q_norm_g[0], 2)[None, :], jnp.tile(k_norm_g[0], 2)[None, :],
        attn_sinks[0], wgate, b_gla_gate[0][None, :], gla_norm_g[0][None, :], wba16, wbg16, wout16,
    )

    cos_p, sin_p = _rope_tables(np.arange(SEQ))
    cos_s, sin_s = _rope_tables(PAST_LEN + np.arange(DEC_SEQ))
    cos_s, sin_s = np.tile(cos_s, (SEQ_GROUP, 1)), np.tile(sin_s, (SEQ_GROUP, 1))

    def kv_rows(c):
        return jnp.transpose(c, (0, 2, 3, 1)).reshape(c.shape[0], KV_WIDTH, WINDOW)

    def kv_out(c):
        return jnp.transpose(c.reshape(c.shape[0], ATT_KV_HEADS, ATT_HEAD_DIM, WINDOW), (0, 3, 1, 2))[None]

    y_p, y_s, wk_p, wv_p, st_p, wk_s, wv_s, st_s = _fused_layer(
        x_prompt, x_sample.reshape(DEC_BATCH * DEC_SEQ, D_MODEL), mod, cos_p, sin_p, cos_s, sin_s, weights,
        kv_rows(cache_win_k[0]), kv_rows(cache_win_v[0]),
        state_gla[0].reshape(DEC_BATCH, GLA_KEY_WIDTH, GLA_DV))

    st_shape = (1, -1, GLA_HEADS, GLA_DK, GLA_DV)
    return (y_p, y_s.reshape(DEC_BATCH, DEC_SEQ, D_MODEL),
            kv_out(wk_p), kv_out(wv_p), st_p.reshape(st_shape),
            kv_out(wk_s), kv_out(wv_s), st_s.reshape(st_shape))
```
